```python
import jax, jax.numpy as jnp
from jax import lax
import numpy as np

D_MODEL = 1024
BATCH = 8
SEQ = 4096
DEPTH = 2

N_HEADS = 16
HEAD_DIM = D_MODEL // N_HEADS
N_MIXERS = 2
MOBA_BLOCK = 256
MOBA_TOPK = 3
MOBA_Q_CHUNK = 64
SB_Q_BLOCK = 128
NORM_EPS = 1e-6

kernel_name = "moba_stickbreaking_interleaved_hybrid"


def rms_norm(x, g):
    xf = x.astype(jnp.float32)
    y = xf * lax.rsqrt(jnp.mean(xf * xf, axis=-1, keepdims=True) + NORM_EPS)
    return (y * g.astype(jnp.float32)).astype(x.dtype)


def alibi_slopes(n_heads):
    return jnp.asarray(2.0 ** (-8.0 * np.arange(1, n_heads + 1) / n_heads), dtype=jnp.float32)


def moba_attention(q, k, v):
    b, h, s, dh = q.shape
    n_blk = -(-s // MOBA_BLOCK)
    s_pad = n_blk * MOBA_BLOCK
    pad = ((0, 0), (0, 0), (0, s_pad - s), (0, 0))
    qp, kp, vp = jnp.pad(q, pad), jnp.pad(k, pad), jnp.pad(v, pad)
    k_blk = kp.reshape(b, h, n_blk, MOBA_BLOCK, dh)
    v_blk = vp.reshape(b, h, n_blk, MOBA_BLOCK, dh)
    k_mean = jnp.mean(k_blk.astype(jnp.float32), axis=3)
    n_sel = min(MOBA_TOPK, n_blk - 1)
    n_chunks = s_pad // MOBA_Q_CHUNK
    slopes = alibi_slopes(h)
    scale = dh ** -0.5
    blk_ar = jnp.arange(MOBA_BLOCK)

    def one_chunk(idx):
        bi = idx // n_chunks
        q0 = (idx % n_chunks) * MOBA_Q_CHUNK
        qf = lax.dynamic_slice_in_dim(qp[bi], q0, MOBA_Q_CHUNK, axis=1).astype(jnp.float32)
        cur = q0 // MOBA_BLOCK
        t_pos = q0 + jnp.arange(MOBA_Q_CHUNK)
        kb, vb = k_blk[bi], v_blk[bi]
        own_k = lax.dynamic_index_in_dim(kb, cur, axis=1, keepdims=False).astype(jnp.float32)
        own_v = lax.dynamic_index_in_dim(vb, cur, axis=1, keepdims=False).astype(jnp.float32)
        s_own = cur * MOBA_BLOCK + blk_ar
        dist_own = (t_pos[:, None] - s_own[None, :]).astype(jnp.float32)
        logit_own = jnp.einsum('hqd,hkd->hqk', qf, own_k) * scale - slopes[:, None, None] * dist_own[None]
        logit_own = jnp.where((s_own[None, :] <= t_pos[:, None])[None], logit_own, -jnp.inf)
        if n_sel == 0:
            probs = jax.nn.softmax(logit_own, axis=-1)
            return jnp.einsum('hqk,hkd->hqd', probs, own_v)
        gate = jnp.einsum('hqd,hnd->hqn', qf, k_mean[bi])
        gate = jnp.where((jnp.arange(n_blk) < cur)[None, None, :], gate, -jnp.inf)
        _, sel = lax.top_k(gate, n_sel)
        sel_valid = sel < cur
        k_sel = jax.vmap(lambda kh, ih: kh[ih])(kb, sel).astype(jnp.float32)
        v_sel = jax.vmap(lambda vh, ih: vh[ih])(vb, sel).astype(jnp.float32)
        s_sel = sel[..., None] * MOBA_BLOCK + blk_ar
        dist_sel = (t_pos[None, :, None, None] - s_sel).astype(jnp.float32)
        logit_sel = jnp.einsum('hqd,hqnkd->hqnk', qf, k_sel) * scale - slopes[:, None, None, None] * dist_sel
        logit_sel = jnp.where(sel_valid[..., None], logit_sel, -jnp.inf)
        n_past = n_sel * MOBA_BLOCK
        logits = jnp.concatenate([logit_sel.reshape(h, MOBA_Q_CHUNK, n_past), logit_own], axis=-1)
        probs = jax.nn.softmax(logits, axis=-1)
        p_sel = probs[..., :n_past].reshape(h, MOBA_Q_CHUNK, n_sel, MOBA_BLOCK)
        p_own = probs[..., n_past:]
        return (jnp.einsum('hqnk,hqnkd->hqd', p_sel, v_sel)
                + jnp.einsum('hqk,hkd->hqd', p_own, own_v))

    outs = lax.map(one_chunk, jnp.arange(b * n_chunks))
    outs = outs.reshape(b, n_chunks, h, MOBA_Q_CHUNK, dh).transpose(0, 2, 1, 3, 4)
    return outs.reshape(b, h, s_pad, dh)[:, :, :s].astype(q.dtype)


def stick_breaking_attention(q, k, v):
    b, h, s, dh = q.shape
    n_qb = s // SB_Q_BLOCK
    scale = dh ** -0.5
    s_pos = jnp.arange(s)

    def one_block(idx):
        bi = idx // n_qb
        q0 = (idx % n_qb) * SB_Q_BLOCK
        qf = lax.dynamic_slice_in_dim(q[bi], q0, SB_Q_BLOCK, axis=1).astype(jnp.float32)
        z = jnp.einsum('hqd,hkd->hqk', qf, k[bi].astype(jnp.float32)) * scale
        t_pos = q0 + jnp.arange(SB_Q_BLOCK)
        strict = (s_pos[None, :] < t_pos[:, None])[None]
        log_1m_beta = jnp.where(strict, jax.nn.log_sigmoid(-z), 0.0)
        suffix = lax.cumsum(log_1m_beta, axis=2, reverse=True) - log_1m_beta
        w = jnp.where(strict, jnp.exp(jax.nn.log_sigmoid(z) + suffix), 0.0)
        return jnp.einsum('hqk,hkd->hqd', w, v[bi].astype(jnp.float32))

    outs = lax.map(one_block, jnp.arange(b * n_qb))
    outs = outs.reshape(b, n_qb, h, SB_Q_BLOCK, dh).transpose(0, 2, 1, 3, 4)
    return outs.reshape(b, h, s, dh).astype(q.dtype)


def mixer_branch(hn, w_in, w_out, mixer_id):
    b, s, d = hn.shape
    proj = jnp.einsum('bsd,de->bse', hn, w_in)
    q, k, v, z = jnp.split(proj, 4, axis=-1)
    to_heads = lambda t: t.reshape(b, s, N_HEADS, HEAD_DIM).transpose(0, 2, 1, 3)
    if mixer_id == 0:
        o = moba_attention(to_heads(q), to_heads(k), to_heads(v))
    else:
        o = stick_breaking_attention(to_heads(q), to_heads(k), to_heads(v))
    o = o.transpose(0, 2, 1, 3).reshape(b, s, d)
    return jnp.einsum('bse,ed->bsd', o * jax.nn.silu(z), w_out)


def setup_inputs(seed: int = 0) -> dict:
    key = jax.random.key(seed)
    k_x, k_g, k_in, k_out, k_f = jax.random.split(key, 5)
    x = jax.random.normal(k_x, (BATCH, SEQ, D_MODEL), jnp.float32)
    norm_g = 1.0 + 0.02 * jax.random.normal(k_g, (DEPTH, D_MODEL), jnp.float32)
    w_in = jax.random.normal(k_in, (DEPTH, D_MODEL, 4 * D_MODEL), jnp.float32) * D_MODEL ** -0.5
    w_out = jax.random.normal(k_out, (DEPTH, D_MODEL, D_MODEL), jnp.float32) * D_MODEL ** -0.5
    final_g = 1.0 + 0.02 * jax.random.normal(k_f, (D_MODEL,), jnp.float32)
    return {"x": x, "norm_g": norm_g, "w_in": w_in, "w_out": w_out, "final_g": final_g}


def reference(x, norm_g, w_in, w_out, final_g):
    h = x
    for i in range(DEPTH):
        h = h + mixer_branch(rms_norm(h, norm_g[i]), w_in[i], w_out[i], i % N_MIXERS)
    return rms_norm(h, final_g)
```

```python
import functools

import numpy as np
import jax
import jax.numpy as jnp
from jax import lax
from jax.experimental import pallas as pl
from jax.experimental.pallas import tpu as pltpu

N_HEADS = 16
HEAD_DIM = 64
MOBA_BLOCK = 256
MOBA_TOPK = 3
NORM_EPS = 1e-6

LANES = 128
HEADS_PER_STEP = LANES // HEAD_DIM
ATT_TILE = MOBA_BLOCK
PROJ_ROWS = 512
MASKED = -1e30
SB_SUFFIX_CUTOFF = 128.0
VMEM_LIMIT = 48 * 1024 * 1024


def _dot(a, b):
    return jnp.dot(a, b, preferred_element_type=jnp.float32)


def _dot_nt(a, b):
    return lax.dot_general(a, b, (((1,), (1,)), ((), ())), preferred_element_type=jnp.float32)


def _proj_kernel(x_ref, g_ref, wtok_ref, wfeat_ref, k_ref, z_ref, qt_ref, vt_ref, *, d, scale):
    x = x_ref[...]
    y = x * lax.rsqrt(jnp.mean(x * x, axis=-1, keepdims=True) + NORM_EPS) * g_ref[...]
    xn = y.astype(jnp.bfloat16)
    tok = _dot(xn, wtok_ref[...])
    k_ref[...] = tok[:, :d].astype(k_ref.dtype)
    z_ref[...] = tok[:, d:].astype(z_ref.dtype)
    feat = _dot_nt(wfeat_ref[...], xn)
    qt_ref[0] = (feat[:d] * scale).astype(qt_ref.dtype)
    for c in range(vt_ref.shape[1]):
        vt_ref[0, c] = feat[d:, c * ATT_TILE:(c + 1) * ATT_TILE].astype(vt_ref.dtype)


def _project(h, g, w_tok, w_feat_t, b, s, d):
    rows = PROJ_ROWS
    steps_per_batch = s // rows
    blocks_per_step = rows // ATT_TILE
    n_blk = s // ATT_TILE
    kern = functools.partial(_proj_kernel, d=d, scale=HEAD_DIM ** -0.5)
    return pl.pallas_call(
        kern,
        grid=(b, steps_per_batch),
        in_specs=[
            pl.BlockSpec((rows, d), lambda i, j: (i * steps_per_batch + j, 0)),
            pl.BlockSpec((1, d), lambda i, j: (0, 0)),
            pl.BlockSpec((d, 2 * d), lambda i, j: (0, 0)),
            pl.BlockSpec((2 * d, d), lambda i, j: (0, 0)),
        ],
        out_specs=[
            pl.BlockSpec((rows, d), lambda i, j: (i * steps_per_batch + j, 0)),
            pl.BlockSpec((rows, d), lambda i, j: (i * steps_per_batch + j, 0)),
            pl.BlockSpec((1, d, rows), lambda i, j: (i, 0, j)),
            pl.BlockSpec((1, blocks_per_step, d, ATT_TILE), lambda i, j: (i, j, 0, 0)),
        ],
        out_shape=[
            jax.ShapeDtypeStruct((b * s, d), jnp.bfloat16),
            jax.ShapeDtypeStruct((b * s, d), jnp.bfloat16),
            jax.ShapeDtypeStruct((b, d, s), jnp.bfloat16),
            jax.ShapeDtypeStruct((b, n_blk, d, ATT_TILE), jnp.bfloat16),
        ],
        compiler_params=pltpu.CompilerParams(
            dimension_semantics=("arbitrary", "arbitrary"), vmem_limit_bytes=VMEM_LIMIT),
        name="rmsnorm_qkvz_proj",
    )(h, g, w_tok, w_feat_t)


def _outproj_kernel(o_ref, z_ref, h_ref, w_ref, out_ref):
    z = z_ref[...].astype(jnp.float32)
    gated = o_ref[...].astype(jnp.float32) * (z / (1.0 + jnp.exp(-z)))
    out_ref[...] = h_ref[...] + _dot(gated.astype(jnp.bfloat16), w_ref[...])


def _outproj_norm_kernel(o_ref, z_ref, h_ref, w_ref, g_ref, out_ref):
    z = z_ref[...].astype(jnp.float32)
    gated = o_ref[...].astype(jnp.float32) * (z / (1.0 + jnp.exp(-z)))
    h = h_ref[...] + _dot(gated.astype(jnp.bfloat16), w_ref[...])
    out_ref[...] = h * lax.rsqrt(jnp.mean(h * h, axis=-1, keepdims=True) + NORM_EPS) * g_ref[...]


def _out_project(o, z, h, w_out, final_g=None):
    n, d = h.shape
    rows = PROJ_ROWS
    row_spec = pl.BlockSpec((rows, d), lambda i: (i, 0))
    in_specs = [row_spec, row_spec, row_spec, pl.BlockSpec((d, d), lambda i: (0, 0))]
    args = [o, z, h, w_out]
    kern = _outproj_kernel
    if final_g is not None:
        in_specs.append(pl.BlockSpec((1, d), lambda i: (0, 0)))
        args.append(final_g)
        kern = _outproj_norm_kernel
    return pl.pallas_call(
        kern,
        grid=(n // rows,),
        in_specs=in_specs,
        out_specs=row_spec,
        out_shape=jax.ShapeDtypeStruct((n, d), jnp.float32),
        compiler_params=pltpu.CompilerParams(
            dimension_semantics=("arbitrary",), vmem_limit_bytes=VMEM_LIMIT),
        name="gate_outproj_residual" if final_g is None else "gate_outproj_residual_norm",
    )(*args)


def _head_masked_q(qt_ref):
    qt = qt_ref[0]
    row = lax.broadcasted_iota(jnp.int32, qt.shape, 0)
    return [jnp.where((row >= hh * HEAD_DIM) & (row < (hh + 1) * HEAD_DIM), qt, jnp.zeros_like(qt))
            for hh in range(HEADS_PER_STEP)]


def _store_heads(o_ref, outs_t):
    stacked = jnp.concatenate(outs_t, axis=0)
    o_ref[0] = stacked.T.astype(o_ref.dtype)


def _attention_call(kern, qt, k, vt, extra_inputs, extra_specs, scratch_shapes, name):
    b, d, s = qt.shape
    n_blk = s // ATT_TILE
    return pl.pallas_call(
        kern,
        grid=(b, d // LANES, n_blk),
        in_specs=extra_specs + [
            pl.BlockSpec((1, LANES, ATT_TILE), lambda i, j, t: (i, j, t)),
            pl.BlockSpec((1, s, LANES), lambda i, j, t: (i, 0, j)),
            pl.BlockSpec((1, n_blk, LANES, ATT_TILE), lambda i, j, t: (i, 0, j, 0)),
        ],
        out_specs=pl.BlockSpec((1, ATT_TILE, LANES), lambda i, j, t: (i, t, j)),
        out_shape=jax.ShapeDtypeStruct((b, s, d), jnp.bfloat16),
        scratch_shapes=scratch_shapes,
        compiler_params=pltpu.CompilerParams(
            dimension_semantics=("arbitrary", "arbitrary", "arbitrary"),
            vmem_limit_bytes=VMEM_LIMIT),
        name=name,
    )(*extra_inputs, qt, k, vt)


def _moba_kernel(slopes_ref, qt_ref, k_ref, vt_ref, o_ref, kmean_ref, maskbias_ref):
    hp = pl.program_id(1)
    qi = pl.program_id(2)
    n_blk = kmean_ref.shape[0]
    tile = ATT_TILE

    @pl.when(qi == 0)
    def _():
        kf = k_ref[0].astype(jnp.float32).reshape(n_blk, tile, LANES)
        kmean_ref[...] = jnp.mean(kf, axis=1)

    qm = _head_masked_q(qt_ref)
    key_pos = lax.broadcasted_iota(jnp.int32, (tile, tile), 0)
    qry_pos = lax.broadcasted_iota(jnp.int32, (tile, tile), 1)
    rel = (qry_pos - key_pos).astype(jnp.float32)
    slopes = [slopes_ref[hp * HEADS_PER_STEP + hh] for hh in range(HEADS_PER_STEP)]

    kmean = kmean_ref[...]
    kmean_hi = kmean.astype(jnp.bfloat16)
    kmean_lo = (kmean - kmean_hi.astype(jnp.float32)).astype(jnp.bfloat16)
    blk = lax.broadcasted_iota(jnp.int32, (n_blk, tile), 0).astype(jnp.float32)
    qi_f = qi.astype(jnp.float32)
    for hh in range(HEADS_PER_STEP):
        gate = _dot(kmean_hi, qm[hh]) + _dot(kmean_lo, qm[hh])
        gate = jnp.where(blk < qi_f, gate, -jnp.inf)
        chosen = jnp.zeros(gate.shape, jnp.bool_)
        for _ in range(MOBA_TOPK):
            top = jnp.max(gate, axis=0, keepdims=True)
            at_top = (gate == top) & (top > -jnp.inf)
            first = jnp.min(jnp.where(at_top, blk, float(n_blk)), axis=0, keepdims=True)
            pick = blk == first
            chosen = jnp.logical_or(chosen, pick)
            gate = jnp.where(pick, -jnp.inf, gate)
        maskbias_ref[hh] = jnp.where(chosen, 0.0, MASKED)

    k_own = k_ref[0, pl.ds(pl.multiple_of(qi * tile, tile), tile), :]
    state = []
    for hh in range(HEADS_PER_STEP):
        logit = _dot(k_own, qm[hh]) - slopes[hh] * rel
        logit = jnp.where(key_pos <= qry_pos, logit, MASKED)
        m = jnp.max(logit, axis=0, keepdims=True)
        p = jnp.exp(logit - m)
        l = jnp.sum(p, axis=0, keepdims=True)
        v_own = vt_ref[0, qi, hh * HEAD_DIM:(hh + 1) * HEAD_DIM, :]
        acc = _dot(v_own, p.astype(jnp.bfloat16))
        state += [m, l, acc]

    def past_block(j, carry):
        k_blk = k_ref[0, pl.ds(pl.multiple_of(j * tile, tile), tile), :]
        offset = ((qi - j) * tile).astype(jnp.float32)
        out = []
        for hh in range(HEADS_PER_STEP):
            m, l, acc = carry[3 * hh:3 * hh + 3]
            row_bias = maskbias_ref[hh, pl.ds(j, 1), :] - slopes[hh] * offset
            logit = _dot(k_blk, qm[hh]) - slopes[hh] * rel + row_bias
            m_new = jnp.maximum(m, jnp.max(logit, axis=0, keepdims=True))
            p = jnp.exp(logit - m_new)
            alpha = jnp.exp(m - m_new)
            l = alpha * l + jnp.sum(p, axis=0, keepdims=True)
            v_blk = vt_ref[0, j, hh * HEAD_DIM:(hh + 1) * HEAD_DIM, :]
            acc = alpha * acc + _dot(v_blk, p.astype(jnp.bfloat16))
            out += [m_new, l, acc]
        return tuple(out)

    state = lax.fori_loop(0, qi, past_block, tuple(state))
    _store_heads(o_ref, [state[3 * hh + 2] / state[3 * hh + 1] for hh in range(HEADS_PER_STEP)])


def _moba_attention(qt, k, vt, slopes):
    n_blk = qt.shape[2] // ATT_TILE
    return _attention_call(
        _moba_kernel, qt, k, vt,
        extra_inputs=[slopes],
        extra_specs=[pl.BlockSpec(memory_space=pltpu.SMEM)],
        scratch_shapes=[
            pltpu.VMEM((n_blk, LANES), jnp.float32),
            pltpu.VMEM((HEADS_PER_STEP, n_blk, ATT_TILE), jnp.float32),
        ],
        name="moba_attention",
    )


def _softplus(z):
    return jnp.maximum(z, 0.0) + jnp.log(1.0 + jnp.exp(-jnp.abs(z)))


def _sb_kernel(later_ref, qt_ref, k_ref, vt_ref, o_ref):
    qi = pl.program_id(2)
    tile = ATT_TILE
    qm = _head_masked_q(qt_ref)
    later = later_ref[...]
    key_pos = lax.broadcasted_iota(jnp.int32, (tile, tile), 0)
    qry_pos = lax.broadcasted_iota(jnp.int32, (tile, tile), 1)

    def block(j, suffix, acc, hh, strict=None):
        k_blk = k_ref[0, pl.ds(pl.multiple_of(j * tile, tile), tile), :]
        z = _dot(k_blk, qm[hh])
        sp = _softplus(z)
        if strict is not None:
            sp = jnp.where(strict, sp, 0.0)
        sp_hi = sp.astype(jnp.bfloat16)
        sp_lo = (sp - sp_hi.astype(jnp.float32)).astype(jnp.bfloat16)
        inner = _dot(later, sp_hi) + _dot(later, sp_lo)
        w = jnp.exp(z - sp - inner - suffix)
        if strict is not None:
            w = jnp.where(strict, w, 0.0)
        v_blk = vt_ref[0, j, hh * HEAD_DIM:(hh + 1) * HEAD_DIM, :]
        acc = acc + _dot(v_blk, w.astype(jnp.bfloat16))
        return suffix + jnp.sum(sp, axis=0, keepdims=True), acc

    strict = key_pos < qry_pos
    state = []
    for hh in range(HEADS_PER_STEP):
        suffix, acc = block(qi, jnp.zeros((1, tile), jnp.float32),
                            jnp.zeros((HEAD_DIM, tile), jnp.float32), hh, strict)
        state += [suffix, acc]

    def min_suffix(st):
        return functools.reduce(jnp.minimum, [jnp.min(st[2 * hh]) for hh in range(HEADS_PER_STEP)])

    def cond(carry):
        j, smallest = carry[0], carry[1]
        return jnp.logical_and(j >= 0, smallest <= SB_SUFFIX_CUTOFF)

    def body(carry):
        j = carry[0]
        st = carry[2:]
        out = []
        for hh in range(HEADS_PER_STEP):
            suffix, acc = block(j, st[2 * hh], st[2 * hh + 1], hh)
            out += [suffix, acc]
        return (j - 1, min_suffix(out)) + tuple(out)

    final = lax.while_loop(cond, body, (qi - 1, min_suffix(state)) + tuple(state))
    _store_heads(o_ref, [final[2 + 2 * hh + 1] for hh in range(HEADS_PER_STEP)])


def _sb_attention(qt, k, vt):
    later = jnp.asarray(np.triu(np.ones((ATT_TILE, ATT_TILE), np.float32), 1), jnp.bfloat16)
    return _attention_call(
        _sb_kernel, qt, k, vt,
        extra_inputs=[later],
        extra_specs=[pl.BlockSpec((ATT_TILE, ATT_TILE), lambda i, j, t: (0, 0))],
        scratch_shapes=[],
        name="stick_breaking_attention",
    )


def kernel(x, norm_g, w_in, w_out, final_g):
    b, s, d = x.shape
    depth = norm_g.shape[0]
    assert depth >= 1 and d == N_HEADS * HEAD_DIM and s % PROJ_ROWS == 0 and PROJ_ROWS % ATT_TILE == 0
    slopes = jnp.asarray(2.0 ** (-8.0 * np.arange(1, N_HEADS + 1) / N_HEADS), jnp.float32)
    h = x.reshape(b * s, d)
    for i in range(depth):
        wq, wk, wv, wz = (w_in[i, :, c * d:(c + 1) * d] for c in range(4))
        w_tok = jnp.concatenate([wk, wz], axis=1).astype(jnp.bfloat16)
        w_feat_t = jnp.concatenate([wq, wv], axis=1).T.astype(jnp.bfloat16)
        k, z, qt, vt = _project(h, norm_g[i].reshape(1, d), w_tok, w_feat_t, b, s, d)
        k3 = k.reshape(b, s, d)
        if i % 2 == 0:
            o = _moba_attention(qt, k3, vt, slopes)
        else:
            o = _sb_attention(qt, k3, vt)
        last = i == depth - 1
        h = _out_project(o.reshape(b * s, d), z, h, w_out[i].astype(jnp.bfloat16),
                         final_g.reshape(1, d) if last else None)
    return h.reshape(b, s, d)
```

```python
import functools

import numpy as np
import jax
import jax.numpy as jnp
from jax import lax
from jax.experimental import pallas as pl
from jax.experimental.pallas import tpu as pltpu

N_HEADS = 16
HEAD_DIM = 64
MOBA_BLOCK = 256
MOBA_TOPK = 3
NORM_EPS = 1e-6

LANES = 128
ATT_TILE = MOBA_BLOCK
PROJ_ROWS = 512
MOBA_HEADS = 2
SB_HEADS = 2
MASKED = -1e30
SB_SUFFIX_CUTOFF = 128.0
VMEM_LIMIT = 48 * 1024 * 1024


def _dot(a, b):
    return jnp.dot(a, b, preferred_element_type=jnp.float32)


def _dot_nt(a, b):
    return lax.dot_general(a, b, (((1,), (1,)), ((), ())), preferred_element_type=jnp.float32)


def _proj_kernel(x_ref, g_ref, wtok_ref, wfeat_ref, k_ref, z_ref, qt_ref, vt_ref, *, d, scale):
    x = x_ref[...]
    y = x * lax.rsqrt(jnp.mean(x * x, axis=-1, keepdims=True) + NORM_EPS) * g_ref[...]
    xn = y.astype(jnp.bfloat16)
    tok = _dot(xn, wtok_ref[...])
    k_ref[...] = tok[:, :d].astype(k_ref.dtype)
    z_ref[...] = tok[:, d:].astype(z_ref.dtype)
    feat = _dot_nt(wfeat_ref[...], xn)
    qt_ref[0] = (feat[:d] * scale).astype(qt_ref.dtype)
    for c in range(vt_ref.shape[1]):
        vt_ref[0, c] = feat[d:, c * ATT_TILE:(c + 1) * ATT_TILE].astype(vt_ref.dtype)


def _project(h, g, w_tok, w_feat_t, b, s, d):
    rows = PROJ_ROWS
    steps_per_batch = s // rows
    blocks_per_step = rows // ATT_TILE
    n_blk = s // ATT_TILE
    kern = functools.partial(_proj_kernel, d=d, scale=HEAD_DIM ** -0.5)
    return pl.pallas_call(
        kern,
        grid=(b, steps_per_batch),
        in_specs=[
            pl.BlockSpec((rows, d), lambda i, j: (i * steps_per_batch + j, 0)),
            pl.BlockSpec((1, d), lambda i, j: (0, 0)),
            pl.BlockSpec((d, 2 * d), lambda i, j: (0, 0)),
            pl.BlockSpec((2 * d, d), lambda i, j: (0, 0)),
        ],
        out_specs=[
            pl.BlockSpec((rows, d), lambda i, j: (i * steps_per_batch + j, 0)),
            pl.BlockSpec((rows, d), lambda i, j: (i * steps_per_batch + j, 0)),
            pl.BlockSpec((1, d, rows), lambda i, j: (i, 0, j)),
            pl.BlockSpec((1, blocks_per_step, d, ATT_TILE), lambda i, j: (i, j, 0, 0)),
        ],
        out_shape=[
            jax.ShapeDtypeStruct((b * s, d), jnp.bfloat16),
            jax.ShapeDtypeStruct((b * s, d), jnp.bfloat16),
            jax.ShapeDtypeStruct((b, d, s), jnp.bfloat16),
            jax.ShapeDtypeStruct((b, n_blk, d, ATT_TILE), jnp.bfloat16),
        ],
        compiler_params=pltpu.CompilerParams(
            dimension_semantics=("arbitrary", "arbitrary"), vmem_limit_bytes=VMEM_LIMIT),
        name="rmsnorm_qkvz_proj",
    )(h.reshape(b * s, d), g, w_tok, w_feat_t)


def _outproj_kernel(*refs, n_o, half_tiles, has_norm):
    o_refs, (z_ref, h_ref, w_ref) = refs[:n_o], refs[n_o:n_o + 3]
    out_ref = refs[-1]
    if n_o == 1:
        o = o_refs[0][0]
    else:
        o = jnp.where(pl.program_id(1) < half_tiles, o_refs[0][0], o_refs[1][0])
    z = z_ref[0].astype(jnp.float32)
    gated = o.astype(jnp.float32) * (z / (1.0 + jnp.exp(-z)))
    h = h_ref[0] + _dot(gated.astype(jnp.bfloat16), w_ref[...])
    if has_norm:
        g_ref = refs[n_o + 3]
        h = h * lax.rsqrt(jnp.mean(h * h, axis=-1, keepdims=True) + NORM_EPS) * g_ref[...]
    out_ref[0] = h


def _out_project(o_parts, z, h, w_out, final_g=None):
    b, s, d = h.shape
    rows = ATT_TILE
    n_tiles = s // rows
    half = n_tiles // 2
    row_spec = pl.BlockSpec((1, rows, d), lambda i, t: (i, t, 0))
    if len(o_parts) == 1:
        o_specs = [row_spec]
    else:
        o_specs = [
            pl.BlockSpec((1, rows, d), lambda i, t: (i, jnp.minimum(t, half - 1), 0)),
            pl.BlockSpec((1, rows, d), lambda i, t: (i, jnp.clip(n_tiles - 1 - t, 0, half - 1), 0)),
        ]
    in_specs = o_specs + [row_spec, row_spec, pl.BlockSpec((d, d), lambda i, t: (0, 0))]
    args = list(o_parts) + [z, h, w_out]
    if final_g is not None:
        in_specs.append(pl.BlockSpec((1, d), lambda i, t: (0, 0)))
        args.append(final_g)
    kern = functools.partial(_outproj_kernel, n_o=len(o_parts), half_tiles=half,
                             has_norm=final_g is not None)
    return pl.pallas_call(
        kern,
        grid=(b, n_tiles),
        in_specs=in_specs,
        out_specs=row_spec,
        out_shape=jax.ShapeDtypeStruct((b, s, d), jnp.float32),
        compiler_params=pltpu.CompilerParams(
            dimension_semantics=("arbitrary", "arbitrary"), vmem_limit_bytes=VMEM_LIMIT),
        name="gate_outproj_residual" if final_g is None else "gate_outproj_residual_norm",
    )(*args)


def _head_masked(qt, hh):
    row = lax.broadcasted_iota(jnp.int32, qt.shape, 0)
    return jnp.where((row >= hh * HEAD_DIM) & (row < (hh + 1) * HEAD_DIM), qt, jnp.zeros_like(qt))


def _store_heads(o_ref, outs_t):
    stacked = jnp.concatenate(outs_t, axis=0)
    o_ref[0] = stacked.T.astype(o_ref.dtype)


def _moba_kernel(slopes_ref, qa_ref, qb_ref, k_ref, vt_ref, olo_ref, ohi_ref,
                 kmean_ref, bias_ref, qm_ref, maskbias_ref, s_ref, m_ref, l_ref, acc_ref, *, nh):
    hg = pl.program_id(1)
    p = pl.program_id(2)
    n_blk = kmean_ref.shape[0]
    tile = ATT_TILE
    q_tile = (p, n_blk - 1 - p)
    q_refs = (qa_ref, qb_ref)
    key_pos = lax.broadcasted_iota(jnp.int32, (tile, tile), 0)
    qry_pos = lax.broadcasted_iota(jnp.int32, (tile, tile), 1)
    slopes = [slopes_ref[hg * nh + hh] for hh in range(nh)]

    @pl.when(p == 0)
    def _():
        kf = k_ref[0].astype(jnp.float32).reshape(n_blk, tile, nh * HEAD_DIM)
        kmean_ref[...] = jnp.mean(kf, axis=1)
        rel = (qry_pos - key_pos).astype(jnp.float32)
        for hh in range(nh):
            bias_ref[hh] = -slopes[hh] * rel

    kmean = kmean_ref[...]
    kmean_hi = kmean.astype(jnp.bfloat16)
    kmean_lo = (kmean - kmean_hi.astype(jnp.float32)).astype(jnp.bfloat16)
    blk = lax.broadcasted_iota(jnp.int32, (n_blk, tile), 0).astype(jnp.float32)
    for sel in range(2):
        qt = q_refs[sel][0]
        n_past = q_tile[sel].astype(jnp.float32)
        for hh in range(nh):
            qm = _head_masked(qt, hh)
            qm_ref[sel, hh] = qm
            gate = _dot(kmean_hi, qm) + _dot(kmean_lo, qm)
            gate = jnp.where(blk < n_past, gate, -jnp.inf)
            chosen = jnp.zeros(gate.shape, jnp.bool_)
            for _ in range(MOBA_TOPK):
                top = jnp.max(gate, axis=0, keepdims=True)
                at_top = (gate == top) & (top > -jnp.inf)
                first = jnp.min(jnp.where(at_top, blk, float(n_blk)), axis=0, keepdims=True)
                pick = blk == first
                chosen = jnp.logical_or(chosen, pick)
                gate = jnp.where(pick, -jnp.inf, gate)
            maskbias_ref[sel, hh] = jnp.where(chosen, 0.0, MASKED)

    def key_block(j):
        return k_ref[0, pl.ds(pl.multiple_of(j * tile, tile), tile), :]

    def slot_of(s):
        is_b = s >= p
        return is_b.astype(jnp.int32), jnp.where(is_b, s - p, s), jnp.where(is_b, n_blk - 1 - p, p)

    causal = key_pos <= qry_pos
    for sel in range(2):
        k_own = key_block(q_tile[sel])
        for hh in range(nh):
            logit = _dot(k_own, qm_ref[sel, hh]) + jnp.where(causal, bias_ref[hh], MASKED)
            s_ref[n_blk - 1 + sel, hh] = logit
            m_ref[sel, hh] = jnp.max(logit, axis=0, keepdims=True)
    for s in range(n_blk - 1):
        sel, j, qt_idx = slot_of(s)
        k_blk = key_block(j)
        offset = ((qt_idx - j) * tile).astype(jnp.float32)
        for hh in range(nh):
            row_bias = maskbias_ref[sel, hh, pl.ds(j, 1), :] - slopes[hh] * offset
            logit = _dot(k_blk, qm_ref[sel, hh]) + bias_ref[hh] + row_bias
            s_ref[s, hh] = logit
            m_ref[sel, hh] = jnp.maximum(m_ref[sel, hh], jnp.max(logit, axis=0, keepdims=True))

    for sel in range(2):
        for hh in range(nh):
            prob = jnp.exp(s_ref[n_blk - 1 + sel, hh] - m_ref[sel, hh])
            l_ref[sel, hh] = jnp.sum(prob, axis=0, keepdims=True)
            v_own = vt_ref[0, q_tile[sel], hh * HEAD_DIM:(hh + 1) * HEAD_DIM, :]
            acc_ref[sel, hh] = _dot(v_own, prob.astype(jnp.bfloat16))
    for s in range(n_blk - 1):
        sel, j, _ = slot_of(s)
        for hh in range(nh):
            prob = jnp.exp(s_ref[s, hh] - m_ref[sel, hh])
            l_ref[sel, hh] += jnp.sum(prob, axis=0, keepdims=True)
            v_blk = vt_ref[0, j, hh * HEAD_DIM:(hh + 1) * HEAD_DIM, :]
            acc_ref[sel, hh] += _dot(v_blk, prob.astype(jnp.bfloat16))

    for sel, o_ref in enumerate((olo_ref, ohi_ref)):
        _store_heads(o_ref, [acc_ref[sel, hh] / l_ref[sel, hh] for hh in range(nh)])


def _moba_attention(qt, k, vt, slopes):
    b, d, s = qt.shape
    nh = MOBA_HEADS
    gw = nh * HEAD_DIM
    n_blk = s // ATT_TILE
    assert n_blk % 2 == 0
    half = jax.ShapeDtypeStruct((b, s // 2, d), jnp.bfloat16)
    return pl.pallas_call(
        functools.partial(_moba_kernel, nh=nh),
        grid=(b, d // gw, n_blk // 2),
        in_specs=[
            pl.BlockSpec(memory_space=pltpu.SMEM),
            pl.BlockSpec((1, gw, ATT_TILE), lambda i, j, t: (i, j, t)),
            pl.BlockSpec((1, gw, ATT_TILE), lambda i, j, t: (i, j, n_blk - 1 - t)),
            pl.BlockSpec((1, s, gw), lambda i, j, t: (i, 0, j)),
            pl.BlockSpec((1, n_blk, gw, ATT_TILE), lambda i, j, t: (i, 0, j, 0)),
        ],
        out_specs=[
            pl.BlockSpec((1, ATT_TILE, gw), lambda i, j, t: (i, t, j)),
            pl.BlockSpec((1, ATT_TILE, gw), lambda i, j, t: (i, t, j)),
        ],
        out_shape=[half, half],
        scratch_shapes=[
            pltpu.VMEM((n_blk, gw), jnp.float32),
            pltpu.VMEM((nh, ATT_TILE, ATT_TILE), jnp.float32),
            pltpu.VMEM((2, nh, gw, ATT_TILE), jnp.bfloat16),
            pltpu.VMEM((2, nh, n_blk, ATT_TILE), jnp.float32),
            pltpu.VMEM((n_blk + 1, nh, ATT_TILE, ATT_TILE), jnp.float32),
            pltpu.VMEM((2, nh, 1, ATT_TILE), jnp.float32),
            pltpu.VMEM((2, nh, 1, ATT_TILE), jnp.float32),
            pltpu.VMEM((2, nh, HEAD_DIM, ATT_TILE), jnp.float32),
        ],
        compiler_params=pltpu.CompilerParams(
            dimension_semantics=("arbitrary", "arbitrary", "arbitrary"),
            vmem_limit_bytes=VMEM_LIMIT),
        name="moba_attention",
    )(slopes, qt, qt, k, vt)


def _softplus(z):
    return jnp.maximum(z, 0.0) + jnp.log(1.0 + jnp.exp(-jnp.abs(z)))


def _sb_kernel(later_ref, qt_ref, k_ref, vt_ref, o_ref, *, nh):
    qi = pl.program_id(2)
    tile = ATT_TILE
    qt = qt_ref[0]
    qm = [_head_masked(qt, hh) for hh in range(nh)]
    later = later_ref[...]
    key_pos = lax.broadcasted_iota(jnp.int32, (tile, tile), 0)
    qry_pos = lax.broadcasted_iota(jnp.int32, (tile, tile), 1)

    def block(j, suffix, acc, hh, strict=None):
        k_blk = k_ref[0, pl.ds(pl.multiple_of(j * tile, tile), tile), :]
        z = _dot(k_blk, qm[hh])
        sp = _softplus(z)
        if strict is not None:
            sp = jnp.where(strict, sp, 0.0)
        sp_hi = sp.astype(jnp.bfloat16)
        sp_lo = (sp - sp_hi.astype(jnp.float32)).astype(jnp.bfloat16)
        inner = _dot(later, sp_hi) + _dot(later, sp_lo)
        w = jnp.exp(z - sp - inner - suffix)
        if strict is not None:
            w = jnp.where(strict, w, 0.0)
        v_blk = vt_ref[0, j, hh * HEAD_DIM:(hh + 1) * HEAD_DIM, :]
        acc = acc + _dot(v_blk, w.astype(jnp.bfloat16))
        return suffix + jnp.sum(sp, axis=0, keepdims=True), acc

    strict = key_pos < qry_pos
    state = []
    for hh in range(nh):
        suffix, acc = block(qi, jnp.zeros((1, tile), jnp.float32),
                            jnp.zeros((HEAD_DIM, tile), jnp.float32), hh, strict)
        state += [suffix, acc]

    def min_suffix(st):
        return functools.reduce(jnp.minimum, [jnp.min(st[2 * hh]) for hh in range(nh)])

    def cond(carry):
        j, smallest = carry[0], carry[1]
        return jnp.logical_and(j >= 0, smallest <= SB_SUFFIX_CUTOFF)

    def body(carry):
        j = carry[0]
        st = carry[2:]
        out = []
        for hh in range(nh):
            suffix, acc = block(j, st[2 * hh], st[2 * hh + 1], hh)
            out += [suffix, acc]
        return (j - 1, min_suffix(out)) + tuple(out)

    final = lax.while_loop(cond, body, (qi - 1, min_suffix(state)) + tuple(state))
    _store_heads(o_ref, [final[2 + 2 * hh + 1] for hh in range(nh)])


def _sb_attention(qt, k, vt):
    b, d, s = qt.shape
    nh = SB_HEADS
    gw = nh * HEAD_DIM
    n_blk = s // ATT_TILE
    later = jnp.asarray(np.triu(np.ones((ATT_TILE, ATT_TILE), np.float32), 1), jnp.bfloat16)
    return pl.pallas_call(
        functools.partial(_sb_kernel, nh=nh),
        grid=(b, d // gw, n_blk),
        in_specs=[
            pl.BlockSpec((ATT_TILE, ATT_TILE), lambda i, j, t: (0, 0)),
            pl.BlockSpec((1, gw, ATT_TILE), lambda i, j, t: (i, j, t)),
            pl.BlockSpec((1, s, gw), lambda i, j, t: (i, 0, j)),
            pl.BlockSpec((1, n_blk, gw, ATT_TILE), lambda i, j, t: (i, 0, j, 0)),
        ],
        out_specs=pl.BlockSpec((1, ATT_TILE, gw), lambda i, j, t: (i, t, j)),
        out_shape=jax.ShapeDtypeStruct((b, s, d), jnp.bfloat16),
        compiler_params=pltpu.CompilerParams(
            dimension_semantics=("arbitrary", "arbitrary", "arbitrary"),
            vmem_limit_bytes=VMEM_LIMIT),
        name="stick_breaking_attention",
    )(later, qt, k, vt)


def kernel(x, norm_g, w_in, w_out, final_g):
    b, s, d = x.shape
    depth = norm_g.shape[0]
    assert depth >= 1 and d == N_HEADS * HEAD_DIM and s % PROJ_ROWS == 0 and PROJ_ROWS % ATT_TILE == 0
    slopes = jnp.asarray(2.0 ** (-8.0 * np.arange(1, N_HEADS + 1) / N_HEADS), jnp.float32)
    h = x
    for i in range(depth):
        wq, wk, wv, wz = (w_in[i, :, c * d:(c + 1) * d] for c in range(4))
        w_tok = jnp.concatenate([wk, wz], axis=1).astype(jnp.bfloat16)
        w_feat_t = jnp.concatenate([wq, wv], axis=1).T.astype(jnp.bfloat16)
        k, z, qt, vt = _project(h, norm_g[i].reshape(1, d), w_tok, w_feat_t, b, s, d)
        k = k.reshape(b, s, d)
        if i % 2 == 0:
            o_parts = _moba_attention(qt, k, vt, slopes)
        else:
            o_parts = [_sb_attention(qt, k, vt)]
        last = i == depth - 1
        h = _out_project(o_parts, z.reshape(b, s, d), h, w_out[i].astype(jnp.bfloat16),
                         final_g.reshape(1, d) if last else None)
    return h
```

```python
import functools
import math

import numpy as np
import jax
import jax.numpy as jnp
from jax import lax
from jax.experimental import pallas as pl
from jax.experimental.pallas import tpu as pltpu

N_HEADS = 16
HEAD_DIM = 64
MOBA_BLOCK = 256
MOBA_TOPK = 3
NORM_EPS = 1e-6

LANES = 128
BF16_ROWS = 16
ATT_TILE = MOBA_BLOCK
PROJ_ROWS = 512
MOBA_HEADS = 2
SB_HEADS = 4
SOFTMAX_LAG = 2
LOG2E = math.log2(math.e)
MASKED = -1e30
SB_SUFFIX_CUTOFF = 128.0 * LOG2E
VMEM_LIMIT = 48 * 1024 * 1024


def _dot(a, b):
    return jnp.dot(a, b, preferred_element_type=jnp.float32)


def _dot_nt(a, b):
    return lax.dot_general(a, b, (((1,), (1,)), ((), ())), preferred_element_type=jnp.float32)


def _bf16_round(x):
    return x.astype(jnp.bfloat16).astype(jnp.float32)


def _proj_kernel(x_ref, g_ref, wtok_ref, wfeat_ref, k_ref, z_ref, qt_ref, vt_ref, *, d, scale):
    x = x_ref[...]
    y = x * lax.rsqrt(jnp.mean(x * x, axis=-1, keepdims=True) + NORM_EPS) * g_ref[...]
    xn = y.astype(jnp.bfloat16)
    tok = _dot(xn, wtok_ref[...])
    k_ref[...] = tok[:, :d].astype(k_ref.dtype)
    z_ref[...] = tok[:, d:].astype(z_ref.dtype)
    feat = _dot_nt(wfeat_ref[...], xn)
    qt_ref[0] = (feat[:d] * scale).astype(qt_ref.dtype)
    for c in range(vt_ref.shape[1]):
        vt_ref[0, c] = feat[d:, c * ATT_TILE:(c + 1) * ATT_TILE].astype(vt_ref.dtype)


def _project(h, g, w_tok, w_feat_t, b, s, d):
    rows = PROJ_ROWS
    steps_per_batch = s // rows
    blocks_per_step = rows // ATT_TILE
    n_blk = s // ATT_TILE
    kern = functools.partial(_proj_kernel, d=d, scale=HEAD_DIM ** -0.5 * LOG2E)
    return pl.pallas_call(
        kern,
        grid=(b, steps_per_batch),
        in_specs=[
            pl.BlockSpec((rows, d), lambda i, j: (i * steps_per_batch + j, 0)),
            pl.BlockSpec((1, d), lambda i, j: (0, 0)),
            pl.BlockSpec((d, 2 * d), lambda i, j: (0, 0)),
            pl.BlockSpec((2 * d, d), lambda i, j: (0, 0)),
        ],
        out_specs=[
            pl.BlockSpec((rows, d), lambda i, j: (i * steps_per_batch + j, 0)),
            pl.BlockSpec((rows, d), lambda i, j: (i * steps_per_batch + j, 0)),
            pl.BlockSpec((1, d, rows), lambda i, j: (i, 0, j)),
            pl.BlockSpec((1, blocks_per_step, d, ATT_TILE), lambda i, j: (i, j, 0, 0)),
        ],
        out_shape=[
            jax.ShapeDtypeStruct((b * s, d), jnp.bfloat16),
            jax.ShapeDtypeStruct((b * s, d), jnp.bfloat16),
            jax.ShapeDtypeStruct((b, d, s), jnp.bfloat16),
            jax.ShapeDtypeStruct((b, n_blk, d, ATT_TILE), jnp.bfloat16),
        ],
        compiler_params=pltpu.CompilerParams(
            dimension_semantics=("arbitrary", "arbitrary"), vmem_limit_bytes=VMEM_LIMIT),
        name="rmsnorm_qkvz_proj",
    )(h.reshape(b * s, d), g, w_tok, w_feat_t)


def _outproj_kernel(*refs, n_o, half_steps, has_norm):
    o_refs, (z_ref, h_ref, w_ref) = refs[:n_o], refs[n_o:n_o + 3]
    out_ref = refs[-1]
    if n_o == 1:
        o = o_refs[0][0]
    else:
        o = jnp.where(pl.program_id(1) < half_steps, o_refs[0][0], o_refs[1][0])
    z = z_ref[0].astype(jnp.float32)
    gated = o.astype(jnp.float32) * (z / (1.0 + jnp.exp(-z)))
    h = h_ref[0] + _dot(gated.astype(jnp.bfloat16), w_ref[...])
    if has_norm:
        g_ref = refs[n_o + 3]
        h = h * lax.rsqrt(jnp.mean(h * h, axis=-1, keepdims=True) + NORM_EPS) * g_ref[...]
    out_ref[0] = h


def _out_project(o_parts, z, h, w_out, final_g=None):
    b, s, d = h.shape
    rows = PROJ_ROWS
    n_steps = s // rows
    half = n_steps // 2
    row_spec = pl.BlockSpec((1, rows, d), lambda i, t: (i, t, 0))
    if len(o_parts) == 1:
        o_specs = [row_spec]
    else:
        o_specs = [
            pl.BlockSpec((1, rows, d), lambda i, t: (i, jnp.minimum(t, half - 1), 0)),
            pl.BlockSpec((1, rows, d), lambda i, t: (i, jnp.clip(n_steps - 1 - t, 0, half - 1), 0)),
        ]
    in_specs = o_specs + [row_spec, row_spec, pl.BlockSpec((d, d), lambda i, t: (0, 0))]
    args = list(o_parts) + [z, h, w_out]
    if final_g is not None:
        in_specs.append(pl.BlockSpec((1, d), lambda i, t: (0, 0)))
        args.append(final_g)
    kern = functools.partial(_outproj_kernel, n_o=len(o_parts), half_steps=half,
                             has_norm=final_g is not None)
    return pl.pallas_call(
        kern,
        grid=(b, n_steps),
        in_specs=in_specs,
        out_specs=row_spec,
        out_shape=jax.ShapeDtypeStruct((b, s, d), jnp.float32),
        compiler_params=pltpu.CompilerParams(
            dimension_semantics=("arbitrary", "arbitrary"), vmem_limit_bytes=VMEM_LIMIT),
        name="gate_outproj_residual" if final_g is None else "gate_outproj_residual_norm",
    )(*args)


def _head_masked(qt, hh):
    row = lax.broadcasted_iota(jnp.int32, qt.shape, 0)
    return jnp.where((row >= hh * HEAD_DIM) & (row < (hh + 1) * HEAD_DIM), qt, jnp.zeros_like(qt))


def _store_heads(o_ref, outs_t):
    stacked = jnp.concatenate(outs_t, axis=0)
    o_ref[0] = stacked.T.astype(o_ref.dtype)


def _moba_key_features(n_blk, n_feat):
    pos = np.arange(n_blk * ATT_TILE)
    feat = np.zeros((n_blk * ATT_TILE, n_feat), np.float32)
    feat[pos, pos // ATT_TILE] = 1.0
    feat[:, n_blk:n_blk + 3] = (pos // ATT_TILE)[:, None]
    feat[:, n_blk + 3:n_blk + 6] = (pos % ATT_TILE)[:, None]
    return jnp.asarray(feat, jnp.bfloat16)


def _moba_kernel(slopes_ref, kfeat_ref, qa_ref, qb_ref, k_ref, vt_ref, olo_ref, ohi_ref,
                 kmean_ref, kaug_ref, qaug_ref, m_ref, acc_ref, *, nh):
    hg = pl.program_id(1)
    p = pl.program_id(2)
    n_blk = kmean_ref.shape[0]
    tile = ATT_TILE
    gw = nh * HEAD_DIM
    n_feat = kaug_ref.shape[1] - gw
    q_tile = (p, n_blk - 1 - p)
    q_refs = (qa_ref, qb_ref)
    key_pos = lax.broadcasted_iota(jnp.int32, (tile, tile), 0)
    qry_pos = lax.broadcasted_iota(jnp.int32, (tile, tile), 1)

    @pl.when(p == 0)
    def _():
        k_all = k_ref[0]
        kmean_ref[...] = jnp.mean(k_all.astype(jnp.float32).reshape(n_blk, tile, gw), axis=1)
        kaug_ref[:, :gw] = k_all
        kaug_ref[:, gw:] = kfeat_ref[...]

    kmean = kmean_ref[...]
    kmean_hi = kmean.astype(jnp.bfloat16)
    kmean_lo = (kmean - kmean_hi.astype(jnp.float32)).astype(jnp.bfloat16)
    blk = lax.broadcasted_iota(jnp.int32, (n_blk, tile), 0).astype(jnp.float32)
    crow = lax.broadcasted_iota(jnp.int32, (n_feat - n_blk, tile), 0)
    for hh in range(nh):
        slope = slopes_ref[hg * nh + hh] * LOG2E
        base = jnp.where(crow < 3, slope * tile, jnp.where(crow < 6, slope, 0.0))
        part0 = _bf16_round(base)
        part1 = _bf16_round(base - part0)
        part2 = base - part0 - part1
        slope_rows = jnp.where((crow == 0) | (crow == 3), part0,
                               jnp.where((crow == 1) | (crow == 4), part1, part2))
        for sel in range(2):
            qm = _head_masked(q_refs[sel][0], hh)
            n_past = q_tile[sel].astype(jnp.float32)
            gate = _dot(kmean_hi, qm) + _dot(kmean_lo, qm)
            gate = jnp.where(blk < n_past, gate, -jnp.inf)
            chosen = blk == n_past
            for _ in range(MOBA_TOPK):
                top = jnp.max(gate, axis=0, keepdims=True)
                at_top = (gate == top) & (top > -jnp.inf)
                first = jnp.min(jnp.where(at_top, blk, float(n_blk)), axis=0, keepdims=True)
                pick = blk == first
                chosen = jnp.logical_or(chosen, pick)
                gate = jnp.where(pick, -jnp.inf, gate)
            choice_rows = jnp.where(chosen, 0.0, MASKED)
            qaug_ref[sel, hh, :gw] = qm
            qaug_ref[sel, hh, gw:] = jnp.concatenate([choice_rows, slope_rows], axis=0).astype(jnp.bfloat16)

    def key_block(j):
        return kaug_ref[pl.ds(pl.multiple_of(j * tile, tile), tile), :]

    def slot_of(s):
        is_b = s >= p
        return is_b.astype(jnp.int32), jnp.where(is_b, s - p, s)

    causal = key_pos <= qry_pos
    ones_rows = jnp.ones((BF16_ROWS, tile), jnp.bfloat16)

    def weighted_values(j, hh, prob):
        v_aug = jnp.concatenate([vt_ref[0, j, hh * HEAD_DIM:(hh + 1) * HEAD_DIM, :], ones_rows], axis=0)
        return _dot(v_aug, prob.astype(jnp.bfloat16))

    items = [(True, sel) for sel in range(2)] + [(False, s) for s in range(n_blk - 1)]
    in_flight = {}
    for step in range(len(items) + SOFTMAX_LAG):
        if step < len(items):
            own, idx = items[step]
            sel, j = (idx, q_tile[idx]) if own else slot_of(idx)
            k_blk = key_block(j)
            logits = [_dot(k_blk, qaug_ref[sel, hh]) for hh in range(nh)]
            if own:
                logits = [jnp.where(causal, lg, MASKED) for lg in logits]
            in_flight[step] = (own, sel, j, logits)
        if step >= SOFTMAX_LAG:
            own, sel, j, logits = in_flight.pop(step - SOFTMAX_LAG)
            for hh in range(nh):
                top = jnp.max(logits[hh], axis=0, keepdims=True)
                if own:
                    m_ref[sel, hh] = top
                    acc_ref[sel, hh] = weighted_values(j, hh, jnp.exp2(logits[hh] - top))
                else:
                    m_old = m_ref[sel, hh]
                    m_new = jnp.maximum(m_old, top)
                    m_ref[sel, hh] = m_new
                    acc_ref[sel, hh] = (jnp.exp2(m_old - m_new) * acc_ref[sel, hh]
                                        + weighted_values(j, hh, jnp.exp2(logits[hh] - m_new)))

    for sel, o_ref in enumerate((olo_ref, ohi_ref)):
        _store_heads(o_ref, [acc_ref[sel, hh, :HEAD_DIM] / acc_ref[sel, hh, HEAD_DIM:HEAD_DIM + 1]
                             for hh in range(nh)])


def _moba_attention(qt, k, vt, slopes):
    b, d, s = qt.shape
    nh = MOBA_HEADS
    gw = nh * HEAD_DIM
    n_blk = s // ATT_TILE
    n_feat = 2 * LANES - gw
    assert n_blk % 2 == 0 and PROJ_ROWS == 2 * ATT_TILE and n_feat >= n_blk + 6
    half = jax.ShapeDtypeStruct((b, s // 2, d), jnp.bfloat16)
    return pl.pallas_call(
        functools.partial(_moba_kernel, nh=nh),
        grid=(b, d // gw, n_blk // 2),
        in_specs=[
            pl.BlockSpec(memory_space=pltpu.SMEM),
            pl.BlockSpec((s, n_feat), lambda i, j, t: (0, 0)),
            pl.BlockSpec((1, gw, ATT_TILE), lambda i, j, t: (i, j, t)),
            pl.BlockSpec((1, gw, ATT_TILE), lambda i, j, t: (i, j, n_blk - 1 - t)),
            pl.BlockSpec((1, s, gw), lambda i, j, t: (i, 0, j)),
            pl.BlockSpec((1, n_blk, gw, ATT_TILE), lambda i, j, t: (i, 0, j, 0)),
        ],
        out_specs=[
            pl.BlockSpec((1, ATT_TILE, gw), lambda i, j, t: (i, t, j)),
            pl.BlockSpec((1, ATT_TILE, gw), lambda i, j, t: (i, lax.bitwise_xor(t, 1), j)),
        ],
        out_shape=[half, half],
        scratch_shapes=[
            pltpu.VMEM((n_blk, gw), jnp.float32),
            pltpu.VMEM((s, gw + n_feat), jnp.bfloat16),
            pltpu.VMEM((2, nh, gw + n_feat, ATT_TILE), jnp.bfloat16),
            pltpu.VMEM((2, nh, 1, ATT_TILE), jnp.float32),
            pltpu.VMEM((2, nh, HEAD_DIM + BF16_ROWS, ATT_TILE), jnp.float32),
        ],
        compiler_params=pltpu.CompilerParams(
            dimension_semantics=("arbitrary", "arbitrary", "arbitrary"),
            vmem_limit_bytes=VMEM_LIMIT),
        name="moba_attention",
    )(slopes, _moba_key_features(n_blk, n_feat), qt, qt, k, vt)


def _softplus2(z2):
    return jnp.maximum(z2, 0.0) + jnp.log(1.0 + jnp.exp2(-jnp.abs(z2))) * LOG2E


def _sb_kernel(later2_ref, qt_ref, k_ref, vt_ref, o_ref, suffix_ref, acc_ref, *, nh):
    qi = pl.program_id(2)
    tile = ATT_TILE
    qt = qt_ref[0]
    qm = [_head_masked(qt, hh) for hh in range(nh)]
    key_pos = lax.broadcasted_iota(jnp.int32, (tile, tile), 0)
    qry_pos = lax.broadcasted_iota(jnp.int32, (tile, tile), 1)
    strict = key_pos < qry_pos

    def key_block(j):
        return k_ref[0, pl.ds(pl.multiple_of(j * tile, tile), tile), :]

    def block_terms(j, hh):
        z2 = _dot(key_block(j), qm[hh])
        sp2 = _softplus2(z2)
        sp_hi = sp2.astype(jnp.bfloat16)
        sp_lo = (sp2 - sp_hi.astype(jnp.float32)).astype(jnp.bfloat16)
        inner = _dot(later2_ref[...], jnp.concatenate([sp_hi, sp_lo], axis=0))
        return z2 - sp2 - inner, inner[0:1] + sp2[0:1]

    def values(j, hh):
        return vt_ref[0, j, hh * HEAD_DIM:(hh + 1) * HEAD_DIM, :]

    j_prev = jnp.maximum(qi - 1, 0)
    units = [(hh, qi, True) for hh in range(nh)] + [(hh, j_prev, False) for hh in range(nh)]
    z2s, sp2s, inners, weights, totals, outs = {}, {}, {}, {}, {}, {}
    for step in range(len(units) + 4):
        if step < len(units):
            hh, j, _ = units[step]
            z2s[step] = _dot(key_block(j), qm[hh])
        u = step - 1
        if 0 <= u < len(units):
            sp2 = _softplus2(z2s[u])
            sp2s[u] = jnp.where(strict, sp2, 0.0) if units[u][2] else sp2
        u = step - 2
        if 0 <= u < len(units):
            sp_hi = sp2s[u].astype(jnp.bfloat16)
            sp_lo = (sp2s[u] - sp_hi.astype(jnp.float32)).astype(jnp.bfloat16)
            inners[u] = _dot(later2_ref[...], jnp.concatenate([sp_hi, sp_lo], axis=0))
        u = step - 3
        if 0 <= u < len(units):
            hh, _, own = units[u]
            totals[u] = inners[u][0:1] + sp2s[u][0:1]
            expo = z2s[u] - sp2s[u] - inners[u]
            if own:
                weights[u] = jnp.where(strict, jnp.exp2(expo), 0.0)
            else:
                weights[u] = jnp.exp2(expo - jnp.where(qi > 0, totals[hh], -MASKED))
        u = step - 4
        if 0 <= u < len(units):
            hh, j, _ = units[u]
            outs[u] = _dot(values(j, hh), weights[u].astype(jnp.bfloat16))
    for hh in range(nh):
        acc_ref[hh] = outs[hh] + outs[nh + hh]
        suffix_ref[hh] = totals[hh] + totals[nh + hh]

    def smallest_suffix():
        return functools.reduce(jnp.minimum, [jnp.min(suffix_ref[hh]) for hh in range(nh)])

    def more_to_do(carry):
        j, smallest = carry
        return jnp.logical_and(j >= 0, smallest <= SB_SUFFIX_CUTOFF)

    def farther_block(carry):
        j, _ = carry
        for hh in range(nh):
            expo, total = block_terms(j, hh)
            w = jnp.exp2(expo - suffix_ref[hh])
            acc_ref[hh] += _dot(values(j, hh), w.astype(jnp.bfloat16))
            suffix_ref[hh] += total
        return j - 1, smallest_suffix()

    lax.while_loop(more_to_do, farther_block, (qi - 2, smallest_suffix()))
    _store_heads(o_ref, [acc_ref[hh] for hh in range(nh)])


def _sb_attention(qt, k, vt):
    b, d, s = qt.shape
    nh = SB_HEADS
    gw = nh * HEAD_DIM
    n_blk = s // ATT_TILE
    later = np.triu(np.ones((ATT_TILE, ATT_TILE), np.float32), 1)
    later2 = jnp.asarray(np.concatenate([later, later], axis=1), jnp.bfloat16)
    return pl.pallas_call(
        functools.partial(_sb_kernel, nh=nh),
        grid=(b, d // gw, n_blk),
        in_specs=[
            pl.BlockSpec((ATT_TILE, 2 * ATT_TILE), lambda i, j, t: (0, 0)),
            pl.BlockSpec((1, gw, ATT_TILE), lambda i, j, t: (i, j, t)),
            pl.BlockSpec((1, s, gw), lambda i, j, t: (i, 0, j)),
            pl.BlockSpec((1, n_blk, gw, ATT_TILE), lambda i, j, t: (i, 0, j, 0)),
        ],
        out_specs=pl.BlockSpec((1, ATT_TILE, gw), lambda i, j, t: (i, t, j)),
        out_shape=jax.ShapeDtypeStruct((b, s, d), jnp.bfloat16),
        scratch_shapes=[
            pltpu.VMEM((nh, 1, ATT_TILE), jnp.float32),
            pltpu.VMEM((nh, HEAD_DIM, ATT_TILE), jnp.float32),
        ],
        compiler_params=pltpu.CompilerParams(
            dimension_semantics=("arbitrary", "arbitrary", "arbitrary"),
            vmem_limit_bytes=VMEM_LIMIT),
        name="stick_breaking_attention",
    )(later2, qt, k, vt)


def kernel(x, norm_g, w_in, w_out, final_g):
    b, s, d = x.shape
    depth = norm_g.shape[0]
    assert depth >= 1 and d == N_HEADS * HEAD_DIM and s % PROJ_ROWS == 0 and PROJ_ROWS % ATT_TILE == 0
    slopes = jnp.asarray(2.0 ** (-8.0 * np.arange(1, N_HEADS + 1) / N_HEADS), jnp.float32)
    h = x
    for i in range(depth):
        wq, wk, wv, wz = (w_in[i, :, c * d:(c + 1) * d] for c in range(4))
        w_tok = jnp.concatenate([wk, wz], axis=1).astype(jnp.bfloat16)
        w_feat_t = jnp.concatenate([wq, wv], axis=1).T.astype(jnp.bfloat16)
        k, z, qt, vt = _project(h, norm_g[i].reshape(1, d), w_tok, w_feat_t, b, s, d)
        k = k.reshape(b, s, d)
        if i % 2 == 0:
            o_parts = _moba_attention(qt, k, vt, slopes)
        else:
            o_parts = [_sb_attention(qt, k, vt)]
        last = i == depth - 1
        h = _out_project(o_parts, z.reshape(b, s, d), h, w_out[i].astype(jnp.bfloat16),
                         final_g.reshape(1, d) if last else None)
    return h
```

```python
import functools
import math

import numpy as np
import jax
import jax.numpy as jnp
from jax import lax
from jax.experimental import pallas as pl
from jax.experimental.pallas import tpu as pltpu

N_HEADS = 16
HEAD_DIM = 64
MOBA_BLOCK = 256
MOBA_TOPK = 3
NORM_EPS = 1e-6

LANES = 128
BF16_ROWS = 16
ATT_TILE = MOBA_BLOCK
PROJ_ROWS = 512
MOBA_HEADS = 2
SB_HEADS = 4
SB_TILES = 4
SOFTMAX_LAG = 2
LOG2E = math.log2(math.e)
MASKED = -1e30
EXP2_ARG_MAX = 126.0
SB_SUFFIX_CUTOFF = 128.0 * LOG2E
VMEM_LIMIT = 56 * 1024 * 1024


def _dot(a, b):
    return jnp.dot(a, b, preferred_element_type=jnp.float32)


def _dot_nt(a, b):
    return lax.dot_general(a, b, (((1,), (1,)), ((), ())), preferred_element_type=jnp.float32)


def _bf16_round(x):
    return x.astype(jnp.bfloat16).astype(jnp.float32)


def _rms_norm(h, g_ref):
    return h * lax.rsqrt(jnp.mean(h * h, axis=-1, keepdims=True) + NORM_EPS) * g_ref[...]


def _boundary_kernel(*refs, n_o, half_steps, has_proj, d, scale):
    refs = list(refs)
    if n_o:
        o_refs = [refs.pop(0) for _ in range(n_o)]
        z_ref, h_ref, wout_ref = refs.pop(0), refs.pop(0), refs.pop(0)
    else:
        h_ref = refs.pop(0)
    g_ref = refs.pop(0)
    if has_proj:
        wtok_ref, wfeat_ref = refs.pop(0), refs.pop(0)
    if n_o:
        hout_ref = refs.pop(0)

    h = h_ref[0]
    if n_o:
        if n_o == 1:
            o = o_refs[0][0]
        else:
            o = jnp.where(pl.program_id(1) < half_steps, o_refs[0][0], o_refs[1][0])
        z = z_ref[0].astype(jnp.float32)
        gated = o.astype(jnp.float32) * (z / (1.0 + jnp.exp(-z)))
        h = h + _dot(gated.astype(jnp.bfloat16), wout_ref[...])
    y = _rms_norm(h, g_ref)
    if n_o:
        hout_ref[0] = h if has_proj else y
    if has_proj:
        k_ref, znext_ref, qt_ref, vt_ref = refs
        xn = y.astype(jnp.bfloat16)
        tok = _dot(xn, wtok_ref[...])
        k_ref[0] = tok[:, :d].astype(k_ref.dtype)
        znext_ref[0] = tok[:, d:].astype(znext_ref.dtype)
        feat = _dot_nt(wfeat_ref[...], xn)
        qt_ref[0] = (feat[:d] * scale).astype(qt_ref.dtype)
        for c in range(vt_ref.shape[1]):
            vt_ref[0, c] = feat[d:, c * ATT_TILE:(c + 1) * ATT_TILE].astype(vt_ref.dtype)


def _layer_boundary(h, gain, attn=None, proj=None):
    b, s, d = h.shape
    rows = PROJ_ROWS
    n_steps = s // rows
    half = n_steps // 2
    n_blk = s // ATT_TILE
    row_spec = pl.BlockSpec((1, rows, d), lambda i, t: (i, t, 0))

    def const_spec(shape):
        return pl.BlockSpec(shape, lambda i, t: (0,) * len(shape), pipeline_mode=pl.Buffered(1))

    in_specs, args, out_specs, out_shape = [], [], [], []
    n_o = 0
    if attn is not None:
        o_parts, z, w_out = attn
        n_o = len(o_parts)
        if n_o == 1:
            in_specs.append(row_spec)
        else:
            in_specs += [
                pl.BlockSpec((1, rows, d), lambda i, t: (i, jnp.minimum(t, half - 1), 0)),
                pl.BlockSpec((1, rows, d), lambda i, t: (i, jnp.clip(n_steps - 1 - t, 0, half - 1), 0)),
            ]
        in_specs += [row_spec, row_spec, const_spec((d, d))]
        args += list(o_parts) + [z, h, w_out]
        out_specs.append(row_spec)
        out_shape.append(jax.ShapeDtypeStruct((b, s, d), jnp.float32))
    else:
        in_specs.append(row_spec)
        args.append(h)
    in_specs.append(const_spec((1, d)))
    args.append(gain.reshape(1, d))
    if proj is not None:
        in_specs += [const_spec((d, 2 * d)), const_spec((2 * d, d))]
        args += list(proj)
        out_specs += [
            row_spec,
            row_spec,
            pl.BlockSpec((1, d, rows), lambda i, t: (i, 0, t)),
            pl.BlockSpec((1, rows // ATT_TILE, d, ATT_TILE), lambda i, t: (i, t, 0, 0)),
        ]
        out_shape += [
            jax.ShapeDtypeStruct((b, s, d), jnp.bfloat16),
            jax.ShapeDtypeStruct((b, s, d), jnp.bfloat16),
            jax.ShapeDtypeStruct((b, d, s), jnp.bfloat16),
            jax.ShapeDtypeStruct((b, n_blk, d, ATT_TILE), jnp.bfloat16),
        ]
    kern = functools.partial(_boundary_kernel, n_o=n_o, half_steps=half, has_proj=proj is not None,
                             d=d, scale=HEAD_DIM ** -0.5 * LOG2E)
    name = ("gate_outproj_" if attn is not None else "") + ("rmsnorm_qkvz_proj" if proj is not None else "final_rmsnorm")
    return pl.pallas_call(
        kern,
        grid=(b, n_steps),
        in_specs=in_specs,
        out_specs=out_specs,
        out_shape=out_shape,
        compiler_params=pltpu.CompilerParams(
            dimension_semantics=("arbitrary", "arbitrary"), vmem_limit_bytes=VMEM_LIMIT),
        name=name,
    )(*args)


def _head_masked(qt, hh):
    row = lax.broadcasted_iota(jnp.int32, qt.shape, 0)
    return jnp.where((row >= hh * HEAD_DIM) & (row < (hh + 1) * HEAD_DIM), qt, jnp.zeros_like(qt))


def _store_heads(o_ref, outs_t):
    stacked = jnp.concatenate(outs_t, axis=0)
    o_ref[0] = stacked.T.astype(o_ref.dtype)


def _moba_key_features(n_blk, n_feat):
    pos = np.arange(n_blk * ATT_TILE)
    feat = np.zeros((n_blk * ATT_TILE, n_feat), np.float32)
    feat[pos, pos // ATT_TILE] = 1.0
    feat[:, n_blk:n_blk + 3] = (pos // ATT_TILE)[:, None]
    feat[:, n_blk + 3:n_blk + 6] = (pos % ATT_TILE)[:, None]
    return jnp.asarray(feat, jnp.bfloat16)


def _moba_kernel(slopes_ref, kfeat_ref, qa_ref, qb_ref, k_ref, vt_ref, olo_ref, ohi_ref,
                 kmean_ref, kaug_ref, qaug_ref, m_ref, acc_ref, *, nh):
    hg = pl.program_id(1)
    p = pl.program_id(2)
    n_blk = kmean_ref.shape[0]
    tile = ATT_TILE
    gw = nh * HEAD_DIM
    n_feat = kaug_ref.shape[1] - gw
    q_tile = (p, n_blk - 1 - p)
    q_refs = (qa_ref, qb_ref)
    pairs = [(sel, hh) for sel in range(2) for hh in range(nh)]
    key_pos = lax.broadcasted_iota(jnp.int32, (tile, tile), 0)
    qry_pos = lax.broadcasted_iota(jnp.int32, (tile, tile), 1)

    @pl.when(p == 0)
    def _():
        k_all = k_ref[0]
        kmean_ref[...] = jnp.mean(k_all.astype(jnp.float32).reshape(n_blk, tile, gw), axis=1)
        kaug_ref[:, :gw] = k_all
        kaug_ref[:, gw:] = kfeat_ref[...]

    crow = lax.broadcasted_iota(jnp.int32, (n_feat - n_blk, tile), 0)
    no_choice = jnp.zeros((n_blk, tile), jnp.float32)
    q_masked = {}
    for hh in range(nh):
        slope = slopes_ref[hg * nh + hh] * LOG2E
        base = jnp.where(crow < 3, slope * tile, jnp.where(crow < 6, slope, 0.0))
        part0 = _bf16_round(base)
        part1 = _bf16_round(base - part0)
        part2 = base - part0 - part1
        slope_rows = jnp.where((crow == 0) | (crow == 3), part0,
                               jnp.where((crow == 1) | (crow == 4), part1, part2))
        bias_rows = jnp.concatenate([no_choice, slope_rows], axis=0).astype(jnp.bfloat16)
        for sel in range(2):
            q_masked[sel, hh] = _head_masked(q_refs[sel][0], hh)
            qaug_ref[sel, hh, :gw] = q_masked[sel, hh]
            qaug_ref[sel, hh, gw:] = bias_rows

    kmean = kmean_ref[...]
    kmean_hi = kmean.astype(jnp.bfloat16)
    kmean_lo = (kmean - kmean_hi.astype(jnp.float32)).astype(jnp.bfloat16)
    kmean_split = jnp.concatenate([kmean_hi, kmean_lo], axis=0)
    gate_parts = [_dot(kmean_split, q_masked[pair]) for pair in pairs]
    gates = jnp.concatenate([g[:n_blk] + g[n_blk:] for g in gate_parts], axis=1)

    def store_block_choice():
        blk = lax.broadcasted_iota(jnp.int32, gates.shape, 0).astype(jnp.float32)
        lane = lax.broadcasted_iota(jnp.int32, gates.shape, 1)
        n_past = jnp.where(lane < nh * tile, q_tile[0], q_tile[1]).astype(jnp.float32)
        gate = jnp.where(blk < n_past, gates, -jnp.inf)
        chosen = blk == n_past
        for _ in range(MOBA_TOPK):
            top = jnp.max(gate, axis=0, keepdims=True)
            at_top = (gate == top) & (top > -jnp.inf)
            first = jnp.min(jnp.where(at_top, blk, float(n_blk)), axis=0, keepdims=True)
            pick = blk == first
            chosen = jnp.logical_or(chosen, pick)
            gate = jnp.where(pick, -jnp.inf, gate)
        choice = jnp.where(chosen, 0.0, MASKED).astype(jnp.bfloat16)
        for c, (sel, hh) in enumerate(pairs):
            qaug_ref[sel, hh, gw:gw + n_blk] = choice[:, c * tile:(c + 1) * tile]

    def key_block(j):
        return kaug_ref[pl.ds(pl.multiple_of(j * tile, tile), tile), :]

    def slot_of(s):
        is_b = s >= p
        return is_b.astype(jnp.int32), jnp.where(is_b, s - p, s)

    causal = key_pos <= qry_pos
    ones_rows = jnp.ones((BF16_ROWS, tile), jnp.bfloat16)

    def weighted_values(j, hh, prob):
        v_aug = jnp.concatenate([vt_ref[0, j, hh * HEAD_DIM:(hh + 1) * HEAD_DIM, :], ones_rows], axis=0)
        return _dot(v_aug, prob.astype(jnp.bfloat16))

    items = [(True, sel) for sel in range(2)] + [(False, s) for s in range(n_blk - 1)]
    in_flight = {}
    for step in range(len(items) + SOFTMAX_LAG):
        if step < len(items):
            own, idx = items[step]
            if not own and items[step - 1][0]:
                store_block_choice()
            sel, j = (idx, q_tile[idx]) if own else slot_of(idx)
            k_blk = key_block(j)
            logits = [_dot(k_blk, qaug_ref[sel, hh]) for hh in range(nh)]
            if own:
                logits = [jnp.where(causal, lg, MASKED) for lg in logits]
            in_flight[step] = (own, sel, j, logits)
        if step >= SOFTMAX_LAG:
            own, sel, j, logits = in_flight.pop(step - SOFTMAX_LAG)
            for hh in range(nh):
                top = jnp.max(logits[hh], axis=0, keepdims=True)
                if own:
                    m_ref[sel, hh] = top
                    acc_ref[sel, hh] = weighted_values(j, hh, jnp.exp2(logits[hh] - top))
                else:
                    m_old = m_ref[sel, hh]
                    m_new = jnp.maximum(m_old, top)
                    m_ref[sel, hh] = m_new
                    acc_ref[sel, hh] = (jnp.exp2(m_old - m_new) * acc_ref[sel, hh]
                                        + weighted_values(j, hh, jnp.exp2(logits[hh] - m_new)))

    for sel, o_ref in enumerate((olo_ref, ohi_ref)):
        _store_heads(o_ref, [acc_ref[sel, hh, :HEAD_DIM] * (1.0 / acc_ref[sel, hh, HEAD_DIM:HEAD_DIM + 1])
                             for hh in range(nh)])


def _moba_attention(qt, k, vt, slopes):
    b, d, s = qt.shape
    nh = MOBA_HEADS
    gw = nh * HEAD_DIM
    n_blk = s // ATT_TILE
    n_feat = 2 * LANES - gw
    assert n_blk % 2 == 0 and PROJ_ROWS == 2 * ATT_TILE and n_feat >= n_blk + 6
    half = jax.ShapeDtypeStruct((b, s // 2, d), jnp.bfloat16)
    return pl.pallas_call(
        functools.partial(_moba_kernel, nh=nh),
        grid=(b, d // gw, n_blk // 2),
        in_specs=[
            pl.BlockSpec(memory_space=pltpu.SMEM),
            pl.BlockSpec((s, n_feat), lambda i, j, t: (0, 0)),
            pl.BlockSpec((1, gw, ATT_TILE), lambda i, j, t: (i, j, t)),
            pl.BlockSpec((1, gw, ATT_TILE), lambda i, j, t: (i, j, n_blk - 1 - t)),
            pl.BlockSpec((1, s, gw), lambda i, j, t: (i, 0, j)),
            pl.BlockSpec((1, n_blk, gw, ATT_TILE), lambda i, j, t: (i, 0, j, 0)),
        ],
        out_specs=[
            pl.BlockSpec((1, ATT_TILE, gw), lambda i, j, t: (i, t, j)),
            pl.BlockSpec((1, ATT_TILE, gw), lambda i, j, t: (i, lax.bitwise_xor(t, 1), j)),
        ],
        out_shape=[half, half],
        scratch_shapes=[
            pltpu.VMEM((n_blk, gw), jnp.float32),
            pltpu.VMEM((s, gw + n_feat), jnp.bfloat16),
            pltpu.VMEM((2, nh, gw + n_feat, ATT_TILE), jnp.bfloat16),
            pltpu.VMEM((2, nh, 1, ATT_TILE), jnp.float32),
            pltpu.VMEM((2, nh, HEAD_DIM + BF16_ROWS, ATT_TILE), jnp.float32),
        ],
        compiler_params=pltpu.CompilerParams(
            dimension_semantics=("arbitrary", "arbitrary", "arbitrary"),
            vmem_limit_bytes=VMEM_LIMIT),
        name="moba_attention",
    )(slopes, _moba_key_features(n_blk, n_feat), qt, qt, k, vt)


def _softplus2(z2):
    return jnp.maximum(z2, jnp.log(1.0 + jnp.exp2(jnp.minimum(z2, EXP2_ARG_MAX))) * LOG2E)


def _sb_kernel(later2_ref, qt_ref, k_ref, vt_ref, o_ref, suffix_ref, acc_ref, *, nh, nt):
    first_tile = pl.program_id(2) * nt
    tile = ATT_TILE
    qm = {(ti, hh): _head_masked(qt_ref[0, :, ti * tile:(ti + 1) * tile], hh)
          for ti in range(nt) for hh in range(nh)}
    key_pos = lax.broadcasted_iota(jnp.int32, (tile, tile), 0)
    qry_pos = lax.broadcasted_iota(jnp.int32, (tile, tile), 1)
    strict = key_pos < qry_pos

    def key_block(j):
        return k_ref[0, pl.ds(pl.multiple_of(j * tile, tile), tile), :]

    def later_sums(sp2):
        sp_hi = sp2.astype(jnp.bfloat16)
        sp_lo = (sp2 - sp_hi.astype(jnp.float32)).astype(jnp.bfloat16)
        return _dot(later2_ref[...], jnp.concatenate([sp_hi, sp_lo], axis=0))

    def values(j, hh):
        return vt_ref[0, j, hh * HEAD_DIM:(hh + 1) * HEAD_DIM, :]

    streams = [(ti, hh) for ti in range(nt) for hh in range(nh)]
    units = ([(ti, hh, first_tile + ti, True) for ti, hh in streams]
             + [(ti, hh, jnp.maximum(first_tile + ti - 1, 0), False) for ti, hh in streams])
    z2s, sp2s, inners, weights, totals, outs = {}, {}, {}, {}, {}, {}
    for step in range(len(units) + 4):
        if step < len(units):
            ti, hh, j, _ = units[step]
            z2s[step] = _dot(key_block(j), qm[ti, hh])
        u = step - 1
        if 0 <= u < len(units):
            sp2 = _softplus2(z2s[u])
            sp2s[u] = jnp.where(strict, sp2, 0.0) if units[u][3] else sp2
        u = step - 2
        if 0 <= u < len(units):
            inners[u] = later_sums(sp2s[u])
        u = step - 3
        if 0 <= u < len(units):
            ti, _, _, own = units[u]
            totals[u] = inners[u][0:1] + sp2s[u][0:1]
            expo = z2s[u] - sp2s[u] - inners[u]
            if own:
                weights[u] = jnp.where(strict, jnp.exp2(expo), 0.0)
            else:
                own_total = totals[u - len(streams)]
                weights[u] = jnp.exp2(expo - jnp.where(first_tile + ti > 0, own_total, -MASKED))
        u = step - 4
        if 0 <= u < len(units):
            _, hh, j, _ = units[u]
            outs[u] = _dot(values(j, hh), weights[u].astype(jnp.bfloat16))
    for c, (ti, hh) in enumerate(streams):
        acc_ref[ti, hh] = outs[c] + outs[len(streams) + c]
        suffix_ref[ti, hh] = totals[c] + totals[len(streams) + c]

    for ti in range(nt):
        def smallest_suffix(ti=ti):
            return functools.reduce(jnp.minimum, [jnp.min(suffix_ref[ti, hh]) for hh in range(nh)])

        def more_to_do(carry):
            j, smallest = carry
            return jnp.logical_and(j >= 0, smallest <= SB_SUFFIX_CUTOFF)

        def farther_block(carry, ti=ti, smallest_suffix=smallest_suffix):
            j, _ = carry
            for hh in range(nh):
                z2 = _dot(key_block(j), qm[ti, hh])
                sp2 = _softplus2(z2)
                inner = later_sums(sp2)
                w = jnp.exp2(z2 - sp2 - inner - suffix_ref[ti, hh])
                acc_ref[ti, hh] += _dot(values(j, hh), w.astype(jnp.bfloat16))
                suffix_ref[ti, hh] += inner[0:1] + sp2[0:1]
            return j - 1, smallest_suffix()

        lax.while_loop(more_to_do, farther_block, (first_tile + ti - 2, smallest_suffix()))
        _store_heads(o_ref.at[:, ti * tile:(ti + 1) * tile], [acc_ref[ti, hh] for hh in range(nh)])


def _sb_attention(qt, k, vt):
    b, d, s = qt.shape
    nh = SB_HEADS
    nt = SB_TILES
    gw = nh * HEAD_DIM
    n_blk = s // ATT_TILE
    assert n_blk % nt == 0
    later = np.triu(np.ones((ATT_TILE, ATT_TILE), np.float32), 1)
    later2 = jnp.asarray(np.concatenate([later, later], axis=1), jnp.bfloat16)
    return pl.pallas_call(
        functools.partial(_sb_kernel, nh=nh, nt=nt),
        grid=(b, d // gw, n_blk // nt),
        in_specs=[
            pl.BlockSpec((ATT_TILE, 2 * ATT_TILE), lambda i, j, t: (0, 0)),
            pl.BlockSpec((1, gw, nt * ATT_TILE), lambda i, j, t: (i, j, t)),
            pl.BlockSpec((1, s, gw), lambda i, j, t: (i, 0, j)),
            pl.BlockSpec((1, n_blk, gw, ATT_TILE), lambda i, j, t: (i, 0, j, 0)),
        ],
        out_specs=pl.BlockSpec((1, nt * ATT_TILE, gw), lambda i, j, t: (i, t, j)),
        out_shape=jax.ShapeDtypeStruct((b, s, d), jnp.bfloat16),
        scratch_shapes=[
            pltpu.VMEM((nt, nh, 1, ATT_TILE), jnp.float32),
            pltpu.VMEM((nt, nh, HEAD_DIM, ATT_TILE), jnp.float32),
        ],
        compiler_params=pltpu.CompilerParams(
            dimension_semantics=("arbitrary", "arbitrary", "arbitrary"),
            vmem_limit_bytes=VMEM_LIMIT),
        name="stick_breaking_attention",
    )(later2, qt, k, vt)


def kernel(x, norm_g, w_in, w_out, final_g):
    b, s, d = x.shape
    depth = norm_g.shape[0]
    assert depth >= 1 and d == N_HEADS * HEAD_DIM and s % PROJ_ROWS == 0 and PROJ_ROWS % ATT_TILE == 0
    slopes = jnp.asarray(2.0 ** (-8.0 * np.arange(1, N_HEADS + 1) / N_HEADS), jnp.float32)

    def proj_weights(i):
        wq, wk, wv, wz = (w_in[i, :, c * d:(c + 1) * d] for c in range(4))
        w_tok = jnp.concatenate([wk, wz], axis=1).astype(jnp.bfloat16)
        w_feat_t = jnp.concatenate([wq, wv], axis=1).T.astype(jnp.bfloat16)
        return w_tok, w_feat_t

    h = x
    k, z, qt, vt = _layer_boundary(h, norm_g[0], proj=proj_weights(0))
    for i in range(depth):
        o_parts = _moba_attention(qt, k, vt, slopes) if i % 2 == 0 else [_sb_attention(qt, k, vt)]
        attn = (o_parts, z, w_out[i].astype(jnp.bfloat16))
        if i + 1 < depth:
            h, k, z, qt, vt = _layer_boundary(h, norm_g[i + 1], attn=attn, proj=proj_weights(i + 1))
        else:
            (h,) = _layer_boundary(h, final_g, attn=attn)
    return h
```

```python
import functools
import math

import numpy as np
import jax
import jax.numpy as jnp
from jax import lax
from jax.experimental import pallas as pl
from jax.experimental.pallas import tpu as pltpu

N_HEADS = 16
HEAD_DIM = 64
MOBA_BLOCK = 256
MOBA_TOPK = 3
NORM_EPS = 1e-6

LANES = 128
BF16_ROWS = 16
ATT_TILE = MOBA_BLOCK
PROJ_ROWS = 512
MOBA_HEADS = 2
SB_HEADS = 4
SB_TILES = 4
SOFTMAX_LAG = 2
LOG2E = math.log2(math.e)
MASKED = -1e30
EXP2_ARG_MAX = 126.0
SB_SUFFIX_CUTOFF = 128.0 * LOG2E
VMEM_LIMIT = 56 * 1024 * 1024


def _dot(a, b):
    return jnp.dot(a, b, preferred_element_type=jnp.float32)


def _dot_nt(a, b):
    return lax.dot_general(a, b, (((1,), (1,)), ((), ())), preferred_element_type=jnp.float32)


def _bf16_round(x):
    return x.astype(jnp.bfloat16).astype(jnp.float32)


def _rms_norm(h, g_ref):
    return h * lax.rsqrt(jnp.mean(h * h, axis=-1, keepdims=True) + NORM_EPS) * g_ref[...]


def _boundary_kernel(*refs, n_o, half_steps, has_proj, d, scale):
    refs = list(refs)
    if n_o:
        o_refs = [refs.pop(0) for _ in range(n_o)]
        z_ref, h_ref, wout_ref = refs.pop(0), refs.pop(0), refs.pop(0)
    else:
        h_ref = refs.pop(0)
    g_ref = refs.pop(0)
    if has_proj:
        wtok_ref, wfeat_ref = refs.pop(0), refs.pop(0)
    if n_o:
        hout_ref = refs.pop(0)

    h = h_ref[0]
    if n_o:
        if n_o == 1:
            o = o_refs[0][0]
        else:
            o = jnp.where(pl.program_id(1) < half_steps, o_refs[0][0], o_refs[1][0])
        z = z_ref[0].astype(jnp.float32)
        gated = o.astype(jnp.float32) * (z / (1.0 + jnp.exp(-z)))
        h = h + _dot(gated.astype(jnp.bfloat16), wout_ref[...])
    y = _rms_norm(h, g_ref)
    if n_o:
        hout_ref[0] = h if has_proj else y
    if has_proj:
        k_ref, znext_ref, qt_ref, vt_ref = refs
        xn = y.astype(jnp.bfloat16)
        tok = _dot(xn, wtok_ref[...])
        k_ref[0] = tok[:, :d].astype(k_ref.dtype)
        znext_ref[0] = tok[:, d:].astype(znext_ref.dtype)
        feat = _dot_nt(wfeat_ref[...], xn)
        qt_ref[0] = (feat[:d] * scale).astype(qt_ref.dtype)
        for c in range(vt_ref.shape[1]):
            vt_ref[0, c] = feat[d:, c * ATT_TILE:(c + 1) * ATT_TILE].astype(vt_ref.dtype)


def _layer_boundary(h, gain, attn=None, proj=None):
    b, s, d = h.shape
    rows = PROJ_ROWS
    n_steps = s // rows
    half = n_steps // 2
    n_blk = s // ATT_TILE
    row_spec = pl.BlockSpec((1, rows, d), lambda i, t: (i, t, 0))

    def const_spec(shape):
        return pl.BlockSpec(shape, lambda i, t: (0,) * len(shape), pipeline_mode=pl.Buffered(1))

    in_specs, args, out_specs, out_shape = [], [], [], []
    n_o = 0
    if attn is not None:
        o_parts, z, w_out = attn
        n_o = len(o_parts)
        if n_o == 1:
            in_specs.append(row_spec)
        else:
            in_specs += [
                pl.BlockSpec((1, rows, d), lambda i, t: (i, jnp.minimum(t, half - 1), 0)),
                pl.BlockSpec((1, rows, d), lambda i, t: (i, jnp.clip(n_steps - 1 - t, 0, half - 1), 0)),
            ]
        in_specs += [row_spec, row_spec, const_spec((d, d))]
        args += list(o_parts) + [z, h, w_out]
        out_specs.append(row_spec)
        out_shape.append(jax.ShapeDtypeStruct((b, s, d), jnp.float32))
    else:
        in_specs.append(row_spec)
        args.append(h)
    in_specs.append(const_spec((1, d)))
    args.append(gain.reshape(1, d))
    if proj is not None:
        in_specs += [const_spec((d, 2 * d)), const_spec((2 * d, d))]
        args += list(proj)
        out_specs += [
            row_spec,
            row_spec,
            pl.BlockSpec((1, d, rows), lambda i, t: (i, 0, t)),
            pl.BlockSpec((1, rows // ATT_TILE, d, ATT_TILE), lambda i, t: (i, t, 0, 0)),
        ]
        out_shape += [
            jax.ShapeDtypeStruct((b, s, d), jnp.bfloat16),
            jax.ShapeDtypeStruct((b, s, d), jnp.bfloat16),
            jax.ShapeDtypeStruct((b, d, s), jnp.bfloat16),
            jax.ShapeDtypeStruct((b, n_blk, d, ATT_TILE), jnp.bfloat16),
        ]
    kern = functools.partial(_boundary_kernel, n_o=n_o, half_steps=half, has_proj=proj is not None,
                             d=d, scale=HEAD_DIM ** -0.5 * LOG2E)
    name = ("gate_outproj_" if attn is not None else "") + ("rmsnorm_qkvz_proj" if proj is not None else "final_rmsnorm")
    return pl.pallas_call(
        kern,
        grid=(b, n_steps),
        in_specs=in_specs,
        out_specs=out_specs,
        out_shape=out_shape,
        compiler_params=pltpu.CompilerParams(
            dimension_semantics=("arbitrary", "arbitrary"), vmem_limit_bytes=VMEM_LIMIT),
        name=name,
    )(*args)


def _head_masked(qt, hh):
    row = lax.broadcasted_iota(jnp.int32, qt.shape, 0)
    return jnp.where((row >= hh * HEAD_DIM) & (row < (hh + 1) * HEAD_DIM), qt, jnp.zeros_like(qt))


def _store_heads(o_ref, outs_t):
    stacked = jnp.concatenate(outs_t, axis=0)
    o_ref[0] = stacked.T.astype(o_ref.dtype)


def _moba_key_features(n_blk):
    pos = np.arange(n_blk * ATT_TILE)
    feat = np.zeros((n_blk * ATT_TILE, HEAD_DIM), np.float32)
    feat[pos, pos // ATT_TILE] = 1.0
    feat[:, n_blk:n_blk + 3] = (pos // ATT_TILE)[:, None]
    feat[:, n_blk + 3:n_blk + 6] = (pos % ATT_TILE)[:, None]
    return jnp.asarray(np.concatenate([feat, feat], axis=1), jnp.bfloat16)


def _moba_kernel(slopes_ref, kfeat_ref, qa_ref, qb_ref, k_ref, vt_ref, olo_ref, ohi_ref,
                 kmean_ref, kaug_ref, qaug_ref, m_ref, acc_ref):
    nh = MOBA_HEADS
    hg = pl.program_id(1)
    p = pl.program_id(2)
    n_blk = kmean_ref.shape[0]
    tile = ATT_TILE
    q_tile = (p, n_blk - 1 - p)
    q_refs = (qa_ref, qb_ref)
    pairs = [(sel, hh) for sel in range(2) for hh in range(nh)]
    key_pos = lax.broadcasted_iota(jnp.int32, (tile, tile), 0)
    qry_pos = lax.broadcasted_iota(jnp.int32, (tile, tile), 1)

    def in_head(index, hh):
        return (index >= hh * HEAD_DIM) & (index < (hh + 1) * HEAD_DIM)

    @pl.when(p == 0)
    def _():
        k_all = k_ref[0]
        kmean_ref[...] = jnp.mean(k_all.astype(jnp.float32).reshape(n_blk, tile, LANES), axis=1)
        lane = lax.broadcasted_iota(jnp.int32, k_all.shape, 1)
        for hh in range(nh):
            kaug_ref[hh] = jnp.where(in_head(lane, hh), k_all, kfeat_ref[...])

    crow = lax.broadcasted_iota(jnp.int32, (HEAD_DIM - n_blk, tile), 0)
    no_choice = jnp.zeros((n_blk, tile), jnp.float32)
    for hh in range(nh):
        slope = slopes_ref[hg * nh + hh] * LOG2E
        base = jnp.where(crow < 3, slope * tile, jnp.where(crow < 6, slope, 0.0))
        part0 = _bf16_round(base)
        part1 = _bf16_round(base - part0)
        part2 = base - part0 - part1
        slope_rows = jnp.where((crow == 0) | (crow == 3), part0,
                               jnp.where((crow == 1) | (crow == 4), part1, part2))
        bias_rows = jnp.concatenate([no_choice, slope_rows], axis=0).astype(jnp.bfloat16)
        for sel in range(2):
            q_rows = q_refs[sel][0, hh * HEAD_DIM:(hh + 1) * HEAD_DIM, :]
            qaug_ref[sel, hh] = jnp.concatenate([q_rows, bias_rows] if hh == 0 else [bias_rows, q_rows], axis=0)

    kmean = kmean_ref[...]
    kmean_lane = lax.broadcasted_iota(jnp.int32, kmean.shape, 1)
    gate_lhs = []
    for hh in range(nh):
        kmean_h = jnp.where(in_head(kmean_lane, hh), kmean, 0.0)
        kmean_hi = kmean_h.astype(jnp.bfloat16)
        gate_lhs += [kmean_hi, (kmean_h - kmean_hi.astype(jnp.float32)).astype(jnp.bfloat16)]
    gate_lhs = jnp.concatenate(gate_lhs, axis=0)
    gate_parts = []
    for sel in range(2):
        g = _dot(gate_lhs, q_refs[sel][0])
        gate_parts += [g[2 * hh * n_blk:(2 * hh + 1) * n_blk] + g[(2 * hh + 1) * n_blk:(2 * hh + 2) * n_blk]
                       for hh in range(nh)]
    gates = jnp.concatenate(gate_parts, axis=1)

    def store_block_choice():
        blk = lax.broadcasted_iota(jnp.int32, gates.shape, 0).astype(jnp.float32)
        lane = lax.broadcasted_iota(jnp.int32, gates.shape, 1)
        n_past = jnp.where(lane < nh * tile, q_tile[0], q_tile[1]).astype(jnp.float32)
        gate = jnp.where(blk < n_past, gates, -jnp.inf)
        chosen = blk == n_past
        for _ in range(MOBA_TOPK):
            top = jnp.max(gate, axis=0, keepdims=True)
            at_top = (gate == top) & (top > -jnp.inf)
            first = jnp.min(jnp.where(at_top, blk, float(n_blk)), axis=0, keepdims=True)
            pick = blk == first
            chosen = jnp.logical_or(chosen, pick)
            gate = jnp.where(pick, -jnp.inf, gate)
        choice = jnp.where(chosen, 0.0, MASKED).astype(jnp.bfloat16)
        for c, (sel, hh) in enumerate(pairs):
            first_bias_row = (1 - hh) * HEAD_DIM
            qaug_ref[sel, hh, first_bias_row:first_bias_row + n_blk] = choice[:, c * tile:(c + 1) * tile]

    def key_block(j, hh):
        return kaug_ref[hh, pl.ds(pl.multiple_of(j * tile, tile), tile), :]

    def slot_of(s):
        is_b = s >= p
        return is_b.astype(jnp.int32), jnp.where(is_b, s - p, s)

    causal = key_pos <= qry_pos
    ones_rows = jnp.ones((BF16_ROWS, tile), jnp.bfloat16)

    def weighted_values(j, hh, prob):
        v_aug = jnp.concatenate([vt_ref[0, j, hh * HEAD_DIM:(hh + 1) * HEAD_DIM, :], ones_rows], axis=0)
        return _dot(v_aug, prob.astype(jnp.bfloat16))

    def finish(sel, o_ref):
        _store_heads(o_ref, [acc_ref[sel, hh, :HEAD_DIM] * (1.0 / acc_ref[sel, hh, HEAD_DIM:HEAD_DIM + 1])
                             for hh in range(nh)])

    items = [(True, sel) for sel in range(2)] + [(False, s) for s in range(n_blk - 1)]
    last_item_of_first_tile = items.index((False, n_blk // 2 - 2)) if n_blk >= 4 else 1
    in_flight = {}
    for step in range(len(items) + SOFTMAX_LAG):
        if step < len(items):
            own, idx = items[step]
            if not own and items[step - 1][0]:
                store_block_choice()
            sel, j = (idx, q_tile[idx]) if own else slot_of(idx)
            logits = [_dot(key_block(j, hh), qaug_ref[sel, hh]) for hh in range(nh)]
            if own:
                logits = [jnp.where(causal, lg, MASKED) for lg in logits]
            in_flight[step] = (own, sel, j, logits)
        done = step - SOFTMAX_LAG
        if done >= 0:
            own, sel, j, logits = in_flight.pop(done)
            for hh in range(nh):
                top = jnp.max(logits[hh], axis=0, keepdims=True)
                if own:
                    m_ref[sel, hh] = top
                    acc_ref[sel, hh] = weighted_values(j, hh, jnp.exp2(logits[hh] - top))
                else:
                    m_old = m_ref[sel, hh]
                    m_new = jnp.maximum(m_old, top)
                    m_ref[sel, hh] = m_new
                    acc_ref[sel, hh] = (jnp.exp2(m_old - m_new) * acc_ref[sel, hh]
                                        + weighted_values(j, hh, jnp.exp2(logits[hh] - m_new)))
            if done == last_item_of_first_tile:
                finish(0, olo_ref)
    finish(1, ohi_ref)


def _moba_attention(qt, k, vt, slopes):
    b, d, s = qt.shape
    nh = MOBA_HEADS
    gw = nh * HEAD_DIM
    n_blk = s // ATT_TILE
    assert gw == LANES and n_blk % 2 == 0 and PROJ_ROWS == 2 * ATT_TILE and HEAD_DIM >= n_blk + 6
    half = jax.ShapeDtypeStruct((b, s // 2, d), jnp.bfloat16)
    return pl.pallas_call(
        _moba_kernel,
        grid=(b, d // gw, n_blk // 2),
        in_specs=[
            pl.BlockSpec(memory_space=pltpu.SMEM),
            pl.BlockSpec((s, LANES), lambda i, j, t: (0, 0)),
            pl.BlockSpec((1, gw, ATT_TILE), lambda i, j, t: (i, j, t)),
            pl.BlockSpec((1, gw, ATT_TILE), lambda i, j, t: (i, j, n_blk - 1 - t)),
            pl.BlockSpec((1, s, gw), lambda i, j, t: (i, 0, j)),
            pl.BlockSpec((1, n_blk, gw, ATT_TILE), lambda i, j, t: (i, 0, j, 0)),
        ],
        out_specs=[
            pl.BlockSpec((1, ATT_TILE, gw), lambda i, j, t: (i, t, j)),
            pl.BlockSpec((1, ATT_TILE, gw), lambda i, j, t: (i, lax.bitwise_xor(t, 1), j)),
        ],
        out_shape=[half, half],
        scratch_shapes=[
            pltpu.VMEM((n_blk, gw), jnp.float32),
            pltpu.VMEM((nh, s, LANES), jnp.bfloat16),
            pltpu.VMEM((2, nh, LANES, ATT_TILE), jnp.bfloat16),
            pltpu.VMEM((2, nh, 1, ATT_TILE), jnp.float32),
            pltpu.VMEM((2, nh, HEAD_DIM + BF16_ROWS, ATT_TILE), jnp.float32),
        ],
        compiler_params=pltpu.CompilerParams(
            dimension_semantics=("arbitrary", "arbitrary", "arbitrary"),
            vmem_limit_bytes=VMEM_LIMIT),
        name="moba_attention",
    )(slopes, _moba_key_features(n_blk), qt, qt, k, vt)


def _softplus2(z2):
    return jnp.maximum(z2, jnp.log(1.0 + jnp.exp2(jnp.minimum(z2, EXP2_ARG_MAX))) * LOG2E)


def _sb_kernel(later2_ref, qt_ref, k_ref, vt_ref, o_ref, suffix_ref, acc_ref, *, nh, nt):
    first_tile = pl.program_id(2) * nt
    tile = ATT_TILE
    qm = {(ti, hh): _head_masked(qt_ref[0, :, ti * tile:(ti + 1) * tile], hh)
          for ti in range(nt) for hh in range(nh)}
    key_pos = lax.broadcasted_iota(jnp.int32, (tile, tile), 0)
    qry_pos = lax.broadcasted_iota(jnp.int32, (tile, tile), 1)
    strict = key_pos < qry_pos

    def key_block(j):
        return k_ref[0, pl.ds(pl.multiple_of(j * tile, tile), tile), :]

    def later_sums(sp2):
        sp_hi = sp2.astype(jnp.bfloat16)
        sp_lo = (sp2 - sp_hi.astype(jnp.float32)).astype(jnp.bfloat16)
        return _dot(later2_ref[...], jnp.concatenate([sp_hi, sp_lo], axis=0))

    def values(j, hh):
        return vt_ref[0, j, hh * HEAD_DIM:(hh + 1) * HEAD_DIM, :]

    streams = [(ti, hh) for ti in range(nt) for hh in range(nh)]
    units = ([(ti, hh, first_tile + ti, True) for ti, hh in streams]
             + [(ti, hh, jnp.maximum(first_tile + ti - 1, 0), False) for ti, hh in streams])
    z2s, sp2s, inners, weights, totals, outs = {}, {}, {}, {}, {}, {}
    for step in range(len(units) + 4):
        if step < len(units):
            ti, hh, j, _ = units[step]
            z2s[step] = _dot(key_block(j), qm[ti, hh])
        u = step - 1
        if 0 <= u < len(units):
            sp2 = _softplus2(z2s[u])
            sp2s[u] = jnp.where(strict, sp2, 0.0) if units[u][3] else sp2
        u = step - 2
        if 0 <= u < len(units):
            inners[u] = later_sums(sp2s[u])
        u = step - 3
        if 0 <= u < len(units):
            ti, _, _, own = units[u]
            totals[u] = inners[u][0:1] + sp2s[u][0:1]
            expo = z2s[u] - sp2s[u] - inners[u]
            if own:
                weights[u] = jnp.where(strict, jnp.exp2(expo), 0.0)
            else:
                own_total = totals[u - len(streams)]
                weights[u] = jnp.exp2(expo - jnp.where(first_tile + ti > 0, own_total, -MASKED))
        u = step - 4
        if 0 <= u < len(units):
            _, hh, j, _ = units[u]
            outs[u] = _dot(values(j, hh), weights[u].astype(jnp.bfloat16))
    for c, (ti, hh) in enumerate(streams):
        acc_ref[ti, hh] = outs[c] + outs[len(streams) + c]
        suffix_ref[ti, hh] = totals[c] + totals[len(streams) + c]

    for ti in range(nt):
        def smallest_suffix(ti=ti):
            return functools.reduce(jnp.minimum, [jnp.min(suffix_ref[ti, hh]) for hh in range(nh)])

        def more_to_do(carry):
            j, smallest = carry
            return jnp.logical_and(j >= 0, smallest <= SB_SUFFIX_CUTOFF)

        def farther_block(carry, ti=ti, smallest_suffix=smallest_suffix):
            j, _ = carry
            for hh in range(nh):
                z2 = _dot(key_block(j), qm[ti, hh])
                sp2 = _softplus2(z2)
                inner = later_sums(sp2)
                w = jnp.exp2(z2 - sp2 - inner - suffix_ref[ti, hh])
                acc_ref[ti, hh] += _dot(values(j, hh), w.astype(jnp.bfloat16))
                suffix_ref[ti, hh] += inner[0:1] + sp2[0:1]
            return j - 1, smallest_suffix()

        lax.while_loop(more_to_do, farther_block, (first_tile + ti - 2, smallest_suffix()))
        _store_heads(o_ref.at[:, ti * tile:(ti + 1) * tile], [acc_ref[ti, hh] for hh in range(nh)])


def _sb_attention(qt, k, vt):
    b, d, s = qt.shape
    nh = SB_HEADS
    nt = SB_TILES
    gw = nh * HEAD_DIM
    n_blk = s // ATT_TILE
    assert n_blk % nt == 0
    later = np.triu(np.ones((ATT_TILE, ATT_TILE), np.float32), 1)
    later2 = jnp.asarray(np.concatenate([later, later], axis=1), jnp.bfloat16)
    return pl.pallas_call(
        functools.partial(_sb_kernel, nh=nh, nt=nt),
        grid=(b, d // gw, n_blk // nt),
        in_specs=[
            pl.BlockSpec((ATT_TILE, 2 * ATT_TILE), lambda i, j, t: (0, 0)),
            pl.BlockSpec((1, gw, nt * ATT_TILE), lambda i, j, t: (i, j, t)),
            pl.BlockSpec((1, s, gw), lambda i, j, t: (i, 0, j)),
            pl.BlockSpec((1, n_blk, gw, ATT_TILE), lambda i, j, t: (i, 0, j, 0)),
        ],
        out_specs=pl.BlockSpec((1, nt * ATT_TILE, gw), lambda i, j, t: (i, t, j)),
        out_shape=jax.ShapeDtypeStruct((b, s, d), jnp.bfloat16),
        scratch_shapes=[
            pltpu.VMEM((nt, nh, 1, ATT_TILE), jnp.float32),
            pltpu.VMEM((nt, nh, HEAD_DIM, ATT_TILE), jnp.float32),
        ],
        compiler_params=pltpu.CompilerParams(
            dimension_semantics=("arbitrary", "arbitrary", "arbitrary"),
            vmem_limit_bytes=VMEM_LIMIT),
        name="stick_breaking_attention",
    )(later2, qt, k, vt)


def kernel(x, norm_g, w_in, w_out, final_g):
    b, s, d = x.shape
    depth = norm_g.shape[0]
    assert depth >= 1 and d == N_HEADS * HEAD_DIM and s % PROJ_ROWS == 0 and PROJ_ROWS % ATT_TILE == 0
    slopes = jnp.asarray(2.0 ** (-8.0 * np.arange(1, N_HEADS + 1) / N_HEADS), jnp.float32)

    def proj_weights(i):
        wq, wk, wv, wz = (w_in[i, :, c * d:(c + 1) * d] for c in range(4))
        w_tok = jnp.concatenate([wk, wz], axis=1).astype(jnp.bfloat16)
        w_feat_t = jnp.concatenate([wq, wv], axis=1).T.astype(jnp.bfloat16)
        return w_tok, w_feat_t

    h = x
    k, z, qt, vt = _layer_boundary(h, norm_g[0], proj=proj_weights(0))
    for i in range(depth):
        o_parts = _moba_attention(qt, k, vt, slopes) if i % 2 == 0 else [_sb_attention(qt, k, vt)]
        attn = (o_parts, z, w_out[i].astype(jnp.bfloat16))
        if i + 1 < depth:
            h, k, z, qt, vt = _layer_boundary(h, norm_g[i + 1], attn=attn, proj=proj_weights(i + 1))
        else:
            (h,) = _layer_boundary(h, final_g, attn=attn)
    return h
```

```python
import functools
import math

import numpy as np
import jax
import jax.numpy as jnp
from jax import lax
from jax.experimental import pallas as pl
from jax.experimental.pallas import tpu as pltpu

N_HEADS = 16
HEAD_DIM = 64
MOBA_BLOCK = 256
MOBA_TOPK = 3
NORM_EPS = 1e-6

LANES = 128
BF16_ROWS = 16
ATT_TILE = MOBA_BLOCK
PROJ_ROWS = 512
MOBA_HEADS = 2
SB_HEADS = 4
SB_TILES = 4
SOFTMAX_LAG = 2
LOG2E = math.log2(math.e)
MASKED = -1e30
EXP2_ARG_MAX = 126.0
SB_SUFFIX_CUTOFF = 128.0 * LOG2E
VMEM_LIMIT = 56 * 1024 * 1024


def _dot(a, b):
    return jnp.dot(a, b, preferred_element_type=jnp.float32)


def _dot_nt(a, b):
    return lax.dot_general(a, b, (((1,), (1,)), ((), ())), preferred_element_type=jnp.float32)


def _bf16_round(x):
    return x.astype(jnp.bfloat16).astype(jnp.float32)


def _rms_norm(h, g_ref):
    return h * lax.rsqrt(jnp.mean(h * h, axis=-1, keepdims=True) + NORM_EPS) * g_ref[...]


def _boundary_kernel(*refs, n_o, half_steps, has_proj, d, scale):
    refs = list(refs)
    if n_o:
        o_refs = [refs.pop(0) for _ in range(n_o)]
        z_ref, h_ref, wout_ref = refs.pop(0), refs.pop(0), refs.pop(0)
    else:
        h_ref = refs.pop(0)
    g_ref = refs.pop(0)
    if has_proj:
        wk_ref, wz_ref, wqt_ref, wvt_ref = (refs.pop(0) for _ in range(4))
    if n_o:
        hout_ref = refs.pop(0)

    h = h_ref[0]
    if n_o:
        if n_o == 1:
            o = o_refs[0][0]
        else:
            o = jnp.where(pl.program_id(1) < half_steps, o_refs[0][0], o_refs[1][0])
        z = z_ref[0].astype(jnp.float32)
        gated = o.astype(jnp.float32) * (z / (1.0 + jnp.exp(-z)))
        h = h + _dot(gated.astype(jnp.bfloat16), wout_ref[0])
    y = _rms_norm(h, g_ref)
    if n_o:
        hout_ref[0] = h if has_proj else y
    if has_proj:
        k_ref, znext_ref, qt_ref, vt_ref = refs
        xn = y.astype(jnp.bfloat16)
        k_ref[0] = _dot(xn, wk_ref[0]).astype(k_ref.dtype)
        znext_ref[0] = _dot(xn, wz_ref[0]).astype(znext_ref.dtype)
        qt_ref[0] = (_dot_nt(wqt_ref[0], xn) * scale).astype(qt_ref.dtype)
        v_t = _dot_nt(wvt_ref[0], xn)
        for c in range(vt_ref.shape[1]):
            vt_ref[0, c] = v_t[:, c * ATT_TILE:(c + 1) * ATT_TILE].astype(vt_ref.dtype)


def _layer_boundary(h, gain, attn=None, proj=None):
    b, s, d = h.shape
    rows = PROJ_ROWS
    n_steps = s // rows
    half = n_steps // 2
    n_blk = s // ATT_TILE
    row_spec = pl.BlockSpec((1, rows, d), lambda i, t: (i, t, 0))

    def const_spec(shape):
        return pl.BlockSpec(shape, lambda i, t: (0,) * len(shape), pipeline_mode=pl.Buffered(1))

    in_specs, args, out_specs, out_shape = [], [], [], []
    n_o = 0
    if attn is not None:
        o_parts, z, w_out, out_layer = attn
        n_o = len(o_parts)
        if n_o == 1:
            in_specs.append(row_spec)
        else:
            in_specs += [
                pl.BlockSpec((1, rows, d), lambda i, t: (i, jnp.minimum(t, half - 1), 0)),
                pl.BlockSpec((1, rows, d), lambda i, t: (i, jnp.clip(n_steps - 1 - t, 0, half - 1), 0)),
            ]
        in_specs += [row_spec, row_spec,
                     pl.BlockSpec((1, d, d), lambda i, t: (out_layer, 0, 0), pipeline_mode=pl.Buffered(1))]
        args += list(o_parts) + [z, h, w_out]
        out_specs.append(row_spec)
        out_shape.append(jax.ShapeDtypeStruct((b, s, d), jnp.float32))
    else:
        in_specs.append(row_spec)
        args.append(h)
    in_specs.append(const_spec((1, d)))
    args.append(gain.reshape(1, d))
    if proj is not None:
        w_in, w_in_t, layer = proj

        def weight_spec(row_block, col_block):
            return pl.BlockSpec((1, d, d), lambda i, t: (layer, row_block, col_block),
                                pipeline_mode=pl.Buffered(1))

        in_specs += [weight_spec(0, 1), weight_spec(0, 3), weight_spec(0, 0), weight_spec(2, 0)]
        args += [w_in, w_in, w_in_t, w_in_t]
        out_specs += [
            row_spec,
            row_spec,
            pl.BlockSpec((1, d, rows), lambda i, t: (i, 0, t)),
            pl.BlockSpec((1, rows // ATT_TILE, d, ATT_TILE), lambda i, t: (i, t, 0, 0)),
        ]
        out_shape += [
            jax.ShapeDtypeStruct((b, s, d), jnp.bfloat16),
            jax.ShapeDtypeStruct((b, s, d), jnp.bfloat16),
            jax.ShapeDtypeStruct((b, d, s), jnp.bfloat16),
            jax.ShapeDtypeStruct((b, n_blk, d, ATT_TILE), jnp.bfloat16),
        ]
    kern = functools.partial(_boundary_kernel, n_o=n_o, half_steps=half, has_proj=proj is not None,
                             d=d, scale=HEAD_DIM ** -0.5 * LOG2E)
    name = ("gate_outproj_" if attn is not None else "") + ("rmsnorm_qkvz_proj" if proj is not None else "final_rmsnorm")
    return pl.pallas_call(
        kern,
        grid=(b, n_steps),
        in_specs=in_specs,
        out_specs=out_specs,
        out_shape=out_shape,
        compiler_params=pltpu.CompilerParams(
            dimension_semantics=("arbitrary", "arbitrary"), vmem_limit_bytes=VMEM_LIMIT),
        name=name,
    )(*args)


def _head_masked(qt, hh):
    row = lax.broadcasted_iota(jnp.int32, qt.shape, 0)
    return jnp.where((row >= hh * HEAD_DIM) & (row < (hh + 1) * HEAD_DIM), qt, jnp.zeros_like(qt))


def _store_heads(o_ref, outs_t):
    stacked = jnp.concatenate(outs_t, axis=0)
    o_ref[0] = stacked.T.astype(o_ref.dtype)


def _moba_key_features(n_blk):
    pos = np.arange(n_blk * ATT_TILE)
    feat = np.zeros((n_blk * ATT_TILE, HEAD_DIM), np.float32)
    feat[pos, pos // ATT_TILE] = 1.0
    feat[:, n_blk:n_blk + 3] = (pos // ATT_TILE)[:, None]
    feat[:, n_blk + 3:n_blk + 6] = (pos % ATT_TILE)[:, None]
    return jnp.asarray(np.concatenate([feat, feat], axis=1), jnp.bfloat16)


def _moba_kernel(slopes_ref, kfeat_ref, qa_ref, qb_ref, k_ref, vt_ref, olo_ref, ohi_ref,
                 kmean_ref, kaug_ref, qaug_ref, m_ref, acc_ref):
    nh = MOBA_HEADS
    hg = pl.program_id(1)
    p = pl.program_id(2)
    n_blk = kmean_ref.shape[0]
    tile = ATT_TILE
    q_tile = (p, n_blk - 1 - p)
    q_refs = (qa_ref, qb_ref)
    pairs = [(sel, hh) for sel in range(2) for hh in range(nh)]
    key_pos = lax.broadcasted_iota(jnp.int32, (tile, tile), 0)
    qry_pos = lax.broadcasted_iota(jnp.int32, (tile, tile), 1)

    def in_head(index, hh):
        return (index >= hh * HEAD_DIM) & (index < (hh + 1) * HEAD_DIM)

    @pl.when(p == 0)
    def _():
        k_all = k_ref[0]
        kmean_ref[...] = jnp.mean(k_all.astype(jnp.float32).reshape(n_blk, tile, LANES), axis=1)
        for hh in range(nh):
            kaug_ref[hh] = kfeat_ref[...]
            kaug_ref[hh, :, hh * HEAD_DIM:(hh + 1) * HEAD_DIM] = k_all[:, hh * HEAD_DIM:(hh + 1) * HEAD_DIM]

    crow = lax.broadcasted_iota(jnp.int32, (HEAD_DIM - n_blk, tile), 0)
    no_choice = jnp.zeros((n_blk, tile), jnp.float32)
    for hh in range(nh):
        slope = slopes_ref[hg * nh + hh] * LOG2E
        base = jnp.where(crow < 3, slope * tile, jnp.where(crow < 6, slope, 0.0))
        part0 = _bf16_round(base)
        part1 = _bf16_round(base - part0)
        part2 = base - part0 - part1
        slope_rows = jnp.where((crow == 0) | (crow == 3), part0,
                               jnp.where((crow == 1) | (crow == 4), part1, part2))
        bias_rows = jnp.concatenate([no_choice, slope_rows], axis=0).astype(jnp.bfloat16)
        for sel in range(2):
            q_rows = q_refs[sel][0, hh * HEAD_DIM:(hh + 1) * HEAD_DIM, :]
            qaug_ref[sel, hh] = jnp.concatenate([q_rows, bias_rows] if hh == 0 else [bias_rows, q_rows], axis=0)

    kmean = kmean_ref[...]
    kmean_lane = lax.broadcasted_iota(jnp.int32, kmean.shape, 1)
    gate_lhs = []
    for hh in range(nh):
        kmean_h = jnp.where(in_head(kmean_lane, hh), kmean, 0.0)
        kmean_hi = kmean_h.astype(jnp.bfloat16)
        gate_lhs += [kmean_hi, (kmean_h - kmean_hi.astype(jnp.float32)).astype(jnp.bfloat16)]
    gate_lhs = jnp.concatenate(gate_lhs, axis=0)
    gate_parts = []
    for sel in range(2):
        g = _dot(gate_lhs, q_refs[sel][0])
        gate_parts += [g[2 * hh * n_blk:(2 * hh + 1) * n_blk] + g[(2 * hh + 1) * n_blk:(2 * hh + 2) * n_blk]
                       for hh in range(nh)]
    gates = jnp.concatenate(gate_parts, axis=1)

    def store_block_choice():
        blk = lax.broadcasted_iota(jnp.int32, gates.shape, 0).astype(jnp.float32)
        lane = lax.broadcasted_iota(jnp.int32, gates.shape, 1)
        n_past = jnp.where(lane < nh * tile, q_tile[0], q_tile[1]).astype(jnp.float32)
        gate = jnp.where(blk < n_past, gates, -jnp.inf)
        chosen = blk == n_past
        for _ in range(MOBA_TOPK):
            top = jnp.max(gate, axis=0, keepdims=True)
            at_top = (gate == top) & (top > -jnp.inf)
            first = jnp.min(jnp.where(at_top, blk, float(n_blk)), axis=0, keepdims=True)
            pick = blk == first
            chosen = jnp.logical_or(chosen, pick)
            gate = jnp.where(pick, -jnp.inf, gate)
        choice = jnp.where(chosen, 0.0, MASKED).astype(jnp.bfloat16)
        for c, (sel, hh) in enumerate(pairs):
            first_bias_row = (1 - hh) * HEAD_DIM
            qaug_ref[sel, hh, first_bias_row:first_bias_row + n_blk] = choice[:, c * tile:(c + 1) * tile]

    def key_block(j, hh):
        return kaug_ref[hh, pl.ds(pl.multiple_of(j * tile, tile), tile), :]

    def slot_of(s):
        is_b = s >= p
        return is_b.astype(jnp.int32), jnp.where(is_b, s - p, s)

    causal = key_pos <= qry_pos
    ones_rows = jnp.ones((BF16_ROWS, tile), jnp.bfloat16)

    def weighted_values(j, hh, prob):
        v_aug = jnp.concatenate([vt_ref[0, j, hh * HEAD_DIM:(hh + 1) * HEAD_DIM, :], ones_rows], axis=0)
        return _dot(v_aug, prob.astype(jnp.bfloat16))

    def finish(sel, o_ref):
        _store_heads(o_ref, [acc_ref[sel, hh, :HEAD_DIM] * (1.0 / acc_ref[sel, hh, HEAD_DIM:HEAD_DIM + 1])
                             for hh in range(nh)])

    items = [(True, sel) for sel in range(2)] + [(False, s) for s in range(n_blk - 1)]
    last_item_of_first_tile = items.index((False, n_blk // 2 - 2)) if n_blk >= 4 else 1
    in_flight = {}
    for step in range(len(items) + SOFTMAX_LAG):
        if step < len(items):
            own, idx = items[step]
            if not own and items[step - 1][0]:
                store_block_choice()
            sel, j = (idx, q_tile[idx]) if own else slot_of(idx)
            logits = [_dot(key_block(j, hh), qaug_ref[sel, hh]) for hh in range(nh)]
            if own:
                logits = [jnp.where(causal, lg, MASKED) for lg in logits]
            in_flight[step] = (own, sel, j, logits)
        done = step - SOFTMAX_LAG
        if done >= 0:
            own, sel, j, logits = in_flight.pop(done)
            for hh in range(nh):
                top = jnp.max(logits[hh], axis=0, keepdims=True)
                if own:
                    m_ref[sel, hh] = top
                    acc_ref[sel, hh] = weighted_values(j, hh, jnp.exp2(logits[hh] - top))
                else:
                    m_old = m_ref[sel, hh]
                    m_new = jnp.maximum(m_old, top)
                    m_ref[sel, hh] = m_new
                    acc_ref[sel, hh] = (jnp.exp2(m_old - m_new) * acc_ref[sel, hh]
                                        + weighted_values(j, hh, jnp.exp2(logits[hh] - m_new)))
            if done == last_item_of_first_tile:
                finish(0, olo_ref)
    finish(1, ohi_ref)


def _moba_attention(qt, k, vt, slopes):
    b, d, s = qt.shape
    nh = MOBA_HEADS
    gw = nh * HEAD_DIM
    n_blk = s // ATT_TILE
    assert gw == LANES and n_blk % 2 == 0 and PROJ_ROWS == 2 * ATT_TILE and HEAD_DIM >= n_blk + 6
    half = jax.ShapeDtypeStruct((b, s // 2, d), jnp.bfloat16)
    return pl.pallas_call(
        _moba_kernel,
        grid=(b, d // gw, n_blk // 2),
        in_specs=[
            pl.BlockSpec(memory_space=pltpu.SMEM),
            pl.BlockSpec((s, LANES), lambda i, j, t: (0, 0)),
            pl.BlockSpec((1, gw, ATT_TILE), lambda i, j, t: (i, j, t)),
            pl.BlockSpec((1, gw, ATT_TILE), lambda i, j, t: (i, j, n_blk - 1 - t)),
            pl.BlockSpec((1, s, gw), lambda i, j, t: (i, 0, j)),
            pl.BlockSpec((1, n_blk, gw, ATT_TILE), lambda i, j, t: (i, 0, j, 0)),
        ],
        out_specs=[
            pl.BlockSpec((1, ATT_TILE, gw), lambda i, j, t: (i, t, j)),
            pl.BlockSpec((1, ATT_TILE, gw), lambda i, j, t: (i, lax.bitwise_xor(t, 1), j)),
        ],
        out_shape=[half, half],
        scratch_shapes=[
            pltpu.VMEM((n_blk, gw), jnp.float32),
            pltpu.VMEM((nh, s, LANES), jnp.bfloat16),
            pltpu.VMEM((2, nh, LANES, ATT_TILE), jnp.bfloat16),
            pltpu.VMEM((2, nh, 1, ATT_TILE), jnp.float32),
            pltpu.VMEM((2, nh, HEAD_DIM + BF16_ROWS, ATT_TILE), jnp.float32),
        ],
        compiler_params=pltpu.CompilerParams(
            dimension_semantics=("arbitrary", "arbitrary", "arbitrary"),
            vmem_limit_bytes=VMEM_LIMIT),
        name="moba_attention",
    )(slopes, _moba_key_features(n_blk), qt, qt, k, vt)


def _softplus2(z2):
    return jnp.maximum(z2, jnp.log(1.0 + jnp.exp2(jnp.minimum(z2, EXP2_ARG_MAX))) * LOG2E)


def _sb_kernel(later_ref, qt_ref, k_ref, vt_ref, o_ref, suffix_ref, acc_ref, *, nh, nt):
    first_tile = pl.program_id(2) * nt
    tile = ATT_TILE
    heads_per_group = LANES // HEAD_DIM

    def lane_group(hh):
        first = hh // heads_per_group * LANES
        return slice(first, first + LANES)

    qm = {(ti, hh): _head_masked(qt_ref[0, lane_group(hh), ti * tile:(ti + 1) * tile], hh % heads_per_group)
          for ti in range(nt) for hh in range(nh)}
    key_pos = lax.broadcasted_iota(jnp.int32, (tile, tile), 0)
    qry_pos = lax.broadcasted_iota(jnp.int32, (tile, tile), 1)
    strict = key_pos < qry_pos

    def key_block(j, hh):
        return k_ref[0, pl.ds(pl.multiple_of(j * tile, tile), tile), lane_group(hh)]

    def later_sums(sp2):
        return _dot(later_ref[...], sp2.astype(jnp.bfloat16))

    def values(j, hh):
        return vt_ref[0, j, hh * HEAD_DIM:(hh + 1) * HEAD_DIM, :]

    streams = [(ti, hh) for ti in range(nt) for hh in range(nh)]
    units = ([(ti, hh, first_tile + ti, True) for ti, hh in streams]
             + [(ti, hh, jnp.maximum(first_tile + ti - 1, 0), False) for ti, hh in streams])
    z2s, sp2s, inners, weights, totals, outs = {}, {}, {}, {}, {}, {}
    for step in range(len(units) + 4):
        if step < len(units):
            ti, hh, j, own = units[step]
            z2 = _dot(key_block(j, hh), qm[ti, hh])
            z2s[step] = jnp.where(strict, z2, MASKED) if own else z2
        u = step - 1
        if 0 <= u < len(units):
            sp2s[u] = _softplus2(z2s[u])
        u = step - 2
        if 0 <= u < len(units):
            inners[u] = later_sums(sp2s[u])
        u = step - 3
        if 0 <= u < len(units):
            ti, _, _, own = units[u]
            totals[u] = inners[u][0:1] + sp2s[u][0:1]
            expo = z2s[u] - sp2s[u] - inners[u]
            if own:
                weights[u] = jnp.exp2(expo)
            else:
                own_total = totals[u - len(streams)]
                weights[u] = jnp.exp2(expo - jnp.where(first_tile + ti > 0, own_total, -MASKED))
        u = step - 4
        if 0 <= u < len(units):
            _, hh, j, _ = units[u]
            outs[u] = _dot(values(j, hh), weights[u].astype(jnp.bfloat16))
    for c, (ti, hh) in enumerate(streams):
        acc_ref[ti, hh] = outs[c] + outs[len(streams) + c]
        suffix_ref[ti, hh] = totals[c] + totals[len(streams) + c]

    for ti in range(nt):
        def smallest_suffix(ti=ti):
            return functools.reduce(jnp.minimum, [jnp.min(suffix_ref[ti, hh]) for hh in range(nh)])

        def more_to_do(carry):
            j, smallest = carry
            return jnp.logical_and(j >= 0, smallest <= SB_SUFFIX_CUTOFF)

        def farther_block(carry, ti=ti, smallest_suffix=smallest_suffix):
            j, _ = carry
            for hh in range(nh):
                z2 = _dot(key_block(j, hh), qm[ti, hh])
                sp2 = _softplus2(z2)
                inner = later_sums(sp2)
                w = jnp.exp2(z2 - sp2 - inner - suffix_ref[ti, hh])
                acc_ref[ti, hh] += _dot(values(j, hh), w.astype(jnp.bfloat16))
                suffix_ref[ti, hh] += inner[0:1] + sp2[0:1]
            return j - 1, smallest_suffix()

        lax.while_loop(more_to_do, farther_block, (first_tile + ti - 2, smallest_suffix()))
        _store_heads(o_ref.at[:, ti * tile:(ti + 1) * tile], [acc_ref[ti, hh] for hh in range(nh)])


def _sb_attention(qt, k, vt):
    b, d, s = qt.shape
    nh = SB_HEADS
    nt = SB_TILES
    gw = nh * HEAD_DIM
    n_blk = s // ATT_TILE
    assert n_blk % nt == 0
    later = jnp.asarray(np.triu(np.ones((ATT_TILE, ATT_TILE), np.float32), 1), jnp.bfloat16)
    return pl.pallas_call(
        functools.partial(_sb_kernel, nh=nh, nt=nt),
        grid=(b, d // gw, n_blk // nt),
        in_specs=[
            pl.BlockSpec((ATT_TILE, ATT_TILE), lambda i, j, t: (0, 0)),
            pl.BlockSpec((1, gw, nt * ATT_TILE), lambda i, j, t: (i, j, t)),
            pl.BlockSpec((1, s, gw), lambda i, j, t: (i, 0, j)),
            pl.BlockSpec((1, n_blk, gw, ATT_TILE), lambda i, j, t: (i, 0, j, 0)),
        ],
        out_specs=pl.BlockSpec((1, nt * ATT_TILE, gw), lambda i, j, t: (i, t, j)),
        out_shape=jax.ShapeDtypeStruct((b, s, d), jnp.bfloat16),
        scratch_shapes=[
            pltpu.VMEM((nt, nh, 1, ATT_TILE), jnp.float32),
            pltpu.VMEM((nt, nh, HEAD_DIM, ATT_TILE), jnp.float32),
        ],
        compiler_params=pltpu.CompilerParams(
            dimension_semantics=("arbitrary", "arbitrary", "arbitrary"),
            vmem_limit_bytes=VMEM_LIMIT),
        name="stick_breaking_attention",
    )(later, qt, k, vt)


def kernel(x, norm_g, w_in, w_out, final_g):
    b, s, d = x.shape
    depth = norm_g.shape[0]
    assert depth >= 1 and d == N_HEADS * HEAD_DIM and s % PROJ_ROWS == 0 and PROJ_ROWS % ATT_TILE == 0
    slopes = jnp.asarray(2.0 ** (-8.0 * np.arange(1, N_HEADS + 1) / N_HEADS), jnp.float32)

    w_in_bf = w_in.astype(jnp.bfloat16)
    w_in_t = jnp.swapaxes(w_in_bf, 1, 2)
    w_out_bf = w_out.astype(jnp.bfloat16)

    h = x
    k, z, qt, vt = _layer_boundary(h, norm_g[0], proj=(w_in_bf, w_in_t, 0))
    for i in range(depth):
        o_parts = _moba_attention(qt, k, vt, slopes) if i % 2 == 0 else [_sb_attention(qt, k, vt)]
        attn = (o_parts, z, w_out_bf, i)
        if i + 1 < depth:
            h, k, z, qt, vt = _layer_boundary(h, norm_g[i + 1], attn=attn, proj=(w_in_bf, w_in_t, i + 1))
        else:
            (h,) = _layer_boundary(h, final_g, attn=attn)
    return h
```

```python
import functools
import math

import numpy as np
import jax
import jax.numpy as jnp
from jax import lax
from jax.experimental import pallas as pl
from jax.experimental.pallas import tpu as pltpu

N_HEADS = 16
HEAD_DIM = 64
MOBA_BLOCK = 256
MOBA_TOPK = 3
NORM_EPS = 1e-6

LANES = 128
BF16_ROWS = 16
ATT_TILE = MOBA_BLOCK
PROJ_ROWS = 512
MOBA_HEADS = 2
SB_HEADS = 4
SB_TILES = 4
SOFTMAX_LAG = 2
SB_STAGE_LAGS = (1, 2, 3, 4)
LOG2E = math.log2(math.e)
MASKED = -1e30
EXP2_ARG_MAX = 126.0
SB_SUFFIX_CUTOFF = 128.0 * LOG2E
VMEM_LIMIT = 56 * 1024 * 1024


def _dot(a, b):
    return jnp.dot(a, b, preferred_element_type=jnp.float32)


def _dot_nt(a, b):
    return lax.dot_general(a, b, (((1,), (1,)), ((), ())), preferred_element_type=jnp.float32)


def _bf16_round(x):
    return x.astype(jnp.bfloat16).astype(jnp.float32)


def _rms_norm(h, g_ref):
    return h * lax.rsqrt(jnp.mean(h * h, axis=-1, keepdims=True) + NORM_EPS) * g_ref[...]


def _boundary_kernel(*refs, n_o, half_steps, has_proj, d, scale):
    refs = list(refs)
    if n_o:
        o_refs = [refs.pop(0) for _ in range(n_o)]
        z_ref, h_ref, wout_ref = refs.pop(0), refs.pop(0), refs.pop(0)
    else:
        h_ref = refs.pop(0)
    g_ref = refs.pop(0)
    if has_proj:
        wk_ref, wz_ref, wqt_ref, wvt_ref = (refs.pop(0) for _ in range(4))
    if n_o:
        hout_ref = refs.pop(0)

    h = h_ref[0]
    if n_o:
        if n_o == 1:
            o = o_refs[0][0]
        else:
            o = jnp.where(pl.program_id(1) < half_steps, o_refs[0][0], o_refs[1][0])
        z = z_ref[0].astype(jnp.float32)
        gated = o.astype(jnp.float32) * (z / (1.0 + jnp.exp(-z)))
        h = h + _dot(gated.astype(jnp.bfloat16), wout_ref[0])
    y = _rms_norm(h, g_ref)
    if n_o:
        hout_ref[0] = h if has_proj else y
    if has_proj:
        k_ref, znext_ref, qt_ref, vt_ref = refs
        xn = y.astype(jnp.bfloat16)
        k_ref[0] = _dot(xn, wk_ref[0].astype(jnp.bfloat16)).astype(k_ref.dtype)
        znext_ref[0] = _dot(xn, wz_ref[0].astype(jnp.bfloat16)).astype(znext_ref.dtype)
        qt_ref[0] = (_dot_nt(wqt_ref[0], xn) * scale).astype(qt_ref.dtype)
        v_t = _dot_nt(wvt_ref[0], xn)
        for c in range(vt_ref.shape[1]):
            vt_ref[0, c] = v_t[:, c * ATT_TILE:(c + 1) * ATT_TILE].astype(vt_ref.dtype)


def _layer_boundary(h, gain, attn=None, proj=None):
    b, s, d = h.shape
    rows = PROJ_ROWS
    n_steps = s // rows
    half = n_steps // 2
    n_blk = s // ATT_TILE
    row_spec = pl.BlockSpec((1, rows, d), lambda i, t: (i, t, 0))

    def const_spec(shape):
        return pl.BlockSpec(shape, lambda i, t: (0,) * len(shape), pipeline_mode=pl.Buffered(1))

    in_specs, args, out_specs, out_shape = [], [], [], []
    n_o = 0
    if attn is not None:
        o_parts, z, w_out, out_layer = attn
        n_o = len(o_parts)
        if n_o == 1:
            in_specs.append(row_spec)
        else:
            in_specs += [
                pl.BlockSpec((1, rows, d), lambda i, t: (i, jnp.minimum(t, half - 1), 0)),
                pl.BlockSpec((1, rows, d), lambda i, t: (i, jnp.clip(n_steps - 1 - t, 0, half - 1), 0)),
            ]
        in_specs += [row_spec, row_spec,
                     pl.BlockSpec((1, d, d), lambda i, t: (out_layer, 0, 0), pipeline_mode=pl.Buffered(1))]
        args += list(o_parts) + [z, h, w_out]
        out_specs.append(row_spec)
        out_shape.append(jax.ShapeDtypeStruct((b, s, d), jnp.float32))
    else:
        in_specs.append(row_spec)
        args.append(h)
    in_specs.append(const_spec((1, d)))
    args.append(gain.reshape(1, d))
    if proj is not None:
        w_in, w_q_t, w_v_t, layer = proj

        def weight_spec(col_block):
            return pl.BlockSpec((1, d, d), lambda i, t: (layer, 0, col_block), pipeline_mode=pl.Buffered(1))

        in_specs += [weight_spec(1), weight_spec(3), weight_spec(0), weight_spec(0)]
        args += [w_in, w_in, w_q_t, w_v_t]
        out_specs += [
            row_spec,
            row_spec,
            pl.BlockSpec((1, d, rows), lambda i, t: (i, 0, t)),
            pl.BlockSpec((1, rows // ATT_TILE, d, ATT_TILE), lambda i, t: (i, t, 0, 0)),
        ]
        out_shape += [
            jax.ShapeDtypeStruct((b, s, d), jnp.bfloat16),
            jax.ShapeDtypeStruct((b, s, d), jnp.bfloat16),
            jax.ShapeDtypeStruct((b, d, s), jnp.bfloat16),
            jax.ShapeDtypeStruct((b, n_blk, d, ATT_TILE), jnp.bfloat16),
        ]
    kern = functools.partial(_boundary_kernel, n_o=n_o, half_steps=half, has_proj=proj is not None,
                             d=d, scale=HEAD_DIM ** -0.5 * LOG2E)
    name = ("gate_outproj_" if attn is not None else "") + ("rmsnorm_qkvz_proj" if proj is not None else "final_rmsnorm")
    return pl.pallas_call(
        kern,
        grid=(b, n_steps),
        in_specs=in_specs,
        out_specs=out_specs,
        out_shape=out_shape,
        compiler_params=pltpu.CompilerParams(
            dimension_semantics=("arbitrary", "arbitrary"), vmem_limit_bytes=VMEM_LIMIT),
        name=name,
    )(*args)


def _head_masked(qt, hh):
    row = lax.broadcasted_iota(jnp.int32, qt.shape, 0)
    return jnp.where((row >= hh * HEAD_DIM) & (row < (hh + 1) * HEAD_DIM), qt, jnp.zeros_like(qt))


def _store_heads(o_ref, outs_t):
    stacked = jnp.concatenate(outs_t, axis=0)
    o_ref[0] = stacked.T.astype(o_ref.dtype)


def _moba_key_features(n_blk):
    pos = np.arange(n_blk * ATT_TILE)
    feat = np.zeros((n_blk * ATT_TILE, HEAD_DIM), np.float32)
    feat[pos, pos // ATT_TILE] = 1.0
    feat[:, n_blk:n_blk + 3] = (pos // ATT_TILE)[:, None]
    feat[:, n_blk + 3:n_blk + 6] = (pos % ATT_TILE)[:, None]
    return jnp.asarray(np.concatenate([feat, feat], axis=1), jnp.bfloat16)


def _moba_kernel(slopes_ref, kfeat_ref, qa_ref, qb_ref, k_ref, vt_ref, olo_ref, ohi_ref,
                 kmean_ref, kaug_ref, qaug_ref, m_ref, acc_ref):
    nh = MOBA_HEADS
    hg = pl.program_id(1)
    p = pl.program_id(2)
    n_blk = kmean_ref.shape[0]
    tile = ATT_TILE
    q_tile = (p, n_blk - 1 - p)
    q_refs = (qa_ref, qb_ref)
    pairs = [(sel, hh) for sel in range(2) for hh in range(nh)]
    key_pos = lax.broadcasted_iota(jnp.int32, (tile, tile), 0)
    qry_pos = lax.broadcasted_iota(jnp.int32, (tile, tile), 1)

    heads_per_group = LANES // HEAD_DIM

    def own_half(hh):
        first = hh % heads_per_group * HEAD_DIM
        return slice(first, first + HEAD_DIM)

    def head_cols(hh):
        return slice(hh * HEAD_DIM, (hh + 1) * HEAD_DIM)

    @pl.when(p == 0)
    def _():
        k_all = k_ref[0]
        kmean_ref[...] = jnp.mean(k_all.astype(jnp.float32).reshape(n_blk, tile, nh * HEAD_DIM), axis=1)
        for hh in range(nh):
            kaug_ref[hh] = kfeat_ref[...]
            kaug_ref[hh, :, own_half(hh)] = k_all[:, head_cols(hh)]

    crow = lax.broadcasted_iota(jnp.int32, (HEAD_DIM - n_blk, tile), 0)
    no_choice = jnp.zeros((n_blk, tile), jnp.float32)
    for hh in range(nh):
        slope = slopes_ref[hg * nh + hh] * LOG2E
        base = jnp.where(crow < 3, slope * tile, jnp.where(crow < 6, slope, 0.0))
        part0 = _bf16_round(base)
        part1 = _bf16_round(base - part0)
        part2 = base - part0 - part1
        slope_rows = jnp.where((crow == 0) | (crow == 3), part0,
                               jnp.where((crow == 1) | (crow == 4), part1, part2))
        bias_rows = jnp.concatenate([no_choice, slope_rows], axis=0).astype(jnp.bfloat16)
        for sel in range(2):
            q_rows = q_refs[sel][0, head_cols(hh), :]
            first_half = own_half(hh).start == 0
            qaug_ref[sel, hh] = jnp.concatenate([q_rows, bias_rows] if first_half else [bias_rows, q_rows], axis=0)

    gate_parts = {}
    group_lane = lax.broadcasted_iota(jnp.int32, (n_blk, LANES), 1)
    for group in range(nh // heads_per_group):
        heads = range(group * heads_per_group, (group + 1) * heads_per_group)
        kmean = kmean_ref[:, group * LANES:(group + 1) * LANES]
        gate_lhs = []
        for hh in heads:
            in_own_half = (group_lane >= own_half(hh).start) & (group_lane < own_half(hh).stop)
            kmean_h = jnp.where(in_own_half, kmean, 0.0)
            kmean_hi = kmean_h.astype(jnp.bfloat16)
            gate_lhs += [kmean_hi, (kmean_h - kmean_hi.astype(jnp.float32)).astype(jnp.bfloat16)]
        gate_lhs = jnp.concatenate(gate_lhs, axis=0)
        for sel in range(2):
            g = _dot(gate_lhs, q_refs[sel][0, group * LANES:(group + 1) * LANES, :])
            for c, hh in enumerate(heads):
                gate_parts[sel, hh] = g[2 * c * n_blk:(2 * c + 1) * n_blk] + g[(2 * c + 1) * n_blk:(2 * c + 2) * n_blk]
    gates = jnp.concatenate([gate_parts[pair] for pair in pairs], axis=1)

    def store_block_choice():
        blk = lax.broadcasted_iota(jnp.int32, gates.shape, 0).astype(jnp.float32)
        lane = lax.broadcasted_iota(jnp.int32, gates.shape, 1)
        n_past = jnp.where(lane < nh * tile, q_tile[0], q_tile[1]).astype(jnp.float32)
        gate = jnp.where(blk < n_past, gates, -jnp.inf)
        chosen = blk == n_past
        for _ in range(MOBA_TOPK):
            top = jnp.max(gate, axis=0, keepdims=True)
            at_top = (gate == top) & (top > -jnp.inf)
            first = jnp.min(jnp.where(at_top, blk, float(n_blk)), axis=0, keepdims=True)
            pick = blk == first
            chosen = jnp.logical_or(chosen, pick)
            gate = jnp.where(pick, -jnp.inf, gate)
        choice = jnp.where(chosen, 0.0, MASKED).astype(jnp.bfloat16)
        for c, (sel, hh) in enumerate(pairs):
            first_bias_row = HEAD_DIM - own_half(hh).start
            qaug_ref[sel, hh, first_bias_row:first_bias_row + n_blk] = choice[:, c * tile:(c + 1) * tile]

    def key_block(j, hh):
        return kaug_ref[hh, pl.ds(pl.multiple_of(j * tile, tile), tile), :]

    def slot_of(s):
        is_b = s >= p
        return is_b.astype(jnp.int32), jnp.where(is_b, s - p, s)

    causal = key_pos <= qry_pos
    ones_rows = jnp.ones((BF16_ROWS, tile), jnp.bfloat16)

    def weighted_values(j, hh, prob):
        v_aug = jnp.concatenate([vt_ref[0, j, hh * HEAD_DIM:(hh + 1) * HEAD_DIM, :], ones_rows], axis=0)
        return _dot(v_aug, prob.astype(jnp.bfloat16))

    def finish(sel, o_ref):
        _store_heads(o_ref, [acc_ref[sel, hh, :HEAD_DIM] * (1.0 / acc_ref[sel, hh, HEAD_DIM:HEAD_DIM + 1])
                             for hh in range(nh)])

    items = [(True, sel) for sel in range(2)] + [(False, s) for s in range(n_blk - 1)]
    last_item_of_first_tile = items.index((False, n_blk // 2 - 2)) if n_blk >= 4 else 1
    in_flight = {}
    for step in range(len(items) + SOFTMAX_LAG):
        if step < len(items):
            own, idx = items[step]
            if not own and items[step - 1][0]:
                store_block_choice()
            sel, j = (idx, q_tile[idx]) if own else slot_of(idx)
            logits = [_dot(key_block(j, hh), qaug_ref[sel, hh]) for hh in range(nh)]
            if own:
                logits = [jnp.where(causal, lg, MASKED) for lg in logits]
            in_flight[step] = (own, sel, j, logits)
        done = step - SOFTMAX_LAG
        if done >= 0:
            own, sel, j, logits = in_flight.pop(done)
            for hh in range(nh):
                top = jnp.max(logits[hh], axis=0, keepdims=True)
                if own:
                    m_ref[sel, hh] = top
                    acc_ref[sel, hh] = weighted_values(j, hh, jnp.exp2(logits[hh] - top))
                else:
                    m_old = m_ref[sel, hh]
                    m_new = jnp.maximum(m_old, top)
                    m_ref[sel, hh] = m_new
                    acc_ref[sel, hh] = (jnp.exp2(m_old - m_new) * acc_ref[sel, hh]
                                        + weighted_values(j, hh, jnp.exp2(logits[hh] - m_new)))
            if done == last_item_of_first_tile:
                finish(0, olo_ref)
    finish(1, ohi_ref)


def _moba_attention(qt, k, vt, slopes):
    b, d, s = qt.shape
    nh = MOBA_HEADS
    gw = nh * HEAD_DIM
    n_blk = s // ATT_TILE
    assert gw % LANES == 0 and n_blk % 2 == 0 and PROJ_ROWS == 2 * ATT_TILE and HEAD_DIM >= n_blk + 6
    half = jax.ShapeDtypeStruct((b, s // 2, d), jnp.bfloat16)
    return pl.pallas_call(
        _moba_kernel,
        grid=(b, d // gw, n_blk // 2),
        in_specs=[
            pl.BlockSpec(memory_space=pltpu.SMEM),
            pl.BlockSpec((s, LANES), lambda i, j, t: (0, 0)),
            pl.BlockSpec((1, gw, ATT_TILE), lambda i, j, t: (i, j, t)),
            pl.BlockSpec((1, gw, ATT_TILE), lambda i, j, t: (i, j, n_blk - 1 - t)),
            pl.BlockSpec((1, s, gw), lambda i, j, t: (i, 0, j)),
            pl.BlockSpec((1, n_blk, gw, ATT_TILE), lambda i, j, t: (i, 0, j, 0)),
        ],
        out_specs=[
            pl.BlockSpec((1, ATT_TILE, gw), lambda i, j, t: (i, t, j)),
            pl.BlockSpec((1, ATT_TILE, gw), lambda i, j, t: (i, lax.bitwise_xor(t, 1), j)),
        ],
        out_shape=[half, half],
        scratch_shapes=[
            pltpu.VMEM((n_blk, gw), jnp.float32),
            pltpu.VMEM((nh, s, LANES), jnp.bfloat16),
            pltpu.VMEM((2, nh, LANES, ATT_TILE), jnp.bfloat16),
            pltpu.VMEM((2, nh, 1, ATT_TILE), jnp.float32),
            pltpu.VMEM((2, nh, HEAD_DIM + BF16_ROWS, ATT_TILE), jnp.float32),
        ],
        compiler_params=pltpu.CompilerParams(
            dimension_semantics=("arbitrary", "arbitrary", "arbitrary"),
            vmem_limit_bytes=VMEM_LIMIT),
        name="moba_attention",
    )(slopes, _moba_key_features(n_blk), qt, qt, k, vt)


def _softplus2(z2):
    return jnp.maximum(z2, jnp.log(1.0 + jnp.exp2(jnp.minimum(z2, EXP2_ARG_MAX))) * LOG2E)


def _sb_kernel(later_ref, qt_ref, k_ref, vt_ref, o_ref, suffix_ref, acc_ref, *, nh, nt):
    first_tile = pl.program_id(2) * nt
    tile = ATT_TILE
    heads_per_group = LANES // HEAD_DIM

    def lane_group(hh):
        first = hh // heads_per_group * LANES
        return slice(first, first + LANES)

    qm = {(ti, hh): _head_masked(qt_ref[0, lane_group(hh), ti * tile:(ti + 1) * tile], hh % heads_per_group)
          for ti in range(nt) for hh in range(nh)}
    key_pos = lax.broadcasted_iota(jnp.int32, (tile, tile), 0)
    qry_pos = lax.broadcasted_iota(jnp.int32, (tile, tile), 1)
    strict = key_pos < qry_pos

    def key_block(j, hh):
        return k_ref[0, pl.ds(pl.multiple_of(j * tile, tile), tile), lane_group(hh)]

    def later_sums(sp2):
        return _dot(later_ref[...], sp2.astype(jnp.bfloat16))

    def values(j, hh):
        return vt_ref[0, j, hh * HEAD_DIM:(hh + 1) * HEAD_DIM, :]

    streams = [(ti, hh) for ti in range(nt) for hh in range(nh)]
    units = ([(ti, hh, first_tile + ti, True) for ti, hh in streams]
             + [(ti, hh, jnp.maximum(first_tile + ti - 1, 0), False) for ti, hh in streams])
    half = tile // 2

    def live_parts(x):
        return x[:half], x[half:, half:]

    def full_tile(parts):
        first, last = parts
        return jnp.concatenate([first, jnp.concatenate([jnp.zeros_like(last), last], axis=1)], axis=0)

    strict_parts = live_parts(strict)
    z2s, sp2s, inners, weights, totals, outs = {}, {}, {}, {}, {}, {}
    lag_softplus, lag_sums, lag_weights, lag_values = SB_STAGE_LAGS
    for step in range(len(units) + lag_values):
        if step < len(units):
            ti, hh, j, own = units[step]
            z2 = _dot(key_block(j, hh), qm[ti, hh])
            if own:
                z2s[step] = [jnp.where(m, part, MASKED) for m, part in zip(strict_parts, live_parts(z2))]
            else:
                z2s[step] = [z2]
        u = step - lag_softplus
        if 0 <= u < len(units):
            sp2s[u] = [_softplus2(part) for part in z2s[u]]
        u = step - lag_sums
        if 0 <= u < len(units):
            sp2_bf = [part.astype(jnp.bfloat16) for part in sp2s[u]]
            inners[u] = later_sums(full_tile(sp2_bf) if units[u][3] else sp2_bf[0])
        u = step - lag_weights
        if 0 <= u < len(units):
            ti, _, _, own = units[u]
            totals[u] = inners[u][0:1] + sp2s[u][0][0:1]
            inner_parts = live_parts(inners[u]) if own else [inners[u]]
            expo = [z2 - sp2 - inner for z2, sp2, inner in zip(z2s[u], sp2s[u], inner_parts)]
            if own:
                weights[u] = full_tile([jnp.exp2(part).astype(jnp.bfloat16) for part in expo])
            else:
                own_total = totals[u - len(streams)]
                suffix = jnp.where(first_tile + ti > 0, own_total, -MASKED)
                weights[u] = jnp.exp2(expo[0] - suffix).astype(jnp.bfloat16)
        u = step - lag_values
        if 0 <= u < len(units):
            _, hh, j, _ = units[u]
            outs[u] = _dot(values(j, hh), weights[u])
    for c, (ti, hh) in enumerate(streams):
        acc_ref[ti, hh] = outs[c] + outs[len(streams) + c]
        suffix_ref[ti, hh] = totals[c] + totals[len(streams) + c]

    for ti in range(nt):
        def smallest_suffix(ti=ti):
            return functools.reduce(jnp.minimum, [jnp.min(suffix_ref[ti, hh]) for hh in range(nh)])

        def more_to_do(carry):
            j, smallest = carry
            return jnp.logical_and(j >= 0, smallest <= SB_SUFFIX_CUTOFF)

        def farther_block(carry, ti=ti, smallest_suffix=smallest_suffix):
            j, _ = carry
            for hh in range(nh):
                z2 = _dot(key_block(j, hh), qm[ti, hh])
                sp2 = _softplus2(z2)
                inner = later_sums(sp2)
                w = jnp.exp2(z2 - sp2 - inner - suffix_ref[ti, hh])
                acc_ref[ti, hh] += _dot(values(j, hh), w.astype(jnp.bfloat16))
                suffix_ref[ti, hh] += inner[0:1] + sp2[0:1]
            return j - 1, smallest_suffix()

        lax.while_loop(more_to_do, farther_block, (first_tile + ti - 2, smallest_suffix()))
        _store_heads(o_ref.at[:, ti * tile:(ti + 1) * tile], [acc_ref[ti, hh] for hh in range(nh)])


def _sb_attention(qt, k, vt):
    b, d, s = qt.shape
    nh = SB_HEADS
    nt = SB_TILES
    gw = nh * HEAD_DIM
    n_blk = s // ATT_TILE
    assert n_blk % nt == 0
    later = jnp.asarray(np.triu(np.ones((ATT_TILE, ATT_TILE), np.float32), 1), jnp.bfloat16)
    return pl.pallas_call(
        functools.partial(_sb_kernel, nh=nh, nt=nt),
        grid=(b, d // gw, n_blk // nt),
        in_specs=[
            pl.BlockSpec((ATT_TILE, ATT_TILE), lambda i, j, t: (0, 0)),
            pl.BlockSpec((1, gw, nt * ATT_TILE), lambda i, j, t: (i, j, t)),
            pl.BlockSpec((1, s, gw), lambda i, j, t: (i, 0, j)),
            pl.BlockSpec((1, n_blk, gw, ATT_TILE), lambda i, j, t: (i, 0, j, 0)),
        ],
        out_specs=pl.BlockSpec((1, nt * ATT_TILE, gw), lambda i, j, t: (i, t, j)),
        out_shape=jax.ShapeDtypeStruct((b, s, d), jnp.bfloat16),
        scratch_shapes=[
            pltpu.VMEM((nt, nh, 1, ATT_TILE), jnp.float32),
            pltpu.VMEM((nt, nh, HEAD_DIM, ATT_TILE), jnp.float32),
        ],
        compiler_params=pltpu.CompilerParams(
            dimension_semantics=("arbitrary", "arbitrary", "arbitrary"),
            vmem_limit_bytes=VMEM_LIMIT),
        name="stick_breaking_attention",
    )(later, qt, k, vt)


def kernel(x, norm_g, w_in, w_out, final_g):
    b, s, d = x.shape
    depth = norm_g.shape[0]
    assert depth >= 1 and d == N_HEADS * HEAD_DIM and s % PROJ_ROWS == 0 and PROJ_ROWS % ATT_TILE == 0
    slopes = jnp.asarray(2.0 ** (-8.0 * np.arange(1, N_HEADS + 1) / N_HEADS), jnp.float32)

    w_q_t = jnp.swapaxes(w_in[:, :, 0:d], 1, 2).astype(jnp.bfloat16)
    w_v_t = jnp.swapaxes(w_in[:, :, 2 * d:3 * d], 1, 2).astype(jnp.bfloat16)
    w_out_bf = w_out.astype(jnp.bfloat16)

    def proj_weights(layer):
        return w_in, w_q_t, w_v_t, layer

    h = x
    k, z, qt, vt = _layer_boundary(h, norm_g[0], proj=proj_weights(0))
    for i in range(depth):
        o_parts = _moba_attention(qt, k, vt, slopes) if i % 2 == 0 else [_sb_attention(qt, k, vt)]
        attn = (o_parts, z, w_out_bf, i)
        if i + 1 < depth:
            h, k, z, qt, vt = _layer_boundary(h, norm_g[i + 1], attn=attn, proj=proj_weights(i + 1))
        else:
            (h,) = _layer_boundary(h, final_g, attn=attn)
    return h
```

```python
import functools
import math

import numpy as np
import jax
import jax.numpy as jnp
from jax import lax
from jax.experimental import pallas as pl
from jax.experimental.pallas import tpu as pltpu

N_HEADS = 16
HEAD_DIM = 64
MOBA_BLOCK = 256
MOBA_TOPK = 3
NORM_EPS = 1e-6

LANES = 128
BF16_ROWS = 16
ATT_TILE = MOBA_BLOCK
PROJ_ROWS = 512
WIDE_ROWS = 1024
MOBA_HEADS = 2
SB_HEADS = 4
SB_TILES = 4
SOFTMAX_LAG = 2
SB_STAGE_LAGS = (1, 2, 3, 4)
LOG2E = math.log2(math.e)
MASKED = -1e30
EXP2_ARG_MAX = 126.0
SB_SUFFIX_CUTOFF = 128.0 * LOG2E
VMEM_LIMIT = 56 * 1024 * 1024


def _dot(a, b):
    return jnp.dot(a, b, preferred_element_type=jnp.float32)


def _dot_nt(a, b):
    return lax.dot_general(a, b, (((1,), (1,)), ((), ())), preferred_element_type=jnp.float32)


def _bf16_round(x):
    return x.astype(jnp.bfloat16).astype(jnp.float32)


def _rms_norm(h, g_ref):
    return h * lax.rsqrt(jnp.mean(h * h, axis=-1, keepdims=True) + NORM_EPS) * g_ref[...]


def _boundary_kernel(*refs, n_o, half_steps, gate_in, has_proj, emit_z, d, scale):
    refs = list(refs)
    if n_o:
        o_refs = [refs.pop(0) for _ in range(n_o)]
        if gate_in:
            z_ref = refs.pop(0)
        else:
            gate_gain_ref, gate_w_ref = refs.pop(0), refs.pop(0)
        h_ref, wout_ref = refs.pop(0), refs.pop(0)
    else:
        h_ref = refs.pop(0)
    g_ref = refs.pop(0)
    if has_proj:
        wk_ref = refs.pop(0)
        wz_ref = refs.pop(0) if emit_z else None
        wqt_ref, wvt_ref = refs.pop(0), refs.pop(0)
    if n_o:
        hout_ref = refs.pop(0)

    h = h_ref[0]
    if n_o:
        if n_o == 1:
            o = o_refs[0][0]
        else:
            o = jnp.where(pl.program_id(1) < half_steps, o_refs[0][0], o_refs[1][0])
        if gate_in:
            z = z_ref[0].astype(jnp.float32)
        else:
            z = _dot(_rms_norm(h, gate_gain_ref).astype(jnp.bfloat16), gate_w_ref[0].astype(jnp.bfloat16))
        gated = o.astype(jnp.float32) * (z / (1.0 + jnp.exp(-z)))
        h = h + _dot(gated.astype(jnp.bfloat16), wout_ref[0])
    y = _rms_norm(h, g_ref)
    if n_o:
        hout_ref[0] = h if has_proj else y
    if has_proj:
        k_ref = refs.pop(0)
        znext_ref = refs.pop(0) if emit_z else None
        qt_ref, vt_ref = refs
        xn = y.astype(jnp.bfloat16)
        k_ref[0] = _dot(xn, wk_ref[0].astype(jnp.bfloat16)).astype(k_ref.dtype)
        if emit_z:
            znext_ref[0] = _dot(xn, wz_ref[0].astype(jnp.bfloat16)).astype(znext_ref.dtype)
        qt_ref[0] = (_dot_nt(wqt_ref[0], xn) * scale).astype(qt_ref.dtype)
        v_t = _dot_nt(wvt_ref[0], xn)
        for c in range(vt_ref.shape[1]):
            vt_ref[0, c] = v_t[:, c * ATT_TILE:(c + 1) * ATT_TILE].astype(vt_ref.dtype)


def _layer_boundary(h, gain, attn=None, proj=None, rows=PROJ_ROWS):
    b, s, d = h.shape
    n_steps = s // rows
    half = n_steps // 2
    n_blk = s // ATT_TILE
    row_spec = pl.BlockSpec((1, rows, d), lambda i, t: (i, t, 0))

    def const_spec(shape):
        return pl.BlockSpec(shape, lambda i, t: (0,) * len(shape), pipeline_mode=pl.Buffered(1))

    def weight_spec(layer, col_block):
        return pl.BlockSpec((1, d, d), lambda i, t: (layer, 0, col_block), pipeline_mode=pl.Buffered(1))

    in_specs, args, out_specs, out_shape = [], [], [], []
    n_o = 0
    gate_in = True
    if attn is not None:
        o_parts, gate, w_out, out_layer = attn
        n_o = len(o_parts)
        if n_o == 1:
            in_specs.append(row_spec)
        else:
            assert rows == PROJ_ROWS
            in_specs += [
                pl.BlockSpec((1, rows, d), lambda i, t: (i, jnp.minimum(t, half - 1), 0)),
                pl.BlockSpec((1, rows, d), lambda i, t: (i, jnp.clip(n_steps - 1 - t, 0, half - 1), 0)),
            ]
        args += list(o_parts)
        gate_in = not isinstance(gate, tuple)
        if gate_in:
            in_specs.append(row_spec)
            args.append(gate)
        else:
            gate_gain, w_in_all = gate
            in_specs += [const_spec((1, d)), weight_spec(out_layer, 3)]
            args += [gate_gain.reshape(1, d), w_in_all]
        in_specs += [row_spec, weight_spec(out_layer, 0)]
        args += [h, w_out]
        out_specs.append(row_spec)
        out_shape.append(jax.ShapeDtypeStruct((b, s, d), jnp.float32))
    else:
        in_specs.append(row_spec)
        args.append(h)
    in_specs.append(const_spec((1, d)))
    args.append(gain.reshape(1, d))
    emit_z = False
    if proj is not None:
        w_in, w_q_t, w_v_t, layer, emit_z = proj
        token_major = (pl.BlockSpec((1, rows, d), lambda i, t: (i, t, 0)),
                       jax.ShapeDtypeStruct((b, s, d), jnp.bfloat16))
        in_specs += [weight_spec(layer, 1)] + ([weight_spec(layer, 3)] if emit_z else [])
        in_specs += [weight_spec(layer, 0), weight_spec(layer, 0)]
        args += [w_in] + ([w_in] if emit_z else []) + [w_q_t, w_v_t]
        outputs = [token_major] + ([token_major] if emit_z else []) + [
            (pl.BlockSpec((1, d, rows), lambda i, t: (i, 0, t)),
             jax.ShapeDtypeStruct((b, d, s), jnp.bfloat16)),
            (pl.BlockSpec((1, rows // ATT_TILE, d, ATT_TILE), lambda i, t: (i, t, 0, 0)),
             jax.ShapeDtypeStruct((b, n_blk, d, ATT_TILE), jnp.bfloat16)),
        ]
        out_specs += [spec for spec, _ in outputs]
        out_shape += [shape for _, shape in outputs]
    kern = functools.partial(_boundary_kernel, n_o=n_o, half_steps=half, gate_in=gate_in,
                             has_proj=proj is not None, emit_z=emit_z, d=d, scale=HEAD_DIM ** -0.5 * LOG2E)
    name = ("gate_outproj_" if attn is not None else "") + ("rmsnorm_qkvz_proj" if proj is not None else "final_rmsnorm")
    return pl.pallas_call(
        kern,
        grid=(b, n_steps),
        in_specs=in_specs,
        out_specs=out_specs,
        out_shape=out_shape,
        compiler_params=pltpu.CompilerParams(
            dimension_semantics=("arbitrary", "arbitrary"), vmem_limit_bytes=VMEM_LIMIT),
        name=name,
    )(*args)


def _head_masked(qt, hh):
    row = lax.broadcasted_iota(jnp.int32, qt.shape, 0)
    return jnp.where((row >= hh * HEAD_DIM) & (row < (hh + 1) * HEAD_DIM), qt, jnp.zeros_like(qt))


def _store_heads(o_ref, outs_t):
    stacked = jnp.concatenate(outs_t, axis=0)
    o_ref[0] = stacked.T.astype(o_ref.dtype)


def _moba_key_features(n_blk):
    pos = np.arange(n_blk * ATT_TILE)
    feat = np.zeros((n_blk * ATT_TILE, HEAD_DIM), np.float32)
    feat[pos, pos // ATT_TILE] = 1.0
    feat[:, n_blk:n_blk + 3] = (pos // ATT_TILE)[:, None]
    feat[:, n_blk + 3:n_blk + 6] = (pos % ATT_TILE)[:, None]
    return jnp.asarray(np.concatenate([feat, feat], axis=1), jnp.bfloat16)


def _moba_kernel(slopes_ref, kfeat_ref, qa_ref, qb_ref, k_ref, vt_ref, olo_ref, ohi_ref,
                 kmean_ref, kaug_ref, qaug_ref, m_ref, acc_ref):
    nh = MOBA_HEADS
    hg = pl.program_id(1)
    p = pl.program_id(2)
    n_blk = kmean_ref.shape[0]
    tile = ATT_TILE
    q_tile = (p, n_blk - 1 - p)
    q_refs = (qa_ref, qb_ref)
    pairs = [(sel, hh) for sel in range(2) for hh in range(nh)]
    key_pos = lax.broadcasted_iota(jnp.int32, (tile, tile), 0)
    qry_pos = lax.broadcasted_iota(jnp.int32, (tile, tile), 1)

    heads_per_group = LANES // HEAD_DIM

    def own_half(hh):
        first = hh % heads_per_group * HEAD_DIM
        return slice(first, first + HEAD_DIM)

    def head_cols(hh):
        return slice(hh * HEAD_DIM, (hh + 1) * HEAD_DIM)

    @pl.when(p == 0)
    def _():
        k_all = k_ref[0]
        kmean_ref[...] = jnp.mean(k_all.astype(jnp.float32).reshape(n_blk, tile, nh * HEAD_DIM), axis=1)
        for hh in range(nh):
            kaug_ref[hh] = kfeat_ref[...]
            kaug_ref[hh, :, own_half(hh)] = k_all[:, head_cols(hh)]

    crow = lax.broadcasted_iota(jnp.int32, (HEAD_DIM - n_blk, tile), 0)
    no_choice = jnp.zeros((n_blk, tile), jnp.float32)
    for hh in range(nh):
        slope = slopes_ref[hg * nh + hh] * LOG2E
        base = jnp.where(crow < 3, slope * tile, jnp.where(crow < 6, slope, 0.0))
        part0 = _bf16_round(base)
        part1 = _bf16_round(base - part0)
        part2 = base - part0 - part1
        slope_rows = jnp.where((crow == 0) | (crow == 3), part0,
                               jnp.where((crow == 1) | (crow == 4), part1, part2))
        bias_rows = jnp.concatenate([no_choice, slope_rows], axis=0).astype(jnp.bfloat16)
        for sel in range(2):
            q_rows = q_refs[sel][0, head_cols(hh), :]
            first_half = own_half(hh).start == 0
            qaug_ref[sel, hh] = jnp.concatenate([q_rows, bias_rows] if first_half else [bias_rows, q_rows], axis=0)

    gate_parts = {}
    group_lane = lax.broadcasted_iota(jnp.int32, (n_blk, LANES), 1)
    for group in range(nh // heads_per_group):
        heads = range(group * heads_per_group, (group + 1) * heads_per_group)
        kmean = kmean_ref[:, group * LANES:(group + 1) * LANES]
        gate_lhs = []
        for hh in heads:
            in_own_half = (group_lane >= own_half(hh).start) & (group_lane < own_half(hh).stop)
            kmean_h = jnp.where(in_own_half, kmean, 0.0)
            kmean_hi = kmean_h.astype(jnp.bfloat16)
            gate_lhs += [kmean_hi, (kmean_h - kmean_hi.astype(jnp.float32)).astype(jnp.bfloat16)]
        gate_lhs = jnp.concatenate(gate_lhs, axis=0)
        for sel in range(2):
            g = _dot(gate_lhs, q_refs[sel][0, group * LANES:(group + 1) * LANES, :])
            for c, hh in enumerate(heads):
                gate_parts[sel, hh] = g[2 * c * n_blk:(2 * c + 1) * n_blk] + g[(2 * c + 1) * n_blk:(2 * c + 2) * n_blk]
    gates = jnp.concatenate([gate_parts[pair] for pair in pairs], axis=1)

    def store_block_choice():
        blk = lax.broadcasted_iota(jnp.int32, gates.shape, 0).astype(jnp.float32)
        lane = lax.broadcasted_iota(jnp.int32, gates.shape, 1)
        n_past = jnp.where(lane < nh * tile, q_tile[0], q_tile[1]).astype(jnp.float32)
        gate = jnp.where(blk < n_past, gates, -jnp.inf)
        chosen = blk == n_past
        for _ in range(MOBA_TOPK):
            top = jnp.max(gate, axis=0, keepdims=True)
            at_top = (gate == top) & (top > -jnp.inf)
            first = jnp.min(jnp.where(at_top, blk, float(n_blk)), axis=0, keepdims=True)
            pick = blk == first
            chosen = jnp.logical_or(chosen, pick)
            gate = jnp.where(pick, -jnp.inf, gate)
        choice = jnp.where(chosen, 0.0, MASKED).astype(jnp.bfloat16)
        for c, (sel, hh) in enumerate(pairs):
            first_bias_row = HEAD_DIM - own_half(hh).start
            qaug_ref[sel, hh, first_bias_row:first_bias_row + n_blk] = choice[:, c * tile:(c + 1) * tile]

    def key_block(j, hh):
        return kaug_ref[hh, pl.ds(pl.multiple_of(j * tile, tile), tile), :]

    def slot_of(s):
        is_b = s >= p
        return is_b.astype(jnp.int32), jnp.where(is_b, s - p, s)

    causal = key_pos <= qry_pos
    ones_rows = jnp.ones((BF16_ROWS, tile), jnp.bfloat16)

    def weighted_values(j, hh, prob):
        v_aug = jnp.concatenate([vt_ref[0, j, hh * HEAD_DIM:(hh + 1) * HEAD_DIM, :], ones_rows], axis=0)
        return _dot(v_aug, prob.astype(jnp.bfloat16))

    def finish(sel, o_ref):
        _store_heads(o_ref, [acc_ref[sel, hh, :HEAD_DIM] * (1.0 / acc_ref[sel, hh, HEAD_DIM:HEAD_DIM + 1])
                             for hh in range(nh)])

    items = [(True, sel) for sel in range(2)] + [(False, s) for s in range(n_blk - 1)]
    last_item_of_first_tile = items.index((False, n_blk // 2 - 2)) if n_blk >= 4 else 1
    in_flight = {}
    for step in range(len(items) + SOFTMAX_LAG):
        if step < len(items):
            own, idx = items[step]
            if not own and items[step - 1][0]:
                store_block_choice()
            sel, j = (idx, q_tile[idx]) if own else slot_of(idx)
            logits = [_dot(key_block(j, hh), qaug_ref[sel, hh]) for hh in range(nh)]
            if own:
                logits = [jnp.where(causal, lg, MASKED) for lg in logits]
            in_flight[step] = (own, sel, j, logits)
        done = step - SOFTMAX_LAG
        if done >= 0:
            own, sel, j, logits = in_flight.pop(done)
            for hh in range(nh):
                top = jnp.max(logits[hh], axis=0, keepdims=True)
                if own:
                    m_ref[sel, hh] = top
                    acc_ref[sel, hh] = weighted_values(j, hh, jnp.exp2(logits[hh] - top))
                else:
                    m_old = m_ref[sel, hh]
                    m_new = jnp.maximum(m_old, top)
                    m_ref[sel, hh] = m_new
                    acc_ref[sel, hh] = (jnp.exp2(m_old - m_new) * acc_ref[sel, hh]
                                        + weighted_values(j, hh, jnp.exp2(logits[hh] - m_new)))
            if done == last_item_of_first_tile:
                finish(0, olo_ref)
    finish(1, ohi_ref)


def _moba_attention(qt, k, vt, slopes):
    b, d, s = qt.shape
    nh = MOBA_HEADS
    gw = nh * HEAD_DIM
    n_blk = s // ATT_TILE
    assert gw % LANES == 0 and n_blk % 2 == 0 and PROJ_ROWS == 2 * ATT_TILE and HEAD_DIM >= n_blk + 6
    half = jax.ShapeDtypeStruct((b, s // 2, d), jnp.bfloat16)
    return pl.pallas_call(
        _moba_kernel,
        grid=(b, d // gw, n_blk // 2),
        in_specs=[
            pl.BlockSpec(memory_space=pltpu.SMEM),
            pl.BlockSpec((s, LANES), lambda i, j, t: (0, 0)),
            pl.BlockSpec((1, gw, ATT_TILE), lambda i, j, t: (i, j, t)),
            pl.BlockSpec((1, gw, ATT_TILE), lambda i, j, t: (i, j, n_blk - 1 - t)),
            pl.BlockSpec((1, s, gw), lambda i, j, t: (i, 0, j)),
            pl.BlockSpec((1, n_blk, gw, ATT_TILE), lambda i, j, t: (i, 0, j, 0)),
        ],
        out_specs=[
            pl.BlockSpec((1, ATT_TILE, gw), lambda i, j, t: (i, t, j)),
            pl.BlockSpec((1, ATT_TILE, gw), lambda i, j, t: (i, lax.bitwise_xor(t, 1), j)),
        ],
        out_shape=[half, half],
        scratch_shapes=[
            pltpu.VMEM((n_blk, gw), jnp.float32),
            pltpu.VMEM((nh, s, LANES), jnp.bfloat16),
            pltpu.VMEM((2, nh, LANES, ATT_TILE), jnp.bfloat16),
            pltpu.VMEM((2, nh, 1, ATT_TILE), jnp.float32),
            pltpu.VMEM((2, nh, HEAD_DIM + BF16_ROWS, ATT_TILE), jnp.float32),
        ],
        compiler_params=pltpu.CompilerParams(
            dimension_semantics=("arbitrary", "arbitrary", "arbitrary"),
            vmem_limit_bytes=VMEM_LIMIT),
        name="moba_attention",
    )(slopes, _moba_key_features(n_blk), qt, qt, k, vt)


def _softplus2(z2):
    return jnp.maximum(z2, jnp.log(1.0 + jnp.exp2(jnp.minimum(z2, EXP2_ARG_MAX))) * LOG2E)


def _sb_kernel(later_ref, qt_ref, k_ref, vt_ref, o_ref, suffix_ref, acc_ref, *, nh, nt):
    first_tile = pl.program_id(2) * nt
    tile = ATT_TILE
    heads_per_group = LANES // HEAD_DIM

    def lane_group(hh):
        first = hh // heads_per_group * LANES
        return slice(first, first + LANES)

    qm = {(ti, hh): _head_masked(qt_ref[0, lane_group(hh), ti * tile:(ti + 1) * tile], hh % heads_per_group)
          for ti in range(nt) for hh in range(nh)}
    key_pos = lax.broadcasted_iota(jnp.int32, (tile, tile), 0)
    qry_pos = lax.broadcasted_iota(jnp.int32, (tile, tile), 1)
    strict = key_pos < qry_pos

    def key_block(j, hh):
        return k_ref[0, pl.ds(pl.multiple_of(j * tile, tile), tile), lane_group(hh)]

    def later_sums(sp2):
        return _dot(later_ref[...], sp2.astype(jnp.bfloat16))

    def values(j, hh):
        return vt_ref[0, j, hh * HEAD_DIM:(hh + 1) * HEAD_DIM, :]

    streams = [(ti, hh) for ti in range(nt) for hh in range(nh)]
    units = ([(ti, hh, first_tile + ti, True) for ti, hh in streams]
             + [(ti, hh, jnp.maximum(first_tile + ti - 1, 0), False) for ti, hh in streams])
    half = tile // 2

    def live_parts(x):
        return x[:half], x[half:, half:]

    def full_tile(parts):
        first, last = parts
        return jnp.concatenate([first, jnp.concatenate([jnp.zeros_like(last), last], axis=1)], axis=0)

    strict_parts = live_parts(strict)
    z2s, sp2s, inners, weights, totals, outs = {}, {}, {}, {}, {}, {}
    lag_softplus, lag_sums, lag_weights, lag_values = SB_STAGE_LAGS
    for step in range(len(units) + lag_values):
        if step < len(units):
            ti, hh, j, own = units[step]
            z2 = _dot(key_block(j, hh), qm[ti, hh])
            if own:
                z2s[step] = [jnp.where(m, part, MASKED) for m, part in zip(strict_parts, live_parts(z2))]
            else:
                z2s[step] = [z2]
        u = step - lag_softplus
        if 0 <= u < len(units):
            sp2s[u] = [_softplus2(part) for part in z2s[u]]
        u = step - lag_sums
        if 0 <= u < len(units):
            sp2_bf = [part.astype(jnp.bfloat16) for part in sp2s[u]]
            inners[u] = later_sums(full_tile(sp2_bf) if units[u][3] else sp2_bf[0])
        u = step - lag_weights
        if 0 <= u < len(units):
            ti, _, _, own = units[u]
            totals[u] = inners[u][0:1] + sp2s[u][0][0:1]
            inner_parts = live_parts(inners[u]) if own else [inners[u]]
            expo = [z2 - sp2 - inner for z2, sp2, inner in zip(z2s[u], sp2s[u], inner_parts)]
            if own:
                weights[u] = full_tile([jnp.exp2(part).astype(jnp.bfloat16) for part in expo])
            else:
                own_total = totals[u - len(streams)]
                suffix = jnp.where(first_tile + ti > 0, own_total, -MASKED)
                weights[u] = jnp.exp2(expo[0] - suffix).astype(jnp.bfloat16)
        u = step - lag_values
        if 0 <= u < len(units):
            _, hh, j, _ = units[u]
            outs[u] = _dot(values(j, hh), weights[u])
    for c, (ti, hh) in enumerate(streams):
        acc_ref[ti, hh] = outs[c] + outs[len(streams) + c]
        suffix_ref[ti, hh] = totals[c] + totals[len(streams) + c]

    for ti in range(nt):
        def smallest_suffix(ti=ti):
            return functools.reduce(jnp.minimum, [jnp.min(suffix_ref[ti, hh]) for hh in range(nh)])

        def more_to_do(carry):
            j, smallest = carry
            return jnp.logical_and(j >= 0, smallest <= SB_SUFFIX_CUTOFF)

        def farther_block(carry, ti=ti, smallest_suffix=smallest_suffix):
            j, _ = carry
            for hh in range(nh):
                z2 = _dot(key_block(j, hh), qm[ti, hh])
                sp2 = _softplus2(z2)
                inner = later_sums(sp2)
                w = jnp.exp2(z2 - sp2 - inner - suffix_ref[ti, hh])
                acc_ref[ti, hh] += _dot(values(j, hh), w.astype(jnp.bfloat16))
                suffix_ref[ti, hh] += inner[0:1] + sp2[0:1]
            return j - 1, smallest_suffix()

        lax.while_loop(more_to_do, farther_block, (first_tile + ti - 2, smallest_suffix()))
        _store_heads(o_ref.at[:, ti * tile:(ti + 1) * tile], [acc_ref[ti, hh] for hh in range(nh)])


def _sb_attention(qt, k, vt):
    b, d, s = qt.shape
    nh = SB_HEADS
    nt = SB_TILES
    gw = nh * HEAD_DIM
    n_blk = s // ATT_TILE
    assert n_blk % nt == 0
    later = jnp.asarray(np.triu(np.ones((ATT_TILE, ATT_TILE), np.float32), 1), jnp.bfloat16)
    return pl.pallas_call(
        functools.partial(_sb_kernel, nh=nh, nt=nt),
        grid=(b, d // gw, n_blk // nt),
        in_specs=[
            pl.BlockSpec((ATT_TILE, ATT_TILE), lambda i, j, t: (0, 0)),
            pl.BlockSpec((1, gw, nt * ATT_TILE), lambda i, j, t: (i, j, t)),
            pl.BlockSpec((1, s, gw), lambda i, j, t: (i, 0, j)),
            pl.BlockSpec((1, n_blk, gw, ATT_TILE), lambda i, j, t: (i, 0, j, 0)),
        ],
        out_specs=pl.BlockSpec((1, nt * ATT_TILE, gw), lambda i, j, t: (i, t, j)),
        out_shape=jax.ShapeDtypeStruct((b, s, d), jnp.bfloat16),
        scratch_shapes=[
            pltpu.VMEM((nt, nh, 1, ATT_TILE), jnp.float32),
            pltpu.VMEM((nt, nh, HEAD_DIM, ATT_TILE), jnp.float32),
        ],
        compiler_params=pltpu.CompilerParams(
            dimension_semantics=("arbitrary", "arbitrary", "arbitrary"),
            vmem_limit_bytes=VMEM_LIMIT),
        name="stick_breaking_attention",
    )(later, qt, k, vt)


def kernel(x, norm_g, w_in, w_out, final_g):
    b, s, d = x.shape
    depth = norm_g.shape[0]
    assert depth >= 1 and d == N_HEADS * HEAD_DIM and s % PROJ_ROWS == 0 and PROJ_ROWS % ATT_TILE == 0
    assert s % WIDE_ROWS == 0 and WIDE_ROWS % ATT_TILE == 0
    slopes = jnp.asarray(2.0 ** (-8.0 * np.arange(1, N_HEADS + 1) / N_HEADS), jnp.float32)

    w_q_t = jnp.swapaxes(w_in[:, :, 0:d], 1, 2).astype(jnp.bfloat16)
    w_v_t = jnp.swapaxes(w_in[:, :, 2 * d:3 * d], 1, 2).astype(jnp.bfloat16)
    w_out_bf = w_out.astype(jnp.bfloat16)

    def proj_weights(layer):
        return w_in, w_q_t, w_v_t, layer, layer + 1 < depth

    h = x
    outs = _layer_boundary(h, norm_g[0], proj=proj_weights(0), rows=WIDE_ROWS)
    for i in range(depth):
        last = i + 1 == depth
        k, qt, vt = outs[0], outs[-2], outs[-1]
        gate = (norm_g[i], w_in) if last else outs[1]
        o_parts = _moba_attention(qt, k, vt, slopes) if i % 2 == 0 else [_sb_attention(qt, k, vt)]
        attn = (o_parts, gate, w_out_bf, i)
        if last:
            rows = WIDE_ROWS if len(o_parts) == 1 else PROJ_ROWS
            (h,) = _layer_boundary(h, final_g, attn=attn, rows=rows)
        else:
            h, *outs = _layer_boundary(h, norm_g[i + 1], attn=attn, proj=proj_weights(i + 1))
    return h
```

```python
import functools
import math

import numpy as np
import jax
import jax.numpy as jnp
from jax import lax
from jax.experimental import pallas as pl
from jax.experimental.pallas import tpu as pltpu

N_HEADS = 16
HEAD_DIM = 64
MOBA_BLOCK = 256
MOBA_TOPK = 3
NORM_EPS = 1e-6

LANES = 128
BF16_ROWS = 16
ATT_TILE = MOBA_BLOCK
PROJ_ROWS = 512
WIDE_ROWS = 1024
MOBA_HEADS = 4
SB_HEADS = 4
SB_TILES = 4
SOFTMAX_LAG = 2
SB_STAGE_LAGS = (1, 2, 3, 4)
LOG2E = math.log2(math.e)
MASKED = -1e30
EXP2_ARG_MAX = 126.0
SB_SUFFIX_CUTOFF = 128.0 * LOG2E
VMEM_LIMIT = 56 * 1024 * 1024


def _dot(a, b):
    return jnp.dot(a, b, preferred_element_type=jnp.float32)


def _dot_nt(a, b):
    return lax.dot_general(a, b, (((1,), (1,)), ((), ())), preferred_element_type=jnp.float32)


def _bf16_round(x):
    return x.astype(jnp.bfloat16).astype(jnp.float32)


def _rms_norm(h, g_ref):
    return h * lax.rsqrt(jnp.mean(h * h, axis=-1, keepdims=True) + NORM_EPS) * g_ref[...]


def _boundary_kernel(*refs, n_o, half_steps, gate_in, has_proj, emit_z, d, scale):
    refs = list(refs)
    if n_o:
        o_refs = [refs.pop(0) for _ in range(n_o)]
        if gate_in:
            z_ref = refs.pop(0)
        else:
            gate_gain_ref, gate_w_ref = refs.pop(0), refs.pop(0)
        h_ref, wout_ref = refs.pop(0), refs.pop(0)
    else:
        h_ref = refs.pop(0)
    g_ref = refs.pop(0)
    if has_proj:
        wk_ref = refs.pop(0)
        wz_ref = refs.pop(0) if emit_z else None
        wqt_ref, wvt_ref = refs.pop(0), refs.pop(0)
    if n_o:
        hout_ref = refs.pop(0)

    h = h_ref[0]
    if n_o:
        if n_o == 1:
            o = o_refs[0][0]
        else:
            o = jnp.where(pl.program_id(1) < half_steps, o_refs[0][0], o_refs[1][0])
        if gate_in:
            z = z_ref[0].astype(jnp.float32)
        else:
            z = _dot(_rms_norm(h, gate_gain_ref).astype(jnp.bfloat16), gate_w_ref[0].astype(jnp.bfloat16))
        gated = o.astype(jnp.float32) * (z / (1.0 + jnp.exp(-z)))
        h = h + _dot(gated.astype(jnp.bfloat16), wout_ref[0])
    y = _rms_norm(h, g_ref)
    if n_o:
        hout_ref[0] = h if has_proj else y
    if has_proj:
        k_ref = refs.pop(0)
        znext_ref = refs.pop(0) if emit_z else None
        qt_ref, vt_ref = refs
        xn = y.astype(jnp.bfloat16)
        k_ref[0] = _dot(xn, wk_ref[0].astype(jnp.bfloat16)).astype(k_ref.dtype)
        if emit_z:
            znext_ref[0] = _dot(xn, wz_ref[0].astype(jnp.bfloat16)).astype(znext_ref.dtype)
        qt_ref[0] = (_dot_nt(wqt_ref[0], xn) * scale).astype(qt_ref.dtype)
        v_t = _dot_nt(wvt_ref[0], xn)
        for c in range(vt_ref.shape[1]):
            vt_ref[0, c] = v_t[:, c * ATT_TILE:(c + 1) * ATT_TILE].astype(vt_ref.dtype)


def _layer_boundary(h, gain, attn=None, proj=None, rows=PROJ_ROWS):
    b, s, d = h.shape
    n_steps = s // rows
    half = n_steps // 2
    n_blk = s // ATT_TILE
    row_spec = pl.BlockSpec((1, rows, d), lambda i, t: (i, t, 0))

    def const_spec(shape):
        return pl.BlockSpec(shape, lambda i, t: (0,) * len(shape), pipeline_mode=pl.Buffered(1))

    def weight_spec(layer, col_block):
        return pl.BlockSpec((1, d, d), lambda i, t: (layer, 0, col_block), pipeline_mode=pl.Buffered(1))

    in_specs, args, out_specs, out_shape = [], [], [], []
    n_o = 0
    gate_in = True
    if attn is not None:
        o_parts, gate, w_out, out_layer = attn
        n_o = len(o_parts)
        if n_o == 1:
            in_specs.append(row_spec)
        else:
            assert rows == PROJ_ROWS
            in_specs += [
                pl.BlockSpec((1, rows, d), lambda i, t: (i, jnp.minimum(t, half - 1), 0)),
                pl.BlockSpec((1, rows, d), lambda i, t: (i, jnp.clip(n_steps - 1 - t, 0, half - 1), 0)),
            ]
        args += list(o_parts)
        gate_in = not isinstance(gate, tuple)
        if gate_in:
            in_specs.append(row_spec)
            args.append(gate)
        else:
            gate_gain, w_in_all = gate
            in_specs += [const_spec((1, d)), weight_spec(out_layer, 3)]
            args += [gate_gain.reshape(1, d), w_in_all]
        in_specs += [row_spec, weight_spec(out_layer, 0)]
        args += [h, w_out]
        out_specs.append(row_spec)
        out_shape.append(jax.ShapeDtypeStruct((b, s, d), jnp.float32))
    else:
        in_specs.append(row_spec)
        args.append(h)
    in_specs.append(const_spec((1, d)))
    args.append(gain.reshape(1, d))
    emit_z = False
    if proj is not None:
        w_in, w_q_t, w_v_t, layer, emit_z = proj
        token_major = (pl.BlockSpec((1, rows, d), lambda i, t: (i, t, 0)),
                       jax.ShapeDtypeStruct((b, s, d), jnp.bfloat16))
        in_specs += [weight_spec(layer, 1)] + ([weight_spec(layer, 3)] if emit_z else [])
        in_specs += [weight_spec(layer, 0), weight_spec(layer, 0)]
        args += [w_in] + ([w_in] if emit_z else []) + [w_q_t, w_v_t]
        outputs = [token_major] + ([token_major] if emit_z else []) + [
            (pl.BlockSpec((1, d, rows), lambda i, t: (i, 0, t)),
             jax.ShapeDtypeStruct((b, d, s), jnp.bfloat16)),
            (pl.BlockSpec((1, rows // ATT_TILE, d, ATT_TILE), lambda i, t: (i, t, 0, 0)),
             jax.ShapeDtypeStruct((b, n_blk, d, ATT_TILE), jnp.bfloat16)),
        ]
        out_specs += [spec for spec, _ in outputs]
        out_shape += [shape for _, shape in outputs]
    kern = functools.partial(_boundary_kernel, n_o=n_o, half_steps=half, gate_in=gate_in,
                             has_proj=proj is not None, emit_z=emit_z, d=d, scale=HEAD_DIM ** -0.5 * LOG2E)
    name = ("gate_outproj_" if attn is not None else "") + ("rmsnorm_qkvz_proj" if proj is not None else "final_rmsnorm")
    return pl.pallas_call(
        kern,
        grid=(b, n_steps),
        in_specs=in_specs,
        out_specs=out_specs,
        out_shape=out_shape,
        compiler_params=pltpu.CompilerParams(
            dimension_semantics=("arbitrary", "arbitrary"), vmem_limit_bytes=VMEM_LIMIT),
        name=name,
    )(*args)


def _head_masked(qt, hh):
    row = lax.broadcasted_iota(jnp.int32, qt.shape, 0)
    return jnp.where((row >= hh * HEAD_DIM) & (row < (hh + 1) * HEAD_DIM), qt, jnp.zeros_like(qt))


def _store_heads(o_ref, outs_t):
    stacked = jnp.concatenate(outs_t, axis=0)
    o_ref[0] = stacked.T.astype(o_ref.dtype)


def _moba_key_features(n_blk):
    pos = np.arange(n_blk * ATT_TILE)
    feat = np.zeros((n_blk * ATT_TILE, HEAD_DIM), np.float32)
    feat[pos, pos // ATT_TILE] = 1.0
    feat[:, n_blk:n_blk + 3] = (pos // ATT_TILE)[:, None]
    feat[:, n_blk + 3:n_blk + 6] = (pos % ATT_TILE)[:, None]
    return jnp.asarray(np.concatenate([feat, feat], axis=1), jnp.bfloat16)


def _moba_kernel(slopes_ref, kfeat_ref, qa_ref, qb_ref, k_ref, vt_ref, olo_ref, ohi_ref,
                 kmean_ref, kaug_ref, qaug_ref, m_ref, acc_ref):
    nh = MOBA_HEADS
    hg = pl.program_id(1)
    p = pl.program_id(2)
    n_blk = kmean_ref.shape[0]
    tile = ATT_TILE
    q_tile = (p, n_blk - 1 - p)
    q_refs = (qa_ref, qb_ref)
    key_pos = lax.broadcasted_iota(jnp.int32, (tile, tile), 0)
    qry_pos = lax.broadcasted_iota(jnp.int32, (tile, tile), 1)

    heads_per_group = LANES // HEAD_DIM

    def own_half(hh):
        first = hh % heads_per_group * HEAD_DIM
        return slice(first, first + HEAD_DIM)

    def head_cols(hh):
        return slice(hh * HEAD_DIM, (hh + 1) * HEAD_DIM)

    @pl.when(p == 0)
    def _():
        k_all = k_ref[0]
        kmean_ref[...] = jnp.mean(k_all.astype(jnp.float32).reshape(n_blk, tile, nh * HEAD_DIM), axis=1)
        for hh in range(nh):
            kaug_ref[hh] = kfeat_ref[...]
            kaug_ref[hh, :, own_half(hh)] = k_all[:, head_cols(hh)]

    crow = lax.broadcasted_iota(jnp.int32, (HEAD_DIM - n_blk, tile), 0)
    no_choice = jnp.zeros((n_blk, tile), jnp.float32)
    for hh in range(nh):
        slope = slopes_ref[hg * nh + hh] * LOG2E
        base = jnp.where(crow < 3, slope * tile, jnp.where(crow < 6, slope, 0.0))
        part0 = _bf16_round(base)
        part1 = _bf16_round(base - part0)
        part2 = base - part0 - part1
        slope_rows = jnp.where((crow == 0) | (crow == 3), part0,
                               jnp.where((crow == 1) | (crow == 4), part1, part2))
        bias_rows = jnp.concatenate([no_choice, slope_rows], axis=0).astype(jnp.bfloat16)
        for sel in range(2):
            q_rows = q_refs[sel][0, head_cols(hh), :]
            first_half = own_half(hh).start == 0
            qaug_ref[sel, hh] = jnp.concatenate([q_rows, bias_rows] if first_half else [bias_rows, q_rows], axis=0)

    gate_parts = {}
    group_lane = lax.broadcasted_iota(jnp.int32, (n_blk, LANES), 1)
    for group in range(nh // heads_per_group):
        heads = range(group * heads_per_group, (group + 1) * heads_per_group)
        kmean = kmean_ref[:, group * LANES:(group + 1) * LANES]
        gate_lhs = []
        for hh in heads:
            in_own_half = (group_lane >= own_half(hh).start) & (group_lane < own_half(hh).stop)
            kmean_h = jnp.where(in_own_half, kmean, 0.0)
            kmean_hi = kmean_h.astype(jnp.bfloat16)
            gate_lhs += [kmean_hi, (kmean_h - kmean_hi.astype(jnp.float32)).astype(jnp.bfloat16)]
        gate_lhs = jnp.concatenate(gate_lhs, axis=0)
        for sel in range(2):
            g = _dot(gate_lhs, q_refs[sel][0, group * LANES:(group + 1) * LANES, :])
            for c, hh in enumerate(heads):
                gate_parts[sel, hh] = g[2 * c * n_blk:(2 * c + 1) * n_blk] + g[(2 * c + 1) * n_blk:(2 * c + 2) * n_blk]

    def store_block_choice(heads):
        pairs = [(sel, hh) for sel in range(2) for hh in heads]
        gates = jnp.concatenate([gate_parts[pair] for pair in pairs], axis=1)
        blk = lax.broadcasted_iota(jnp.int32, gates.shape, 0).astype(jnp.float32)
        lane = lax.broadcasted_iota(jnp.int32, gates.shape, 1)
        n_past = jnp.where(lane < len(heads) * tile, q_tile[0], q_tile[1]).astype(jnp.float32)
        gate = jnp.where(blk < n_past, gates, -jnp.inf)
        chosen = blk == n_past
        for _ in range(MOBA_TOPK):
            top = jnp.max(gate, axis=0, keepdims=True)
            at_top = (gate == top) & (top > -jnp.inf)
            first = jnp.min(jnp.where(at_top, blk, float(n_blk)), axis=0, keepdims=True)
            pick = blk == first
            chosen = jnp.logical_or(chosen, pick)
            gate = jnp.where(pick, -jnp.inf, gate)
        choice = jnp.where(chosen, 0.0, MASKED).astype(jnp.bfloat16)
        for c, (sel, hh) in enumerate(pairs):
            first_bias_row = HEAD_DIM - own_half(hh).start
            qaug_ref[sel, hh, first_bias_row:first_bias_row + n_blk] = choice[:, c * tile:(c + 1) * tile]

    def key_block(j, hh):
        return kaug_ref[hh, pl.ds(pl.multiple_of(j * tile, tile), tile), :]

    def slot_of(s):
        is_b = s >= p
        return is_b.astype(jnp.int32), jnp.where(is_b, s - p, s)

    causal = key_pos <= qry_pos
    ones_rows = jnp.ones((BF16_ROWS, tile), jnp.bfloat16)

    def weighted_values(j, hh, prob):
        v_aug = jnp.concatenate([vt_ref[0, j, hh * HEAD_DIM:(hh + 1) * HEAD_DIM, :], ones_rows], axis=0)
        return _dot(v_aug, prob.astype(jnp.bfloat16))

    def finish(sel, heads):
        o_ref = (olo_ref, ohi_ref)[sel]
        cols = slice(heads[0] * HEAD_DIM, (heads[-1] + 1) * HEAD_DIM)
        _store_heads(o_ref.at[:, :, cols],
                     [acc_ref[sel, hh, :HEAD_DIM] * (1.0 / acc_ref[sel, hh, HEAD_DIM:HEAD_DIM + 1]) for hh in heads])

    group_items = [(True, sel) for sel in range(2)] + [(False, s) for s in range(n_blk - 1)]
    first_tile_done = (False, n_blk // 2 - 2) if n_blk >= 4 else (True, 1)
    items = [(group, own, idx) for group in range(nh // heads_per_group) for own, idx in group_items]
    in_flight = {}
    for step in range(len(items) + SOFTMAX_LAG):
        if step < len(items):
            group, own, idx = items[step]
            heads = list(range(group * heads_per_group, (group + 1) * heads_per_group))
            if (own, idx) == group_items[2]:
                store_block_choice(heads)
            sel, j = (idx, q_tile[idx]) if own else slot_of(idx)
            logits = [_dot(key_block(j, hh), qaug_ref[sel, hh]) for hh in heads]
            if own:
                logits = [jnp.where(causal, lg, MASKED) for lg in logits]
            in_flight[step] = (heads, own, sel, j, logits)
        done = step - SOFTMAX_LAG
        if done >= 0:
            heads, own, sel, j, logits = in_flight.pop(done)
            for hh, lg in zip(heads, logits):
                top = jnp.max(lg, axis=0, keepdims=True)
                if own:
                    m_ref[sel, hh] = top
                    acc_ref[sel, hh] = weighted_values(j, hh, jnp.exp2(lg - top))
                else:
                    m_old = m_ref[sel, hh]
                    m_new = jnp.maximum(m_old, top)
                    m_ref[sel, hh] = m_new
                    acc_ref[sel, hh] = (jnp.exp2(m_old - m_new) * acc_ref[sel, hh]
                                        + weighted_values(j, hh, jnp.exp2(lg - m_new)))
            if items[done][1:] == first_tile_done:
                finish(0, heads)
            if items[done][1:] == group_items[-1]:
                finish(1, heads)


def _moba_attention(qt, k, vt, slopes):
    b, d, s = qt.shape
    nh = MOBA_HEADS
    gw = nh * HEAD_DIM
    n_blk = s // ATT_TILE
    assert gw % LANES == 0 and n_blk % 2 == 0 and PROJ_ROWS == 2 * ATT_TILE and HEAD_DIM >= n_blk + 6
    half = jax.ShapeDtypeStruct((b, s // 2, d), jnp.bfloat16)
    return pl.pallas_call(
        _moba_kernel,
        grid=(b, d // gw, n_blk // 2),
        in_specs=[
            pl.BlockSpec(memory_space=pltpu.SMEM),
            pl.BlockSpec((s, LANES), lambda i, j, t: (0, 0)),
            pl.BlockSpec((1, gw, ATT_TILE), lambda i, j, t: (i, j, t)),
            pl.BlockSpec((1, gw, ATT_TILE), lambda i, j, t: (i, j, n_blk - 1 - t)),
            pl.BlockSpec((1, s, gw), lambda i, j, t: (i, 0, j)),
            pl.BlockSpec((1, n_blk, gw, ATT_TILE), lambda i, j, t: (i, 0, j, 0)),
        ],
        out_specs=[
            pl.BlockSpec((1, ATT_TILE, gw), lambda i, j, t: (i, t, j)),
            pl.BlockSpec((1, ATT_TILE, gw), lambda i, j, t: (i, lax.bitwise_xor(t, 1), j)),
        ],
        out_shape=[half, half],
        scratch_shapes=[
            pltpu.VMEM((n_blk, gw), jnp.float32),
            pltpu.VMEM((nh, s, LANES), jnp.bfloat16),
            pltpu.VMEM((2, nh, LANES, ATT_TILE), jnp.bfloat16),
            pltpu.VMEM((2, nh, 1, ATT_TILE), jnp.float32),
            pltpu.VMEM((2, nh, HEAD_DIM + BF16_ROWS, ATT_TILE), jnp.float32),
        ],
        compiler_params=pltpu.CompilerParams(
            dimension_semantics=("arbitrary", "arbitrary", "arbitrary"),
            vmem_limit_bytes=VMEM_LIMIT),
        name="moba_attention",
    )(slopes, _moba_key_features(n_blk), qt, qt, k, vt)


def _softplus2(z2):
    return jnp.maximum(z2, jnp.log(1.0 + jnp.exp2(jnp.minimum(z2, EXP2_ARG_MAX))) * LOG2E)


def _sb_kernel(later_ref, qt_ref, k_ref, vt_ref, o_ref, suffix_ref, acc_ref, *, nh, nt):
    first_tile = pl.program_id(2) * nt
    tile = ATT_TILE
    heads_per_group = LANES // HEAD_DIM

    def lane_group(hh):
        first = hh // heads_per_group * LANES
        return slice(first, first + LANES)

    qm = {(ti, hh): _head_masked(qt_ref[0, lane_group(hh), ti * tile:(ti + 1) * tile], hh % heads_per_group)
          for ti in range(nt) for hh in range(nh)}
    key_pos = lax.broadcasted_iota(jnp.int32, (tile, tile), 0)
    qry_pos = lax.broadcasted_iota(jnp.int32, (tile, tile), 1)
    strict = key_pos < qry_pos

    def key_block(j, hh):
        return k_ref[0, pl.ds(pl.multiple_of(j * tile, tile), tile), lane_group(hh)]

    def later_sums(sp2):
        return _dot(later_ref[...], sp2.astype(jnp.bfloat16))

    def values(j, hh):
        return vt_ref[0, j, hh * HEAD_DIM:(hh + 1) * HEAD_DIM, :]

    streams = [(ti, hh) for ti in range(nt) for hh in range(nh)]
    units = ([(ti, hh, first_tile + ti, True) for ti, hh in streams]
             + [(ti, hh, jnp.maximum(first_tile + ti - 1, 0), False) for ti, hh in streams])
    half = tile // 2

    def live_parts(x):
        return x[:half], x[half:, half:]

    def full_tile(parts):
        first, last = parts
        return jnp.concatenate([first, jnp.concatenate([jnp.zeros_like(last), last], axis=1)], axis=0)

    strict_parts = live_parts(strict)
    z2s, sp2s, inners, weights, totals, outs = {}, {}, {}, {}, {}, {}
    lag_softplus, lag_sums, lag_weights, lag_values = SB_STAGE_LAGS
    for step in range(len(units) + lag_values):
        if step < len(units):
            ti, hh, j, own = units[step]
            z2 = _dot(key_block(j, hh), qm[ti, hh])
            if own:
                z2s[step] = [jnp.where(m, part, MASKED) for m, part in zip(strict_parts, live_parts(z2))]
            else:
                z2s[step] = [z2]
        u = step - lag_softplus
        if 0 <= u < len(units):
            sp2s[u] = [_softplus2(part) for part in z2s[u]]
        u = step - lag_sums
        if 0 <= u < len(units):
            sp2_bf = [part.astype(jnp.bfloat16) for part in sp2s[u]]
            inners[u] = later_sums(full_tile(sp2_bf) if units[u][3] else sp2_bf[0])
        u = step - lag_weights
        if 0 <= u < len(units):
            ti, _, _, own = units[u]
            totals[u] = inners[u][0:1] + sp2s[u][0][0:1]
            inner_parts = live_parts(inners[u]) if own else [inners[u]]
            expo = [z2 - sp2 - inner for z2, sp2, inner in zip(z2s[u], sp2s[u], inner_parts)]
            if own:
                weights[u] = full_tile([jnp.exp2(part).astype(jnp.bfloat16) for part in expo])
            else:
                own_total = totals[u - len(streams)]
                suffix = jnp.where(first_tile + ti > 0, own_total, -MASKED)
                weights[u] = jnp.exp2(expo[0] - suffix).astype(jnp.bfloat16)
        u = step - lag_values
        if 0 <= u < len(units):
            _, hh, j, _ = units[u]
            outs[u] = _dot(values(j, hh), weights[u])
    for c, (ti, hh) in enumerate(streams):
        acc_ref[ti, hh] = outs[c] + outs[len(streams) + c]
        suffix_ref[ti, hh] = totals[c] + totals[len(streams) + c]

    for ti in range(nt):
        def smallest_suffix(ti=ti):
            return functools.reduce(jnp.minimum, [jnp.min(suffix_ref[ti, hh]) for hh in range(nh)])

        def more_to_do(carry):
            j, smallest = carry
            return jnp.logical_and(j >= 0, smallest <= SB_SUFFIX_CUTOFF)

        def farther_block(carry, ti=ti, smallest_suffix=smallest_suffix):
            j, _ = carry
            for hh in range(nh):
                z2 = _dot(key_block(j, hh), qm[ti, hh])
                sp2 = _softplus2(z2)
                inner = later_sums(sp2)
                w = jnp.exp2(z2 - sp2 - inner - suffix_ref[ti, hh])
                acc_ref[ti, hh] += _dot(values(j, hh), w.astype(jnp.bfloat16))
                suffix_ref[ti, hh] += inner[0:1] + sp2[0:1]
            return j - 1, smallest_suffix()

        lax.while_loop(more_to_do, farther_block, (first_tile + ti - 2, smallest_suffix()))
        _store_heads(o_ref.at[:, ti * tile:(ti + 1) * tile], [acc_ref[ti, hh] for hh in range(nh)])


def _sb_attention(qt, k, vt):
    b, d, s = qt.shape
    nh = SB_HEADS
    nt = SB_TILES
    gw = nh * HEAD_DIM
    n_blk = s // ATT_TILE
    assert n_blk % nt == 0
    later = jnp.asarray(np.triu(np.ones((ATT_TILE, ATT_TILE), np.float32), 1), jnp.bfloat16)
    return pl.pallas_call(
        functools.partial(_sb_kernel, nh=nh, nt=nt),
        grid=(b, d // gw, n_blk // nt),
        in_specs=[
            pl.BlockSpec((ATT_TILE, ATT_TILE), lambda i, j, t: (0, 0)),
            pl.BlockSpec((1, gw, nt * ATT_TILE), lambda i, j, t: (i, j, t)),
            pl.BlockSpec((1, s, gw), lambda i, j, t: (i, 0, j)),
            pl.BlockSpec((1, n_blk, gw, ATT_TILE), lambda i, j, t: (i, 0, j, 0)),
        ],
        out_specs=pl.BlockSpec((1, nt * ATT_TILE, gw), lambda i, j, t: (i, t, j)),
        out_shape=jax.ShapeDtypeStruct((b, s, d), jnp.bfloat16),
        scratch_shapes=[
            pltpu.VMEM((nt, nh, 1, ATT_TILE), jnp.float32),
            pltpu.VMEM((nt, nh, HEAD_DIM, ATT_TILE), jnp.float32),
        ],
        compiler_params=pltpu.CompilerParams(
            dimension_semantics=("arbitrary", "arbitrary", "arbitrary"),
            vmem_limit_bytes=VMEM_LIMIT),
        name="stick_breaking_attention",
    )(later, qt, k, vt)


def kernel(x, norm_g, w_in, w_out, final_g):
    b, s, d = x.shape
    depth = norm_g.shape[0]
    assert depth >= 1 and d == N_HEADS * HEAD_DIM and s % PROJ_ROWS == 0 and PROJ_ROWS % ATT_TILE == 0
    assert s % WIDE_ROWS == 0 and WIDE_ROWS % ATT_TILE == 0
    slopes = jnp.asarray(2.0 ** (-8.0 * np.arange(1, N_HEADS + 1) / N_HEADS), jnp.float32)

    w_q_t = jnp.swapaxes(w_in[:, :, 0:d], 1, 2).astype(jnp.bfloat16)
    w_v_t = jnp.swapaxes(w_in[:, :, 2 * d:3 * d], 1, 2).astype(jnp.bfloat16)
    w_out_bf = w_out.astype(jnp.bfloat16)

    def proj_weights(layer):
        return w_in, w_q_t, w_v_t, layer, layer + 1 < depth

    h = x
    outs = _layer_boundary(h, norm_g[0], proj=proj_weights(0), rows=WIDE_ROWS)
    for i in range(depth):
        last = i + 1 == depth
        k, qt, vt = outs[0], outs[-2], outs[-1]
        gate = (norm_g[i], w_in) if last else outs[1]
        o_parts = _moba_attention(qt, k, vt, slopes) if i % 2 == 0 else [_sb_attention(qt, k, vt)]
        attn = (o_parts, gate, w_out_bf, i)
        if last:
            rows = WIDE_ROWS if len(o_parts) == 1 else PROJ_ROWS
            (h,) = _layer_boundary(h, final_g, attn=attn, rows=rows)
        else:
            h, *outs = _layer_boundary(h, norm_g[i + 1], attn=attn, proj=proj_weights(i + 1))
    return h
```

```python
import functools
import math

import numpy as np
import jax
import jax.numpy as jnp
from jax import lax
from jax.experimental import pallas as pl
from jax.experimental.pallas import tpu as pltpu

N_HEADS = 16
HEAD_DIM = 64
MOBA_BLOCK = 256
MOBA_TOPK = 3
NORM_EPS = 1e-6

LANES = 128
BF16_ROWS = 16
ATT_TILE = MOBA_BLOCK
PROJ_ROWS = 512
WIDE_ROWS = 1024
MOBA_HEADS = 8
SB_HEADS = 4
SB_TILES = 4
SOFTMAX_LAG = 2
SB_STAGE_LAGS = (1, 2, 3, 4)
LOG2E = math.log2(math.e)
MASKED = -1e30
EXP2_ARG_MAX = 126.0
SB_SUFFIX_CUTOFF = 128.0 * LOG2E
VMEM_LIMIT = 56 * 1024 * 1024


def _dot(a, b):
    return jnp.dot(a, b, preferred_element_type=jnp.float32)


def _dot_nt(a, b):
    return lax.dot_general(a, b, (((1,), (1,)), ((), ())), preferred_element_type=jnp.float32)


def _bf16_round(x):
    return x.astype(jnp.bfloat16).astype(jnp.float32)


def _rms_norm(h, g_ref):
    return h * lax.rsqrt(jnp.mean(h * h, axis=-1, keepdims=True) + NORM_EPS) * g_ref[...]


def _boundary_kernel(*refs, n_o, half_steps, gate_in, has_proj, emit_z, d, scale):
    refs = list(refs)
    if n_o:
        o_refs = [refs.pop(0) for _ in range(n_o)]
        if gate_in:
            z_ref = refs.pop(0)
        else:
            gate_gain_ref, gate_w_ref = refs.pop(0), refs.pop(0)
        h_ref, wout_ref = refs.pop(0), refs.pop(0)
    else:
        h_ref = refs.pop(0)
    g_ref = refs.pop(0)
    if has_proj:
        wk_ref = refs.pop(0)
        wz_ref = refs.pop(0) if emit_z else None
        wqt_ref, wvt_ref = refs.pop(0), refs.pop(0)
    if n_o:
        hout_ref = refs.pop(0)

    h = h_ref[0]
    if n_o:
        if n_o == 1:
            o = o_refs[0][0]
        else:
            o = jnp.where(pl.program_id(1) < half_steps, o_refs[0][0], o_refs[1][0])
        if gate_in:
            z = z_ref[0].astype(jnp.float32)
        else:
            z = _dot(_rms_norm(h, gate_gain_ref).astype(jnp.bfloat16), gate_w_ref[0].astype(jnp.bfloat16))
        gated = o.astype(jnp.float32) * (z / (1.0 + jnp.exp(-z)))
        h = h + _dot(gated.astype(jnp.bfloat16), wout_ref[0])
    y = _rms_norm(h, g_ref)
    if n_o:
        hout_ref[0] = h if has_proj else y
    if has_proj:
        k_ref = refs.pop(0)
        znext_ref = refs.pop(0) if emit_z else None
        qt_ref, vt_ref = refs
        xn = y.astype(jnp.bfloat16)
        k_ref[0] = _dot(xn, wk_ref[0].astype(jnp.bfloat16)).astype(k_ref.dtype)
        if emit_z:
            znext_ref[0] = _dot(xn, wz_ref[0].astype(jnp.bfloat16)).astype(znext_ref.dtype)
        qt_ref[0] = (_dot_nt(wqt_ref[0], xn) * scale).astype(qt_ref.dtype)
        v_t = _dot_nt(wvt_ref[0], xn)
        for c in range(vt_ref.shape[1]):
            vt_ref[0, c] = v_t[:, c * ATT_TILE:(c + 1) * ATT_TILE].astype(vt_ref.dtype)


def _layer_boundary(h, gain, attn=None, proj=None, rows=PROJ_ROWS):
    b, s, d = h.shape
    n_steps = s // rows
    half = n_steps // 2
    n_blk = s // ATT_TILE
    row_spec = pl.BlockSpec((1, rows, d), lambda i, t: (i, t, 0))

    def const_spec(shape):
        return pl.BlockSpec(shape, lambda i, t: (0,) * len(shape), pipeline_mode=pl.Buffered(1))

    def weight_spec(layer, col_block):
        return pl.BlockSpec((1, d, d), lambda i, t: (layer, 0, col_block), pipeline_mode=pl.Buffered(1))

    in_specs, args, out_specs, out_shape = [], [], [], []
    n_o = 0
    gate_in = True
    if attn is not None:
        o_parts, gate, w_out, out_layer = attn
        n_o = len(o_parts)
        if n_o == 1:
            in_specs.append(row_spec)
        else:
            assert rows == PROJ_ROWS
            in_specs += [
                pl.BlockSpec((1, rows, d), lambda i, t: (i, jnp.minimum(t, half - 1), 0)),
                pl.BlockSpec((1, rows, d), lambda i, t: (i, jnp.clip(n_steps - 1 - t, 0, half - 1), 0)),
            ]
        args += list(o_parts)
        gate_in = not isinstance(gate, tuple)
        if gate_in:
            in_specs.append(row_spec)
            args.append(gate)
        else:
            gate_gain, w_in_all = gate
            in_specs += [const_spec((1, d)), weight_spec(out_layer, 3)]
            args += [gate_gain.reshape(1, d), w_in_all]
        in_specs += [row_spec, weight_spec(out_layer, 0)]
        args += [h, w_out]
        out_specs.append(row_spec)
        out_shape.append(jax.ShapeDtypeStruct((b, s, d), jnp.float32))
    else:
        in_specs.append(row_spec)
        args.append(h)
    in_specs.append(const_spec((1, d)))
    args.append(gain.reshape(1, d))
    emit_z = False
    if proj is not None:
        w_in, w_q_t, w_v_t, layer, emit_z = proj
        token_major = (pl.BlockSpec((1, rows, d), lambda i, t: (i, t, 0)),
                       jax.ShapeDtypeStruct((b, s, d), jnp.bfloat16))
        in_specs += [weight_spec(layer, 1)] + ([weight_spec(layer, 3)] if emit_z else [])
        in_specs += [weight_spec(layer, 0), weight_spec(layer, 0)]
        args += [w_in] + ([w_in] if emit_z else []) + [w_q_t, w_v_t]
        outputs = [token_major] + ([token_major] if emit_z else []) + [
            (pl.BlockSpec((1, d, rows), lambda i, t: (i, 0, t)),
             jax.ShapeDtypeStruct((b, d, s), jnp.bfloat16)),
            (pl.BlockSpec((1, rows // ATT_TILE, d, ATT_TILE), lambda i, t: (i, t, 0, 0)),
             jax.ShapeDtypeStruct((b, n_blk, d, ATT_TILE), jnp.bfloat16)),
        ]
        out_specs += [spec for spec, _ in outputs]
        out_shape += [shape for _, shape in outputs]
    kern = functools.partial(_boundary_kernel, n_o=n_o, half_steps=half, gate_in=gate_in,
                             has_proj=proj is not None, emit_z=emit_z, d=d, scale=HEAD_DIM ** -0.5 * LOG2E)
    name = ("gate_outproj_" if attn is not None else "") + ("rmsnorm_qkvz_proj" if proj is not None else "final_rmsnorm")
    return pl.pallas_call(
        kern,
        grid=(b, n_steps),
        in_specs=in_specs,
        out_specs=out_specs,
        out_shape=out_shape,
        compiler_params=pltpu.CompilerParams(
            dimension_semantics=("arbitrary", "arbitrary"), vmem_limit_bytes=VMEM_LIMIT),
        name=name,
    )(*args)


def _head_masked(qt, hh):
    row = lax.broadcasted_iota(jnp.int32, qt.shape, 0)
    return jnp.where((row >= hh * HEAD_DIM) & (row < (hh + 1) * HEAD_DIM), qt, jnp.zeros_like(qt))


def _store_heads(o_ref, outs_t):
    stacked = jnp.concatenate(outs_t, axis=0)
    o_ref[0] = stacked.T.astype(o_ref.dtype)


def _moba_key_features(n_blk):
    pos = np.arange(n_blk * ATT_TILE)
    feat = np.zeros((n_blk * ATT_TILE, HEAD_DIM), np.float32)
    feat[pos, pos // ATT_TILE] = 1.0
    feat[:, n_blk:n_blk + 3] = (pos // ATT_TILE)[:, None]
    feat[:, n_blk + 3:n_blk + 6] = (pos % ATT_TILE)[:, None]
    return jnp.asarray(np.concatenate([feat, feat], axis=1), jnp.bfloat16)


def _moba_kernel(slopes_ref, kfeat_ref, qa_ref, qb_ref, k_ref, vt_ref, olo_ref, ohi_ref,
                 kmean_ref, kaug_ref, qaug_ref, m_ref, acc_ref):
    nh = MOBA_HEADS
    hg = pl.program_id(1)
    p = pl.program_id(2)
    n_blk = kmean_ref.shape[0]
    tile = ATT_TILE
    q_tile = (p, n_blk - 1 - p)
    q_refs = (qa_ref, qb_ref)
    key_pos = lax.broadcasted_iota(jnp.int32, (tile, tile), 0)
    qry_pos = lax.broadcasted_iota(jnp.int32, (tile, tile), 1)

    heads_per_group = LANES // HEAD_DIM

    def own_half(hh):
        first = hh % heads_per_group * HEAD_DIM
        return slice(first, first + HEAD_DIM)

    def head_cols(hh):
        return slice(hh * HEAD_DIM, (hh + 1) * HEAD_DIM)

    @pl.when(p == 0)
    def _():
        k_all = k_ref[0]
        kmean_ref[...] = jnp.mean(k_all.astype(jnp.float32).reshape(n_blk, tile, nh * HEAD_DIM), axis=1)
        for hh in range(nh):
            kaug_ref[hh] = kfeat_ref[...]
            kaug_ref[hh, :, own_half(hh)] = k_all[:, head_cols(hh)]

    crow = lax.broadcasted_iota(jnp.int32, (HEAD_DIM - n_blk, tile), 0)
    no_choice = jnp.zeros((n_blk, tile), jnp.float32)
    for hh in range(nh):
        slope = slopes_ref[hg * nh + hh] * LOG2E
        base = jnp.where(crow < 3, slope * tile, jnp.where(crow < 6, slope, 0.0))
        part0 = _bf16_round(base)
        part1 = _bf16_round(base - part0)
        part2 = base - part0 - part1
        slope_rows = jnp.where((crow == 0) | (crow == 3), part0,
                               jnp.where((crow == 1) | (crow == 4), part1, part2))
        bias_rows = jnp.concatenate([no_choice, slope_rows], axis=0).astype(jnp.bfloat16)
        for sel in range(2):
            q_rows = q_refs[sel][0, head_cols(hh), :]
            first_half = own_half(hh).start == 0
            qaug_ref[sel, hh] = jnp.concatenate([q_rows, bias_rows] if first_half else [bias_rows, q_rows], axis=0)

    gate_parts = {}
    group_lane = lax.broadcasted_iota(jnp.int32, (n_blk, LANES), 1)
    for group in range(nh // heads_per_group):
        heads = range(group * heads_per_group, (group + 1) * heads_per_group)
        kmean = kmean_ref[:, group * LANES:(group + 1) * LANES]
        gate_lhs = []
        for hh in heads:
            in_own_half = (group_lane >= own_half(hh).start) & (group_lane < own_half(hh).stop)
            kmean_h = jnp.where(in_own_half, kmean, 0.0)
            kmean_hi = kmean_h.astype(jnp.bfloat16)
            gate_lhs += [kmean_hi, (kmean_h - kmean_hi.astype(jnp.float32)).astype(jnp.bfloat16)]
        gate_lhs = jnp.concatenate(gate_lhs, axis=0)
        for sel in range(2):
            g = _dot(gate_lhs, q_refs[sel][0, group * LANES:(group + 1) * LANES, :])
            for c, hh in enumerate(heads):
                gate_parts[sel, hh] = g[2 * c * n_blk:(2 * c + 1) * n_blk] + g[(2 * c + 1) * n_blk:(2 * c + 2) * n_blk]

    def store_block_choice(heads):
        pairs = [(sel, hh) for sel in range(2) for hh in heads]
        gates = jnp.concatenate([gate_parts[pair] for pair in pairs], axis=1)
        blk = lax.broadcasted_iota(jnp.int32, gates.shape, 0).astype(jnp.float32)
        lane = lax.broadcasted_iota(jnp.int32, gates.shape, 1)
        n_past = jnp.where(lane < len(heads) * tile, q_tile[0], q_tile[1]).astype(jnp.float32)
        gate = jnp.where(blk < n_past, gates, -jnp.inf)
        chosen = blk == n_past
        for _ in range(MOBA_TOPK):
            top = jnp.max(gate, axis=0, keepdims=True)
            at_top = (gate == top) & (top > -jnp.inf)
            first = jnp.min(jnp.where(at_top, blk, float(n_blk)), axis=0, keepdims=True)
            pick = blk == first
            chosen = jnp.logical_or(chosen, pick)
            gate = jnp.where(pick, -jnp.inf, gate)
        choice = jnp.where(chosen, 0.0, MASKED).astype(jnp.bfloat16)
        for c, (sel, hh) in enumerate(pairs):
            first_bias_row = HEAD_DIM - own_half(hh).start
            qaug_ref[sel, hh, first_bias_row:first_bias_row + n_blk] = choice[:, c * tile:(c + 1) * tile]

    def key_block(j, hh):
        return kaug_ref[hh, pl.ds(pl.multiple_of(j * tile, tile), tile), :]

    def slot_of(s):
        is_b = s >= p
        return is_b.astype(jnp.int32), jnp.where(is_b, s - p, s)

    causal = key_pos <= qry_pos
    ones_rows = jnp.ones((BF16_ROWS, tile), jnp.bfloat16)

    def weighted_values(j, hh, prob):
        v_aug = jnp.concatenate([vt_ref[0, j, hh * HEAD_DIM:(hh + 1) * HEAD_DIM, :], ones_rows], axis=0)
        return _dot(v_aug, prob.astype(jnp.bfloat16))

    def finish(sel, heads):
        o_ref = (olo_ref, ohi_ref)[sel]
        cols = slice(heads[0] * HEAD_DIM, (heads[-1] + 1) * HEAD_DIM)
        _store_heads(o_ref.at[:, :, cols],
                     [acc_ref[sel, hh, :HEAD_DIM] * (1.0 / acc_ref[sel, hh, HEAD_DIM:HEAD_DIM + 1]) for hh in heads])

    group_items = [(True, sel) for sel in range(2)] + [(False, s) for s in range(n_blk - 1)]
    first_tile_done = (False, n_blk // 2 - 2) if n_blk >= 4 else (True, 1)
    items = [(group, own, idx) for group in range(nh // heads_per_group) for own, idx in group_items]
    in_flight = {}
    for step in range(len(items) + SOFTMAX_LAG):
        if step < len(items):
            group, own, idx = items[step]
            heads = list(range(group * heads_per_group, (group + 1) * heads_per_group))
            if (own, idx) == group_items[2]:
                store_block_choice(heads)
            sel, j = (idx, q_tile[idx]) if own else slot_of(idx)
            logits = [_dot(key_block(j, hh), qaug_ref[sel, hh]) for hh in heads]
            if own:
                logits = [jnp.where(causal, lg, MASKED) for lg in logits]
            in_flight[step] = (heads, own, sel, j, logits)
        done = step - SOFTMAX_LAG
        if done >= 0:
            heads, own, sel, j, logits = in_flight.pop(done)
            for hh, lg in zip(heads, logits):
                top = jnp.max(lg, axis=0, keepdims=True)
                if own:
                    m_ref[sel, hh] = top
                    acc_ref[sel, hh] = weighted_values(j, hh, jnp.exp2(lg - top))
                else:
                    m_old = m_ref[sel, hh]
                    m_new = jnp.maximum(m_old, top)
                    m_ref[sel, hh] = m_new
                    acc_ref[sel, hh] = (jnp.exp2(m_old - m_new) * acc_ref[sel, hh]
                                        + weighted_values(j, hh, jnp.exp2(lg - m_new)))
            if items[done][1:] == first_tile_done:
                finish(0, heads)
            if items[done][1:] == group_items[-1]:
                finish(1, heads)


def _moba_attention(qt, k, vt, slopes):
    b, d, s = qt.shape
    nh = MOBA_HEADS
    gw = nh * HEAD_DIM
    n_blk = s // ATT_TILE
    assert gw % LANES == 0 and n_blk % 2 == 0 and PROJ_ROWS == 2 * ATT_TILE and HEAD_DIM >= n_blk + 6
    half = jax.ShapeDtypeStruct((b, s // 2, d), jnp.bfloat16)
    return pl.pallas_call(
        _moba_kernel,
        grid=(b, d // gw, n_blk // 2),
        in_specs=[
            pl.BlockSpec(memory_space=pltpu.SMEM),
            pl.BlockSpec((s, LANES), lambda i, j, t: (0, 0)),
            pl.BlockSpec((1, gw, ATT_TILE), lambda i, j, t: (i, j, t)),
            pl.BlockSpec((1, gw, ATT_TILE), lambda i, j, t: (i, j, n_blk - 1 - t)),
            pl.BlockSpec((1, s, gw), lambda i, j, t: (i, 0, j)),
            pl.BlockSpec((1, n_blk, gw, ATT_TILE), lambda i, j, t: (i, 0, j, 0)),
        ],
        out_specs=[
            pl.BlockSpec((1, ATT_TILE, gw), lambda i, j, t: (i, t, j)),
            pl.BlockSpec((1, ATT_TILE, gw), lambda i, j, t: (i, lax.bitwise_xor(t, 1), j)),
        ],
        out_shape=[half, half],
        scratch_shapes=[
            pltpu.VMEM((n_blk, gw), jnp.float32),
            pltpu.VMEM((nh, s, LANES), jnp.bfloat16),
            pltpu.VMEM((2, nh, LANES, ATT_TILE), jnp.bfloat16),
            pltpu.VMEM((2, nh, 1, ATT_TILE), jnp.float32),
            pltpu.VMEM((2, nh, HEAD_DIM + BF16_ROWS, ATT_TILE), jnp.float32),
        ],
        compiler_params=pltpu.CompilerParams(
            dimension_semantics=("arbitrary", "arbitrary", "arbitrary"),
            vmem_limit_bytes=VMEM_LIMIT),
        name="moba_attention",
    )(slopes, _moba_key_features(n_blk), qt, qt, k, vt)


def _softplus2(z2):
    return jnp.maximum(z2, jnp.log(1.0 + jnp.exp2(jnp.minimum(z2, EXP2_ARG_MAX))) * LOG2E)


def _sb_kernel(later_ref, qt_ref, k_ref, vt_ref, o_ref, suffix_ref, acc_ref, *, nh, nt):
    first_tile = pl.program_id(2) * nt
    tile = ATT_TILE
    heads_per_group = LANES // HEAD_DIM

    def lane_group(hh):
        first = hh // heads_per_group * LANES
        return slice(first, first + LANES)

    qm = {(ti, hh): _head_masked(qt_ref[0, lane_group(hh), ti * tile:(ti + 1) * tile], hh % heads_per_group)
          for ti in range(nt) for hh in range(nh)}
    key_pos = lax.broadcasted_iota(jnp.int32, (tile, tile), 0)
    qry_pos = lax.broadcasted_iota(jnp.int32, (tile, tile), 1)
    strict = key_pos < qry_pos

    def key_block(j, hh):
        return k_ref[0, pl.ds(pl.multiple_of(j * tile, tile), tile), lane_group(hh)]

    def later_sums(sp2):
        return _dot(later_ref[...], sp2.astype(jnp.bfloat16))

    def values(j, hh):
        return vt_ref[0, j, hh * HEAD_DIM:(hh + 1) * HEAD_DIM, :]

    streams = [(ti, hh) for ti in range(nt) for hh in range(nh)]
    units = ([(ti, hh, first_tile + ti, True) for ti, hh in streams]
             + [(ti, hh, jnp.maximum(first_tile + ti - 1, 0), False) for ti, hh in streams])
    half = tile // 2

    def live_parts(x):
        return x[:half], x[half:, half:]

    def full_tile(parts):
        first, last = parts
        return jnp.concatenate([first, jnp.concatenate([jnp.zeros_like(last), last], axis=1)], axis=0)

    strict_parts = live_parts(strict)
    z2s, sp2s, inners, weights, totals, outs = {}, {}, {}, {}, {}, {}
    lag_softplus, lag_sums, lag_weights, lag_values = SB_STAGE_LAGS
    for step in range(len(units) + lag_values):
        if step < len(units):
            ti, hh, j, own = units[step]
            z2 = _dot(key_block(j, hh), qm[ti, hh])
            if own:
                z2s[step] = [jnp.where(m, part, MASKED) for m, part in zip(strict_parts, live_parts(z2))]
            else:
                z2s[step] = [z2]
        u = step - lag_softplus
        if 0 <= u < len(units):
            sp2s[u] = [_softplus2(part) for part in z2s[u]]
        u = step - lag_sums
        if 0 <= u < len(units):
            sp2_bf = [part.astype(jnp.bfloat16) for part in sp2s[u]]
            inners[u] = later_sums(full_tile(sp2_bf) if units[u][3] else sp2_bf[0])
        u = step - lag_weights
        if 0 <= u < len(units):
            ti, _, _, own = units[u]
            totals[u] = inners[u][0:1] + sp2s[u][0][0:1]
            inner_parts = live_parts(inners[u]) if own else [inners[u]]
            expo = [z2 - sp2 - inner for z2, sp2, inner in zip(z2s[u], sp2s[u], inner_parts)]
            if own:
                weights[u] = full_tile([jnp.exp2(part).astype(jnp.bfloat16) for part in expo])
            else:
                own_total = totals[u - len(streams)]
                suffix = jnp.where(first_tile + ti > 0, own_total, -MASKED)
                weights[u] = jnp.exp2(expo[0] - suffix).astype(jnp.bfloat16)
        u = step - lag_values
        if 0 <= u < len(units):
            _, hh, j, _ = units[u]
            outs[u] = _dot(values(j, hh), weights[u])
    for c, (ti, hh) in enumerate(streams):
        acc_ref[ti, hh] = outs[c] + outs[len(streams) + c]
        suffix_ref[ti, hh] = totals[c] + totals[len(streams) + c]

    for ti in range(nt):
        def smallest_suffix(ti=ti):
            return functools.reduce(jnp.minimum, [jnp.min(suffix_ref[ti, hh]) for hh in range(nh)])

        def more_to_do(carry):
            j, smallest = carry
            return jnp.logical_and(j >= 0, smallest <= SB_SUFFIX_CUTOFF)

        def farther_block(carry, ti=ti, smallest_suffix=smallest_suffix):
            j, _ = carry
            for hh in range(nh):
                z2 = _dot(key_block(j, hh), qm[ti, hh])
                sp2 = _softplus2(z2)
                inner = later_sums(sp2)
                w = jnp.exp2(z2 - sp2 - inner - suffix_ref[ti, hh])
                acc_ref[ti, hh] += _dot(values(j, hh), w.astype(jnp.bfloat16))
                suffix_ref[ti, hh] += inner[0:1] + sp2[0:1]
            return j - 1, smallest_suffix()

        lax.while_loop(more_to_do, farther_block, (first_tile + ti - 2, smallest_suffix()))
        _store_heads(o_ref.at[:, ti * tile:(ti + 1) * tile], [acc_ref[ti, hh] for hh in range(nh)])


def _sb_attention(qt, k, vt):
    b, d, s = qt.shape
    nh = SB_HEADS
    nt = SB_TILES
    gw = nh * HEAD_DIM
    n_blk = s // ATT_TILE
    assert n_blk % nt == 0
    later = jnp.asarray(np.triu(np.ones((ATT_TILE, ATT_TILE), np.float32), 1), jnp.bfloat16)
    return pl.pallas_call(
        functools.partial(_sb_kernel, nh=nh, nt=nt),
        grid=(b, d // gw, n_blk // nt),
        in_specs=[
            pl.BlockSpec((ATT_TILE, ATT_TILE), lambda i, j, t: (0, 0)),
            pl.BlockSpec((1, gw, nt * ATT_TILE), lambda i, j, t: (i, j, t)),
            pl.BlockSpec((1, s, gw), lambda i, j, t: (i, 0, j)),
            pl.BlockSpec((1, n_blk, gw, ATT_TILE), lambda i, j, t: (i, 0, j, 0)),
        ],
        out_specs=pl.BlockSpec((1, nt * ATT_TILE, gw), lambda i, j, t: (i, t, j)),
        out_shape=jax.ShapeDtypeStruct((b, s, d), jnp.bfloat16),
        scratch_shapes=[
            pltpu.VMEM((nt, nh, 1, ATT_TILE), jnp.float32),
            pltpu.VMEM((nt, nh, HEAD_DIM, ATT_TILE), jnp.float32),
        ],
        compiler_params=pltpu.CompilerParams(
            dimension_semantics=("arbitrary", "arbitrary", "arbitrary"),
            vmem_limit_bytes=VMEM_LIMIT),
        name="stick_breaking_attention",
    )(later, qt, k, vt)


def kernel(x, norm_g, w_in, w_out, final_g):
    b, s, d = x.shape
    depth = norm_g.shape[0]
    assert depth >= 1 and d == N_HEADS * HEAD_DIM and s % PROJ_ROWS == 0 and PROJ_ROWS % ATT_TILE == 0
    assert s % WIDE_ROWS == 0 and WIDE_ROWS % ATT_TILE == 0
    slopes = jnp.asarray(2.0 ** (-8.0 * np.arange(1, N_HEADS + 1) / N_HEADS), jnp.float32)

    w_q_t = jnp.swapaxes(w_in[:, :, 0:d], 1, 2).astype(jnp.bfloat16)
    w_v_t = jnp.swapaxes(w_in[:, :, 2 * d:3 * d], 1, 2).astype(jnp.bfloat16)
    w_out_bf = w_out.astype(jnp.bfloat16)

    def proj_weights(layer):
        return w_in, w_q_t, w_v_t, layer, layer + 1 < depth

    h = x
    outs = _layer_boundary(h, norm_g[0], proj=proj_weights(0), rows=WIDE_ROWS)
    for i in range(depth):
        last = i + 1 == depth
        k, qt, vt = outs[0], outs[-2], outs[-1]
        gate = (norm_g[i], w_in) if last else outs[1]
        o_parts = _moba_attention(qt, k, vt, slopes) if i % 2 == 0 else [_sb_attention(qt, k, vt)]
        attn = (o_parts, gate, w_out_bf, i)
        if last:
            rows = WIDE_ROWS if len(o_parts) == 1 else PROJ_ROWS
            (h,) = _layer_boundary(h, final_g, attn=attn, rows=rows)
        else:
            h, *outs = _layer_boundary(h, norm_g[i + 1], attn=attn, proj=proj_weights(i + 1))
    return h
```

```python
import functools
import math

import numpy as np
import jax
import jax.numpy as jnp
from jax import lax
from jax.experimental import pallas as pl
from jax.experimental.pallas import tpu as pltpu

N_HEADS = 16
HEAD_DIM = 64
MOBA_BLOCK = 256
MOBA_TOPK = 3
NORM_EPS = 1e-6

LANES = 128
BF16_ROWS = 16
ATT_TILE = MOBA_BLOCK
PROJ_ROWS = 512
WIDE_ROWS = 1024
MOBA_HEADS = 8
SB_HEADS = 4
SB_TILES = 8
SOFTMAX_LAG = 2
SB_STAGE_LAGS = (1, 2, 3, 4)
LOG2E = math.log2(math.e)
MASKED = -1e30
EXP2_ARG_MAX = 126.0
SB_SUFFIX_CUTOFF = 128.0 * LOG2E
VMEM_LIMIT = 56 * 1024 * 1024


def _dot(a, b):
    return jnp.dot(a, b, preferred_element_type=jnp.float32)


def _dot_nt(a, b):
    return lax.dot_general(a, b, (((1,), (1,)), ((), ())), preferred_element_type=jnp.float32)


def _bf16_round(x):
    return x.astype(jnp.bfloat16).astype(jnp.float32)


def _rms_norm(h, g_ref):
    return h * lax.rsqrt(jnp.mean(h * h, axis=-1, keepdims=True) + NORM_EPS) * g_ref[...]


def _boundary_kernel(*refs, n_o, half_steps, gate_in, has_proj, emit_z, d, scale):
    refs = list(refs)
    if n_o:
        o_refs = [refs.pop(0) for _ in range(n_o)]
        if gate_in:
            z_ref = refs.pop(0)
        else:
            gate_gain_ref, gate_w_ref = refs.pop(0), refs.pop(0)
        h_ref, wout_ref = refs.pop(0), refs.pop(0)
    else:
        h_ref = refs.pop(0)
    g_ref = refs.pop(0)
    if has_proj:
        wk_ref = refs.pop(0)
        wz_ref = refs.pop(0) if emit_z else None
        wqt_ref, wvt_ref = refs.pop(0), refs.pop(0)
    if n_o:
        hout_ref = refs.pop(0)

    h = h_ref[0]
    if n_o:
        if n_o == 1:
            o = o_refs[0][0]
        else:
            o = jnp.where(pl.program_id(1) < half_steps, o_refs[0][0], o_refs[1][0])
        if gate_in:
            z = z_ref[0].astype(jnp.float32)
        else:
            z = _dot(_rms_norm(h, gate_gain_ref).astype(jnp.bfloat16), gate_w_ref[0].astype(jnp.bfloat16))
        gated = o.astype(jnp.float32) * (z / (1.0 + jnp.exp(-z)))
        h = h + _dot(gated.astype(jnp.bfloat16), wout_ref[0])
    y = _rms_norm(h, g_ref)
    if n_o:
        hout_ref[0] = h if has_proj else y
    if has_proj:
        k_ref = refs.pop(0)
        znext_ref = refs.pop(0) if emit_z else None
        qt_ref, vt_ref = refs
        xn = y.astype(jnp.bfloat16)
        k_ref[0] = _dot(xn, wk_ref[0].astype(jnp.bfloat16)).astype(k_ref.dtype)
        if emit_z:
            znext_ref[0] = _dot(xn, wz_ref[0].astype(jnp.bfloat16)).astype(znext_ref.dtype)
        qt_ref[0] = (_dot_nt(wqt_ref[0], xn) * scale).astype(qt_ref.dtype)
        v_t = _dot_nt(wvt_ref[0], xn)
        for c in range(vt_ref.shape[1]):
            vt_ref[0, c] = v_t[:, c * ATT_TILE:(c + 1) * ATT_TILE].astype(vt_ref.dtype)


def _layer_boundary(h, gain, attn=None, proj=None, rows=PROJ_ROWS):
    b, s, d = h.shape
    n_steps = s // rows
    half = n_steps // 2
    n_blk = s // ATT_TILE
    row_spec = pl.BlockSpec((1, rows, d), lambda i, t: (i, t, 0))

    def const_spec(shape):
        return pl.BlockSpec(shape, lambda i, t: (0,) * len(shape), pipeline_mode=pl.Buffered(1))

    def weight_spec(layer, col_block):
        return pl.BlockSpec((1, d, d), lambda i, t: (layer, 0, col_block), pipeline_mode=pl.Buffered(1))

    in_specs, args, out_specs, out_shape = [], [], [], []
    n_o = 0
    gate_in = True
    if attn is not None:
        o_parts, gate, w_out, out_layer = attn
        n_o = len(o_parts)
        if n_o == 1:
            in_specs.append(row_spec)
        else:
            assert rows == PROJ_ROWS
            in_specs += [
                pl.BlockSpec((1, rows, d), lambda i, t: (i, jnp.minimum(t, half - 1), 0)),
                pl.BlockSpec((1, rows, d), lambda i, t: (i, jnp.clip(n_steps - 1 - t, 0, half - 1), 0)),
            ]
        args += list(o_parts)
        gate_in = not isinstance(gate, tuple)
        if gate_in:
            in_specs.append(row_spec)
            args.append(gate)
        else:
            gate_gain, w_in_all = gate
            in_specs += [const_spec((1, d)), weight_spec(out_layer, 3)]
            args += [gate_gain.reshape(1, d), w_in_all]
        in_specs += [row_spec, weight_spec(out_layer, 0)]
        args += [h, w_out]
        out_specs.append(row_spec)
        out_shape.append(jax.ShapeDtypeStruct((b, s, d), jnp.float32))
    else:
        in_specs.append(row_spec)
        args.append(h)
    in_specs.append(const_spec((1, d)))
    args.append(gain.reshape(1, d))
    emit_z = False
    if proj is not None:
        w_in, w_q_t, w_v_t, layer, emit_z = proj
        token_major = (pl.BlockSpec((1, rows, d), lambda i, t: (i, t, 0)),
                       jax.ShapeDtypeStruct((b, s, d), jnp.bfloat16))
        in_specs += [weight_spec(layer, 1)] + ([weight_spec(layer, 3)] if emit_z else [])
        in_specs += [weight_spec(layer, 0), weight_spec(layer, 0)]
        args += [w_in] + ([w_in] if emit_z else []) + [w_q_t, w_v_t]
        outputs = [token_major] + ([token_major] if emit_z else []) + [
            (pl.BlockSpec((1, d, rows), lambda i, t: (i, 0, t)),
             jax.ShapeDtypeStruct((b, d, s), jnp.bfloat16)),
            (pl.BlockSpec((1, rows // ATT_TILE, d, ATT_TILE), lambda i, t: (i, t, 0, 0)),
             jax.ShapeDtypeStruct((b, n_blk, d, ATT_TILE), jnp.bfloat16)),
        ]
        out_specs += [spec for spec, _ in outputs]
        out_shape += [shape for _, shape in outputs]
    kern = functools.partial(_boundary_kernel, n_o=n_o, half_steps=half, gate_in=gate_in,
                             has_proj=proj is not None, emit_z=emit_z, d=d, scale=HEAD_DIM ** -0.5 * LOG2E)
    name = ("gate_outproj_" if attn is not None else "") + ("rmsnorm_qkvz_proj" if proj is not None else "final_rmsnorm")
    return pl.pallas_call(
        kern,
        grid=(b, n_steps),
        in_specs=in_specs,
        out_specs=out_specs,
        out_shape=out_shape,
        compiler_params=pltpu.CompilerParams(
            dimension_semantics=("arbitrary", "arbitrary"), vmem_limit_bytes=VMEM_LIMIT),
        name=name,
    )(*args)


def _head_masked(qt, hh):
    row = lax.broadcasted_iota(jnp.int32, qt.shape, 0)
    return jnp.where((row >= hh * HEAD_DIM) & (row < (hh + 1) * HEAD_DIM), qt, jnp.zeros_like(qt))


def _store_heads(o_ref, outs_t):
    stacked = jnp.concatenate(outs_t, axis=0)
    o_ref[0] = stacked.T.astype(o_ref.dtype)


def _moba_key_features(n_blk):
    pos = np.arange(n_blk * ATT_TILE)
    feat = np.zeros((n_blk * ATT_TILE, HEAD_DIM), np.float32)
    feat[pos, pos // ATT_TILE] = 1.0
    feat[:, n_blk:n_blk + 3] = (pos // ATT_TILE)[:, None]
    feat[:, n_blk + 3:n_blk + 6] = (pos % ATT_TILE)[:, None]
    return jnp.asarray(np.concatenate([feat, feat], axis=1), jnp.bfloat16)


def _moba_kernel(slopes_ref, kfeat_ref, qa_ref, qb_ref, k_ref, vt_ref, olo_ref, ohi_ref,
                 kmean_ref, kaug_ref, qaug_ref, m_ref, acc_ref):
    nh = MOBA_HEADS
    hg = pl.program_id(1)
    p = pl.program_id(2)
    n_blk = kmean_ref.shape[0]
    tile = ATT_TILE
    q_tile = (p, n_blk - 1 - p)
    q_refs = (qa_ref, qb_ref)
    key_pos = lax.broadcasted_iota(jnp.int32, (tile, tile), 0)
    qry_pos = lax.broadcasted_iota(jnp.int32, (tile, tile), 1)

    heads_per_group = LANES // HEAD_DIM

    def own_half(hh):
        first = hh % heads_per_group * HEAD_DIM
        return slice(first, first + HEAD_DIM)

    def head_cols(hh):
        return slice(hh * HEAD_DIM, (hh + 1) * HEAD_DIM)

    @pl.when(p == 0)
    def _():
        k_all = k_ref[0]
        kmean_ref[...] = jnp.mean(k_all.astype(jnp.float32).reshape(n_blk, tile, nh * HEAD_DIM), axis=1)
        for hh in range(nh):
            kaug_ref[hh] = kfeat_ref[...]
            kaug_ref[hh, :, own_half(hh)] = k_all[:, head_cols(hh)]

    crow = lax.broadcasted_iota(jnp.int32, (HEAD_DIM - n_blk, tile), 0)
    no_choice = jnp.zeros((n_blk, tile), jnp.float32)
    for hh in range(nh):
        slope = slopes_ref[hg * nh + hh] * LOG2E
        base = jnp.where(crow < 3, slope * tile, jnp.where(crow < 6, slope, 0.0))
        part0 = _bf16_round(base)
        part1 = _bf16_round(base - part0)
        part2 = base - part0 - part1
        slope_rows = jnp.where((crow == 0) | (crow == 3), part0,
                               jnp.where((crow == 1) | (crow == 4), part1, part2))
        bias_rows = jnp.concatenate([no_choice, slope_rows], axis=0).astype(jnp.bfloat16)
        for sel in range(2):
            q_rows = q_refs[sel][0, head_cols(hh), :]
            first_half = own_half(hh).start == 0
            qaug_ref[sel, hh] = jnp.concatenate([q_rows, bias_rows] if first_half else [bias_rows, q_rows], axis=0)

    gate_parts = {}
    group_lane = lax.broadcasted_iota(jnp.int32, (n_blk, LANES), 1)
    for group in range(nh // heads_per_group):
        heads = range(group * heads_per_group, (group + 1) * heads_per_group)
        kmean = kmean_ref[:, group * LANES:(group + 1) * LANES]
        gate_lhs = []
        for hh in heads:
            in_own_half = (group_lane >= own_half(hh).start) & (group_lane < own_half(hh).stop)
            kmean_h = jnp.where(in_own_half, kmean, 0.0)
            kmean_hi = kmean_h.astype(jnp.bfloat16)
            gate_lhs += [kmean_hi, (kmean_h - kmean_hi.astype(jnp.float32)).astype(jnp.bfloat16)]
        gate_lhs = jnp.concatenate(gate_lhs, axis=0)
        for sel in range(2):
            g = _dot(gate_lhs, q_refs[sel][0, group * LANES:(group + 1) * LANES, :])
            for c, hh in enumerate(heads):
                gate_parts[sel, hh] = g[2 * c * n_blk:(2 * c + 1) * n_blk] + g[(2 * c + 1) * n_blk:(2 * c + 2) * n_blk]

    def store_block_choice(heads):
        pairs = [(sel, hh) for sel in range(2) for hh in heads]
        gates = jnp.concatenate([gate_parts[pair] for pair in pairs], axis=1)
        blk = lax.broadcasted_iota(jnp.int32, gates.shape, 0).astype(jnp.float32)
        lane = lax.broadcasted_iota(jnp.int32, gates.shape, 1)
        n_past = jnp.where(lane < len(heads) * tile, q_tile[0], q_tile[1]).astype(jnp.float32)
        gate = jnp.where(blk < n_past, gates, -jnp.inf)
        chosen = blk == n_past
        for _ in range(MOBA_TOPK):
            top = jnp.max(gate, axis=0, keepdims=True)
            at_top = (gate == top) & (top > -jnp.inf)
            first = jnp.min(jnp.where(at_top, blk, float(n_blk)), axis=0, keepdims=True)
            pick = blk == first
            chosen = jnp.logical_or(chosen, pick)
            gate = jnp.where(pick, -jnp.inf, gate)
        choice = jnp.where(chosen, 0.0, MASKED).astype(jnp.bfloat16)
        for c, (sel, hh) in enumerate(pairs):
            first_bias_row = HEAD_DIM - own_half(hh).start
            qaug_ref[sel, hh, first_bias_row:first_bias_row + n_blk] = choice[:, c * tile:(c + 1) * tile]

    def key_block(j, hh):
        return kaug_ref[hh, pl.ds(pl.multiple_of(j * tile, tile), tile), :]

    def slot_of(s):
        is_b = s >= p
        return is_b.astype(jnp.int32), jnp.where(is_b, s - p, s)

    causal = key_pos <= qry_pos
    ones_rows = jnp.ones((BF16_ROWS, tile), jnp.bfloat16)

    def weighted_values(j, hh, prob):
        v_aug = jnp.concatenate([vt_ref[0, j, hh * HEAD_DIM:(hh + 1) * HEAD_DIM, :], ones_rows], axis=0)
        return _dot(v_aug, prob.astype(jnp.bfloat16))

    def finish(sel, heads):
        o_ref = (olo_ref, ohi_ref)[sel]
        cols = slice(heads[0] * HEAD_DIM, (heads[-1] + 1) * HEAD_DIM)
        _store_heads(o_ref.at[:, :, cols],
                     [acc_ref[sel, hh, :HEAD_DIM] * (1.0 / acc_ref[sel, hh, HEAD_DIM:HEAD_DIM + 1]) for hh in heads])

    group_items = [(True, sel) for sel in range(2)] + [(False, s) for s in range(n_blk - 1)]
    first_tile_done = (False, n_blk // 2 - 2) if n_blk >= 4 else (True, 1)
    items = [(group, own, idx) for group in range(nh // heads_per_group) for own, idx in group_items]
    in_flight = {}
    for step in range(len(items) + SOFTMAX_LAG):
        if step < len(items):
            group, own, idx = items[step]
            heads = list(range(group * heads_per_group, (group + 1) * heads_per_group))
            if (own, idx) == group_items[2]:
                store_block_choice(heads)
            sel, j = (idx, q_tile[idx]) if own else slot_of(idx)
            logits = [_dot(key_block(j, hh), qaug_ref[sel, hh]) for hh in heads]
            if own:
                logits = [jnp.where(causal, lg, MASKED) for lg in logits]
            in_flight[step] = (heads, own, sel, j, logits)
        done = step - SOFTMAX_LAG
        if done >= 0:
            heads, own, sel, j, logits = in_flight.pop(done)
            for hh, lg in zip(heads, logits):
                top = jnp.max(lg, axis=0, keepdims=True)
                if own:
                    m_ref[sel, hh] = top
                    acc_ref[sel, hh] = weighted_values(j, hh, jnp.exp2(lg - top))
                else:
                    m_old = m_ref[sel, hh]
                    m_new = jnp.maximum(m_old, top)
                    m_ref[sel, hh] = m_new
                    acc_ref[sel, hh] = (jnp.exp2(m_old - m_new) * acc_ref[sel, hh]
                                        + weighted_values(j, hh, jnp.exp2(lg - m_new)))
            if items[done][1:] == first_tile_done:
                finish(0, heads)
            if items[done][1:] == group_items[-1]:
                finish(1, heads)


def _moba_attention(qt, k, vt, slopes):
    b, d, s = qt.shape
    nh = MOBA_HEADS
    gw = nh * HEAD_DIM
    n_blk = s // ATT_TILE
    assert gw % LANES == 0 and n_blk % 2 == 0 and PROJ_ROWS == 2 * ATT_TILE and HEAD_DIM >= n_blk + 6
    half = jax.ShapeDtypeStruct((b, s // 2, d), jnp.bfloat16)
    return pl.pallas_call(
        _moba_kernel,
        grid=(b, d // gw, n_blk // 2),
        in_specs=[
            pl.BlockSpec(memory_space=pltpu.SMEM),
            pl.BlockSpec((s, LANES), lambda i, j, t: (0, 0)),
            pl.BlockSpec((1, gw, ATT_TILE), lambda i, j, t: (i, j, t)),
            pl.BlockSpec((1, gw, ATT_TILE), lambda i, j, t: (i, j, n_blk - 1 - t)),
            pl.BlockSpec((1, s, gw), lambda i, j, t: (i, 0, j)),
            pl.BlockSpec((1, n_blk, gw, ATT_TILE), lambda i, j, t: (i, 0, j, 0)),
        ],
        out_specs=[
            pl.BlockSpec((1, ATT_TILE, gw), lambda i, j, t: (i, t, j)),
            pl.BlockSpec((1, ATT_TILE, gw), lambda i, j, t: (i, lax.bitwise_xor(t, 1), j)),
        ],
        out_shape=[half, half],
        scratch_shapes=[
            pltpu.VMEM((n_blk, gw), jnp.float32),
            pltpu.VMEM((nh, s, LANES), jnp.bfloat16),
            pltpu.VMEM((2, nh, LANES, ATT_TILE), jnp.bfloat16),
            pltpu.VMEM((2, nh, 1, ATT_TILE), jnp.float32),
            pltpu.VMEM((2, nh, HEAD_DIM + BF16_ROWS, ATT_TILE), jnp.float32),
        ],
        compiler_params=pltpu.CompilerParams(
            dimension_semantics=("arbitrary", "arbitrary", "arbitrary"),
            vmem_limit_bytes=VMEM_LIMIT),
        name="moba_attention",
    )(slopes, _moba_key_features(n_blk), qt, qt, k, vt)


def _softplus2(z2):
    return jnp.maximum(z2, jnp.log(1.0 + jnp.exp2(jnp.minimum(z2, EXP2_ARG_MAX))) * LOG2E)


def _sb_kernel(later_ref, qt_ref, k_ref, vt_ref, o_ref, suffix_ref, acc_ref, *, nh, nt):
    first_tile = pl.program_id(2) * nt
    tile = ATT_TILE
    heads_per_group = LANES // HEAD_DIM

    def lane_group(hh):
        first = hh // heads_per_group * LANES
        return slice(first, first + LANES)

    qm = {(ti, hh): _head_masked(qt_ref[0, lane_group(hh), ti * tile:(ti + 1) * tile], hh % heads_per_group)
          for ti in range(nt) for hh in range(nh)}
    key_pos = lax.broadcasted_iota(jnp.int32, (tile, tile), 0)
    qry_pos = lax.broadcasted_iota(jnp.int32, (tile, tile), 1)
    strict = key_pos < qry_pos

    def key_block(j, hh):
        return k_ref[0, pl.ds(pl.multiple_of(j * tile, tile), tile), lane_group(hh)]

    def later_sums(sp2):
        return _dot(later_ref[...], sp2.astype(jnp.bfloat16))

    def values(j, hh):
        return vt_ref[0, j, hh * HEAD_DIM:(hh + 1) * HEAD_DIM, :]

    streams = [(ti, hh) for ti in range(nt) for hh in range(nh)]
    units = ([(ti, hh, first_tile + ti, True) for ti, hh in streams]
             + [(ti, hh, jnp.maximum(first_tile + ti - 1, 0), False) for ti, hh in streams])
    half = tile // 2

    def live_parts(x):
        return x[:half], x[half:, half:]

    def full_tile(parts):
        first, last = parts
        return jnp.concatenate([first, jnp.concatenate([jnp.zeros_like(last), last], axis=1)], axis=0)

    strict_parts = live_parts(strict)
    z2s, sp2s, inners, weights, totals, outs = {}, {}, {}, {}, {}, {}
    lag_softplus, lag_sums, lag_weights, lag_values = SB_STAGE_LAGS
    for step in range(len(units) + lag_values):
        if step < len(units):
            ti, hh, j, own = units[step]
            z2 = _dot(key_block(j, hh), qm[ti, hh])
            if own:
                z2s[step] = [jnp.where(m, part, MASKED) for m, part in zip(strict_parts, live_parts(z2))]
            else:
                z2s[step] = [z2]
        u = step - lag_softplus
        if 0 <= u < len(units):
            sp2s[u] = [_softplus2(part) for part in z2s[u]]
        u = step - lag_sums
        if 0 <= u < len(units):
            sp2_bf = [part.astype(jnp.bfloat16) for part in sp2s[u]]
            inners[u] = later_sums(full_tile(sp2_bf) if units[u][3] else sp2_bf[0])
        u = step - lag_weights
        if 0 <= u < len(units):
            ti, _, _, own = units[u]
            totals[u] = inners[u][0:1] + sp2s[u][0][0:1]
            inner_parts = live_parts(inners[u]) if own else [inners[u]]
            expo = [z2 - sp2 - inner for z2, sp2, inner in zip(z2s[u], sp2s[u], inner_parts)]
            if own:
                weights[u] = full_tile([jnp.exp2(part).astype(jnp.bfloat16) for part in expo])
            else:
                own_total = totals[u - len(streams)]
                suffix = jnp.where(first_tile + ti > 0, own_total, -MASKED)
                weights[u] = jnp.exp2(expo[0] - suffix).astype(jnp.bfloat16)
        u = step - lag_values
        if 0 <= u < len(units):
            _, hh, j, _ = units[u]
            outs[u] = _dot(values(j, hh), weights[u])
    for c, (ti, hh) in enumerate(streams):
        acc_ref[ti, hh] = outs[c] + outs[len(streams) + c]
        suffix_ref[ti, hh] = totals[c] + totals[len(streams) + c]

    for ti in range(nt):
        def smallest_suffix(ti=ti):
            return functools.reduce(jnp.minimum, [jnp.min(suffix_ref[ti, hh]) for hh in range(nh)])

        def more_to_do(carry):
            j, smallest = carry
            return jnp.logical_and(j >= 0, smallest <= SB_SUFFIX_CUTOFF)

        def farther_block(carry, ti=ti, smallest_suffix=smallest_suffix):
            j, _ = carry
            for hh in range(nh):
                z2 = _dot(key_block(j, hh), qm[ti, hh])
                sp2 = _softplus2(z2)
                inner = later_sums(sp2)
                w = jnp.exp2(z2 - sp2 - inner - suffix_ref[ti, hh])
                acc_ref[ti, hh] += _dot(values(j, hh), w.astype(jnp.bfloat16))
                suffix_ref[ti, hh] += inner[0:1] + sp2[0:1]
            return j - 1, smallest_suffix()

        lax.while_loop(more_to_do, farther_block, (first_tile + ti - 2, smallest_suffix()))
        _store_heads(o_ref.at[:, ti * tile:(ti + 1) * tile], [acc_ref[ti, hh] for hh in range(nh)])


def _sb_attention(qt, k, vt):
    b, d, s = qt.shape
    nh = SB_HEADS
    nt = SB_TILES
    gw = nh * HEAD_DIM
    n_blk = s // ATT_TILE
    assert n_blk % nt == 0
    later = jnp.asarray(np.triu(np.ones((ATT_TILE, ATT_TILE), np.float32), 1), jnp.bfloat16)
    return pl.pallas_call(
        functools.partial(_sb_kernel, nh=nh, nt=nt),
        grid=(b, d // gw, n_blk // nt),
        in_specs=[
            pl.BlockSpec((ATT_TILE, ATT_TILE), lambda i, j, t: (0, 0)),
            pl.BlockSpec((1, gw, nt * ATT_TILE), lambda i, j, t: (i, j, t)),
            pl.BlockSpec((1, s, gw), lambda i, j, t: (i, 0, j)),
            pl.BlockSpec((1, n_blk, gw, ATT_TILE), lambda i, j, t: (i, 0, j, 0)),
        ],
        out_specs=pl.BlockSpec((1, nt * ATT_TILE, gw), lambda i, j, t: (i, t, j)),
        out_shape=jax.ShapeDtypeStruct((b, s, d), jnp.bfloat16),
        scratch_shapes=[
            pltpu.VMEM((nt, nh, 1, ATT_TILE), jnp.float32),
            pltpu.VMEM((nt, nh, HEAD_DIM, ATT_TILE), jnp.float32),
        ],
        compiler_params=pltpu.CompilerParams(
            dimension_semantics=("arbitrary", "arbitrary", "arbitrary"),
            vmem_limit_bytes=VMEM_LIMIT),
        name="stick_breaking_attention",
    )(later, qt, k, vt)


def kernel(x, norm_g, w_in, w_out, final_g):
    b, s, d = x.shape
    depth = norm_g.shape[0]
    assert depth >= 1 and d == N_HEADS * HEAD_DIM and s % PROJ_ROWS == 0 and PROJ_ROWS % ATT_TILE == 0
    assert s % WIDE_ROWS == 0 and WIDE_ROWS % ATT_TILE == 0
    slopes = jnp.asarray(2.0 ** (-8.0 * np.arange(1, N_HEADS + 1) / N_HEADS), jnp.float32)

    w_q_t = jnp.swapaxes(w_in[:, :, 0:d], 1, 2).astype(jnp.bfloat16)
    w_v_t = jnp.swapaxes(w_in[:, :, 2 * d:3 * d], 1, 2).astype(jnp.bfloat16)
    w_out_bf = w_out.astype(jnp.bfloat16)

    def proj_weights(layer):
        return w_in, w_q_t, w_v_t, layer, layer + 1 < depth

    h = x
    outs = _layer_boundary(h, norm_g[0], proj=proj_weights(0), rows=WIDE_ROWS)
    for i in range(depth):
        last = i + 1 == depth
        k, qt, vt = outs[0], outs[-2], outs[-1]
        gate = (norm_g[i], w_in) if last else outs[1]
        o_parts = _moba_attention(qt, k, vt, slopes) if i % 2 == 0 else [_sb_attention(qt, k, vt)]
        attn = (o_parts, gate, w_out_bf, i)
        if last:
            rows = WIDE_ROWS if len(o_parts) == 1 else PROJ_ROWS
            (h,) = _layer_boundary(h, final_g, attn=attn, rows=rows)
        else:
            h, *outs = _layer_boundary(h, norm_g[i + 1], attn=attn, proj=proj_weights(i + 1))
    return h
```

```python
import functools
import math

import numpy as np
import jax
import jax.numpy as jnp
from jax import lax
from jax.experimental import pallas as pl
from jax.experimental.pallas import tpu as pltpu

N_HEADS = 16
HEAD_DIM = 64
MOBA_BLOCK = 256
MOBA_TOPK = 3
NORM_EPS = 1e-6

LANES = 128
BF16_ROWS = 16
ATT_TILE = MOBA_BLOCK
PROJ_ROWS = 512
WIDE_ROWS = 1024
MOBA_HEADS = 8
SB_HEADS = 4
SB_TILES = 8
SOFTMAX_LAG = 2
SB_STAGE_LAGS = (1, 2, 3, 4)
LOG2E = math.log2(math.e)
MASKED = -1e30
EXP2_ARG_MAX = 126.0
SB_SUFFIX_CUTOFF = 128.0 * LOG2E
VMEM_LIMIT = 56 * 1024 * 1024


def _dot(a, b):
    return jnp.dot(a, b, preferred_element_type=jnp.float32)


def _dot_nt(a, b):
    return lax.dot_general(a, b, (((1,), (1,)), ((), ())), preferred_element_type=jnp.float32)


def _bf16_round(x):
    return x.astype(jnp.bfloat16).astype(jnp.float32)


def _rms_norm(h, g_ref):
    return h * lax.rsqrt(jnp.mean(h * h, axis=-1, keepdims=True) + NORM_EPS) * g_ref[...]


def _boundary_kernel(*refs, n_o, half_steps, gate_in, has_proj, emit_z, d, scale):
    refs = list(refs)
    if has_proj:
        wvt_ref, wqt_ref = refs.pop(), refs.pop()
    if n_o:
        o_refs = [refs.pop(0) for _ in range(n_o)]
        if gate_in:
            z_ref = refs.pop(0)
        else:
            gate_gain_ref, gate_w_ref = refs.pop(0), refs.pop(0)
        h_ref, wout_ref = refs.pop(0), refs.pop(0)
    else:
        h_ref = refs.pop(0)
    g_ref = refs.pop(0)
    if has_proj:
        wk_ref = refs.pop(0)
        wz_ref = refs.pop(0) if emit_z else None
        wq_ref, wv_ref = refs.pop(0), refs.pop(0)

        @pl.when((pl.program_id(0) == 0) & (pl.program_id(1) == 0))
        def _():
            wqt_ref[...] = wq_ref[0].T.astype(wqt_ref.dtype)
            wvt_ref[...] = wv_ref[0].T.astype(wvt_ref.dtype)
    if n_o:
        hout_ref = refs.pop(0)

    h = h_ref[0]
    if n_o:
        if n_o == 1:
            o = o_refs[0][0]
        else:
            o = jnp.where(pl.program_id(1) < half_steps, o_refs[0][0], o_refs[1][0])
        if gate_in:
            z = z_ref[0].astype(jnp.float32)
        else:
            z = _dot(_rms_norm(h, gate_gain_ref).astype(jnp.bfloat16), gate_w_ref[0].astype(jnp.bfloat16))
        gated = o.astype(jnp.float32) * (z / (1.0 + jnp.exp(-z)))
        h = h + _dot(gated.astype(jnp.bfloat16), wout_ref[0])
    y = _rms_norm(h, g_ref)
    if n_o:
        hout_ref[0] = h if has_proj else y
    if has_proj:
        k_ref = refs.pop(0)
        znext_ref = refs.pop(0) if emit_z else None
        qt_ref, vt_ref = refs
        xn = y.astype(jnp.bfloat16)
        k_ref[0] = _dot(xn, wk_ref[0].astype(jnp.bfloat16)).astype(k_ref.dtype)
        if emit_z:
            znext_ref[0] = _dot(xn, wz_ref[0].astype(jnp.bfloat16)).astype(znext_ref.dtype)
        qt_ref[0] = (_dot_nt(wqt_ref[...], xn) * scale).astype(qt_ref.dtype)
        v_t = _dot_nt(wvt_ref[...], xn)
        for c in range(vt_ref.shape[1]):
            vt_ref[0, c] = v_t[:, c * ATT_TILE:(c + 1) * ATT_TILE].astype(vt_ref.dtype)


def _layer_boundary(h, gain, attn=None, proj=None, rows=PROJ_ROWS):
    b, s, d = h.shape
    n_steps = s // rows
    half = n_steps // 2
    n_blk = s // ATT_TILE
    row_spec = pl.BlockSpec((1, rows, d), lambda i, t: (i, t, 0))

    def const_spec(shape):
        return pl.BlockSpec(shape, lambda i, t: (0,) * len(shape), pipeline_mode=pl.Buffered(1))

    def weight_spec(layer, col_block):
        return pl.BlockSpec((1, d, d), lambda i, t: (layer, 0, col_block), pipeline_mode=pl.Buffered(1))

    in_specs, args, out_specs, out_shape = [], [], [], []
    n_o = 0
    gate_in = True
    if attn is not None:
        o_parts, gate, w_out, out_layer = attn
        n_o = len(o_parts)
        if n_o == 1:
            in_specs.append(row_spec)
        else:
            assert rows == PROJ_ROWS
            in_specs += [
                pl.BlockSpec((1, rows, d), lambda i, t: (i, jnp.minimum(t, half - 1), 0)),
                pl.BlockSpec((1, rows, d), lambda i, t: (i, jnp.clip(n_steps - 1 - t, 0, half - 1), 0)),
            ]
        args += list(o_parts)
        gate_in = not isinstance(gate, tuple)
        if gate_in:
            in_specs.append(row_spec)
            args.append(gate)
        else:
            gate_gain, w_in_all = gate
            in_specs += [const_spec((1, d)), weight_spec(out_layer, 3)]
            args += [gate_gain.reshape(1, d), w_in_all]
        in_specs += [row_spec, weight_spec(out_layer, 0)]
        args += [h, w_out]
        out_specs.append(row_spec)
        out_shape.append(jax.ShapeDtypeStruct((b, s, d), jnp.float32))
    else:
        in_specs.append(row_spec)
        args.append(h)
    in_specs.append(const_spec((1, d)))
    args.append(gain.reshape(1, d))
    emit_z = False
    scratch_shapes = []
    if proj is not None:
        w_in, layer, emit_z = proj
        token_major = (pl.BlockSpec((1, rows, d), lambda i, t: (i, t, 0)),
                       jax.ShapeDtypeStruct((b, s, d), jnp.bfloat16))
        in_specs += [weight_spec(layer, 1)] + ([weight_spec(layer, 3)] if emit_z else [])
        in_specs += [weight_spec(layer, 0), weight_spec(layer, 2)]
        args += [w_in] * (4 if emit_z else 3)
        scratch_shapes = [pltpu.VMEM((d, d), jnp.bfloat16)] * 2
        outputs = [token_major] + ([token_major] if emit_z else []) + [
            (pl.BlockSpec((1, d, rows), lambda i, t: (i, 0, t)),
             jax.ShapeDtypeStruct((b, d, s), jnp.bfloat16)),
            (pl.BlockSpec((1, rows // ATT_TILE, d, ATT_TILE), lambda i, t: (i, t, 0, 0)),
             jax.ShapeDtypeStruct((b, n_blk, d, ATT_TILE), jnp.bfloat16)),
        ]
        out_specs += [spec for spec, _ in outputs]
        out_shape += [shape for _, shape in outputs]
    kern = functools.partial(_boundary_kernel, n_o=n_o, half_steps=half, gate_in=gate_in,
                             has_proj=proj is not None, emit_z=emit_z, d=d, scale=HEAD_DIM ** -0.5 * LOG2E)
    name = ("gate_outproj_" if attn is not None else "") + ("rmsnorm_qkvz_proj" if proj is not None else "final_rmsnorm")
    return pl.pallas_call(
        kern,
        grid=(b, n_steps),
        in_specs=in_specs,
        out_specs=out_specs,
        out_shape=out_shape,
        scratch_shapes=scratch_shapes,
        compiler_params=pltpu.CompilerParams(
            dimension_semantics=("arbitrary", "arbitrary"), vmem_limit_bytes=VMEM_LIMIT),
        name=name,
    )(*args)


def _head_masked(qt, hh):
    row = lax.broadcasted_iota(jnp.int32, qt.shape, 0)
    return jnp.where((row >= hh * HEAD_DIM) & (row < (hh + 1) * HEAD_DIM), qt, jnp.zeros_like(qt))


def _store_heads(o_ref, outs_t):
    stacked = jnp.concatenate(outs_t, axis=0)
    o_ref[0] = stacked.T.astype(o_ref.dtype)


def _moba_key_features(n_blk):
    pos = np.arange(n_blk * ATT_TILE)
    feat = np.zeros((n_blk * ATT_TILE, HEAD_DIM), np.float32)
    feat[pos, pos // ATT_TILE] = 1.0
    feat[:, n_blk:n_blk + 3] = (pos // ATT_TILE)[:, None]
    feat[:, n_blk + 3:n_blk + 6] = (pos % ATT_TILE)[:, None]
    return jnp.asarray(np.concatenate([feat, feat], axis=1), jnp.bfloat16)


def _moba_kernel(slopes_ref, kfeat_ref, qa_ref, qb_ref, k_ref, vt_ref, olo_ref, ohi_ref,
                 kmean_ref, kaug_ref, qaug_ref, m_ref, acc_ref):
    nh = MOBA_HEADS
    hg = pl.program_id(1)
    p = pl.program_id(2)
    n_blk = kmean_ref.shape[0]
    tile = ATT_TILE
    q_tile = (p, n_blk - 1 - p)
    q_refs = (qa_ref, qb_ref)
    key_pos = lax.broadcasted_iota(jnp.int32, (tile, tile), 0)
    qry_pos = lax.broadcasted_iota(jnp.int32, (tile, tile), 1)

    heads_per_group = LANES // HEAD_DIM

    def own_half(hh):
        first = hh % heads_per_group * HEAD_DIM
        return slice(first, first + HEAD_DIM)

    def head_cols(hh):
        return slice(hh * HEAD_DIM, (hh + 1) * HEAD_DIM)

    @pl.when(p == 0)
    def _():
        k_all = k_ref[0]
        kmean_ref[...] = jnp.mean(k_all.astype(jnp.float32).reshape(n_blk, tile, nh * HEAD_DIM), axis=1)
        for hh in range(nh):
            kaug_ref[hh, :, own_half(hh)] = k_all[:, head_cols(hh)]

    @pl.when((pl.program_id(0) == 0) & (hg == 0) & (p == 0))
    def _():
        for hh in range(nh):
            first = HEAD_DIM - own_half(hh).start
            kaug_ref[hh, :, first:first + HEAD_DIM] = kfeat_ref[:, first:first + HEAD_DIM]

    crow = lax.broadcasted_iota(jnp.int32, (HEAD_DIM - n_blk, tile), 0)
    no_choice = jnp.zeros((n_blk, tile), jnp.float32)
    for hh in range(nh):
        slope = slopes_ref[hg * nh + hh] * LOG2E
        base = jnp.where(crow < 3, slope * tile, jnp.where(crow < 6, slope, 0.0))
        part0 = _bf16_round(base)
        part1 = _bf16_round(base - part0)
        part2 = base - part0 - part1
        slope_rows = jnp.where((crow == 0) | (crow == 3), part0,
                               jnp.where((crow == 1) | (crow == 4), part1, part2))
        bias_rows = jnp.concatenate([no_choice, slope_rows], axis=0).astype(jnp.bfloat16)
        for sel in range(2):
            q_rows = q_refs[sel][0, head_cols(hh), :]
            first_half = own_half(hh).start == 0
            qaug_ref[sel, hh] = jnp.concatenate([q_rows, bias_rows] if first_half else [bias_rows, q_rows], axis=0)

    gate_parts = {}
    group_lane = lax.broadcasted_iota(jnp.int32, (n_blk, LANES), 1)
    for group in range(nh // heads_per_group):
        heads = range(group * heads_per_group, (group + 1) * heads_per_group)
        kmean = kmean_ref[:, group * LANES:(group + 1) * LANES]
        gate_lhs = []
        for hh in heads:
            in_own_half = (group_lane >= own_half(hh).start) & (group_lane < own_half(hh).stop)
            kmean_h = jnp.where(in_own_half, kmean, 0.0)
            kmean_hi = kmean_h.astype(jnp.bfloat16)
            gate_lhs += [kmean_hi, (kmean_h - kmean_hi.astype(jnp.float32)).astype(jnp.bfloat16)]
        gate_lhs = jnp.concatenate(gate_lhs, axis=0)
        for sel in range(2):
            g = _dot(gate_lhs, q_refs[sel][0, group * LANES:(group + 1) * LANES, :])
            for c, hh in enumerate(heads):
                gate_parts[sel, hh] = g[2 * c * n_blk:(2 * c + 1) * n_blk] + g[(2 * c + 1) * n_blk:(2 * c + 2) * n_blk]

    def store_block_choice(heads):
        pairs = [(sel, hh) for sel in range(2) for hh in heads]
        gates = jnp.concatenate([gate_parts[pair] for pair in pairs], axis=1)
        blk = lax.broadcasted_iota(jnp.int32, gates.shape, 0).astype(jnp.float32)
        lane = lax.broadcasted_iota(jnp.int32, gates.shape, 1)
        n_past = jnp.where(lane < len(heads) * tile, q_tile[0], q_tile[1]).astype(jnp.float32)
        gate = jnp.where(blk < n_past, gates, -jnp.inf)
        chosen = blk == n_past
        for _ in range(MOBA_TOPK):
            top = jnp.max(gate, axis=0, keepdims=True)
            at_top = (gate == top) & (top > -jnp.inf)
            first = jnp.min(jnp.where(at_top, blk, float(n_blk)), axis=0, keepdims=True)
            pick = blk == first
            chosen = jnp.logical_or(chosen, pick)
            gate = jnp.where(pick, -jnp.inf, gate)
        choice = jnp.where(chosen, 0.0, MASKED).astype(jnp.bfloat16)
        for c, (sel, hh) in enumerate(pairs):
            first_bias_row = HEAD_DIM - own_half(hh).start
            qaug_ref[sel, hh, first_bias_row:first_bias_row + n_blk] = choice[:, c * tile:(c + 1) * tile]

    def key_block(j, hh):
        return kaug_ref[hh, pl.ds(pl.multiple_of(j * tile, tile), tile), :]

    def slot_of(s):
        is_b = s >= p
        return is_b.astype(jnp.int32), jnp.where(is_b, s - p, s)

    causal = key_pos <= qry_pos
    ones_rows = jnp.ones((BF16_ROWS, tile), jnp.bfloat16)

    def weighted_values(j, hh, prob):
        v_aug = jnp.concatenate([vt_ref[0, j, hh * HEAD_DIM:(hh + 1) * HEAD_DIM, :], ones_rows], axis=0)
        return _dot(v_aug, prob.astype(jnp.bfloat16))

    def finish(sel, heads):
        o_ref = (olo_ref, ohi_ref)[sel]
        cols = slice(heads[0] * HEAD_DIM, (heads[-1] + 1) * HEAD_DIM)
        _store_heads(o_ref.at[:, :, cols],
                     [acc_ref[sel, hh, :HEAD_DIM] * (1.0 / acc_ref[sel, hh, HEAD_DIM:HEAD_DIM + 1]) for hh in heads])

    group_items = [(True, sel) for sel in range(2)] + [(False, s) for s in range(n_blk - 1)]
    first_tile_done = (False, n_blk // 2 - 2) if n_blk >= 4 else (True, 1)
    items = [(group, own, idx) for group in range(nh // heads_per_group) for own, idx in group_items]
    in_flight = {}
    for step in range(len(items) + SOFTMAX_LAG):
        if step < len(items):
            group, own, idx = items[step]
            heads = list(range(group * heads_per_group, (group + 1) * heads_per_group))
            if (own, idx) == group_items[2]:
                store_block_choice(heads)
            sel, j = (idx, q_tile[idx]) if own else slot_of(idx)
            logits = [_dot(key_block(j, hh), qaug_ref[sel, hh]) for hh in heads]
            if own:
                logits = [jnp.where(causal, lg, MASKED) for lg in logits]
            in_flight[step] = (heads, own, sel, j, logits)
        done = step - SOFTMAX_LAG
        if done >= 0:
            heads, own, sel, j, logits = in_flight.pop(done)
            for hh, lg in zip(heads, logits):
                top = jnp.max(lg, axis=0, keepdims=True)
                if own:
                    m_ref[sel, hh] = top
                    acc_ref[sel, hh] = weighted_values(j, hh, jnp.exp2(lg - top))
                else:
                    m_old = m_ref[sel, hh]
                    m_new = jnp.maximum(m_old, top)
                    m_ref[sel, hh] = m_new
                    acc_ref[sel, hh] = (jnp.exp2(m_old - m_new) * acc_ref[sel, hh]
                                        + weighted_values(j, hh, jnp.exp2(lg - m_new)))
            if items[done][1:] == first_tile_done:
                finish(0, heads)
            if items[done][1:] == group_items[-1]:
                finish(1, heads)


def _moba_attention(qt, k, vt, slopes):
    b, d, s = qt.shape
    nh = MOBA_HEADS
    gw = nh * HEAD_DIM
    n_blk = s // ATT_TILE
    assert gw % LANES == 0 and n_blk % 2 == 0 and PROJ_ROWS == 2 * ATT_TILE and HEAD_DIM >= n_blk + 6
    half = jax.ShapeDtypeStruct((b, s // 2, d), jnp.bfloat16)
    return pl.pallas_call(
        _moba_kernel,
        grid=(b, d // gw, n_blk // 2),
        in_specs=[
            pl.BlockSpec(memory_space=pltpu.SMEM),
            pl.BlockSpec((s, LANES), lambda i, j, t: (0, 0)),
            pl.BlockSpec((1, gw, ATT_TILE), lambda i, j, t: (i, j, t)),
            pl.BlockSpec((1, gw, ATT_TILE), lambda i, j, t: (i, j, n_blk - 1 - t)),
            pl.BlockSpec((1, s, gw), lambda i, j, t: (i, 0, j)),
            pl.BlockSpec((1, n_blk, gw, ATT_TILE), lambda i, j, t: (i, 0, j, 0)),
        ],
        out_specs=[
            pl.BlockSpec((1, ATT_TILE, gw), lambda i, j, t: (i, t, j)),
            pl.BlockSpec((1, ATT_TILE, gw), lambda i, j, t: (i, lax.bitwise_xor(t, 1), j)),
        ],
        out_shape=[half, half],
        scratch_shapes=[
            pltpu.VMEM((n_blk, gw), jnp.float32),
            pltpu.VMEM((nh, s, LANES), jnp.bfloat16),
            pltpu.VMEM((2, nh, LANES, ATT_TILE), jnp.bfloat16),
            pltpu.VMEM((2, nh, 1, ATT_TILE), jnp.float32),
            pltpu.VMEM((2, nh, HEAD_DIM + BF16_ROWS, ATT_TILE), jnp.float32),
        ],
        compiler_params=pltpu.CompilerParams(
            dimension_semantics=("arbitrary", "arbitrary", "arbitrary"),
            vmem_limit_bytes=VMEM_LIMIT),
        name="moba_attention",
    )(slopes, _moba_key_features(n_blk), qt, qt, k, vt)


def _softplus2(z2):
    return jnp.maximum(z2, jnp.log(1.0 + jnp.exp2(jnp.minimum(z2, EXP2_ARG_MAX))) * LOG2E)


def _sb_kernel(later_ref, qt_ref, k_ref, vt_ref, o_ref, suffix_ref, acc_ref, *, nh, nt):
    first_tile = pl.program_id(2) * nt
    tile = ATT_TILE
    heads_per_group = LANES // HEAD_DIM

    def lane_group(hh):
        first = hh // heads_per_group * LANES
        return slice(first, first + LANES)

    qm = {(ti, hh): _head_masked(qt_ref[0, lane_group(hh), ti * tile:(ti + 1) * tile], hh % heads_per_group)
          for ti in range(nt) for hh in range(nh)}
    key_pos = lax.broadcasted_iota(jnp.int32, (tile, tile), 0)
    qry_pos = lax.broadcasted_iota(jnp.int32, (tile, tile), 1)
    strict = key_pos < qry_pos

    def key_block(j, hh):
        return k_ref[0, pl.ds(pl.multiple_of(j * tile, tile), tile), lane_group(hh)]

    def later_sums(sp2):
        return _dot(later_ref[...], sp2.astype(jnp.bfloat16))

    def values(j, hh):
        return vt_ref[0, j, hh * HEAD_DIM:(hh + 1) * HEAD_DIM, :]

    streams = [(ti, hh) for ti in range(nt) for hh in range(nh)]
    units = ([(ti, hh, first_tile + ti, True) for ti, hh in streams]
             + [(ti, hh, jnp.maximum(first_tile + ti - 1, 0), False) for ti, hh in streams])
    half = tile // 2

    def live_parts(x):
        return x[:half], x[half:, half:]

    def full_tile(parts):
        first, last = parts
        return jnp.concatenate([first, jnp.concatenate([jnp.zeros_like(last), last], axis=1)], axis=0)

    strict_parts = live_parts(strict)
    z2s, sp2s, inners, weights, totals, outs = {}, {}, {}, {}, {}, {}
    lag_softplus, lag_sums, lag_weights, lag_values = SB_STAGE_LAGS
    for step in range(len(units) + lag_values):
        if step < len(units):
            ti, hh, j, own = units[step]
            z2 = _dot(key_block(j, hh), qm[ti, hh])
            if own:
                z2s[step] = [jnp.where(m, part, MASKED) for m, part in zip(strict_parts, live_parts(z2))]
            else:
                z2s[step] = [z2]
        u = step - lag_softplus
        if 0 <= u < len(units):
            sp2s[u] = [_softplus2(part) for part in z2s[u]]
        u = step - lag_sums
        if 0 <= u < len(units):
            sp2_bf = [part.astype(jnp.bfloat16) for part in sp2s[u]]
            inners[u] = later_sums(full_tile(sp2_bf) if units[u][3] else sp2_bf[0])
        u = step - lag_weights
        if 0 <= u < len(units):
            ti, _, _, own = units[u]
            totals[u] = inners[u][0:1] + sp2s[u][0][0:1]
            inner_parts = live_parts(inners[u]) if own else [inners[u]]
            expo = [z2 - sp2 - inner for z2, sp2, inner in zip(z2s[u], sp2s[u], inner_parts)]
            if own:
                weights[u] = full_tile([jnp.exp2(part).astype(jnp.bfloat16) for part in expo])
            else:
                own_total = totals[u - len(streams)]
                suffix = jnp.where(first_tile + ti > 0, own_total, -MASKED)
                weights[u] = jnp.exp2(expo[0] - suffix).astype(jnp.bfloat16)
        u = step - lag_values
        if 0 <= u < len(units):
            _, hh, j, _ = units[u]
            outs[u] = _dot(values(j, hh), weights[u])
    for c, (ti, hh) in enumerate(streams):
        acc_ref[ti, hh] = outs[c] + outs[len(streams) + c]
        suffix_ref[ti, hh] = totals[c] + totals[len(streams) + c]

    for ti in range(nt):
        def smallest_suffix(ti=ti):
            return functools.reduce(jnp.minimum, [jnp.min(suffix_ref[ti, hh]) for hh in range(nh)])

        def more_to_do(carry):
            j, smallest = carry
            return jnp.logical_and(j >= 0, smallest <= SB_SUFFIX_CUTOFF)

        def farther_block(carry, ti=ti, smallest_suffix=smallest_suffix):
            j, _ = carry
            for hh in range(nh):
                z2 = _dot(key_block(j, hh), qm[ti, hh])
                sp2 = _softplus2(z2)
                inner = later_sums(sp2)
                w = jnp.exp2(z2 - sp2 - inner - suffix_ref[ti, hh])
                acc_ref[ti, hh] += _dot(values(j, hh), w.astype(jnp.bfloat16))
                suffix_ref[ti, hh] += inner[0:1] + sp2[0:1]
            return j - 1, smallest_suffix()

        lax.while_loop(more_to_do, farther_block, (first_tile + ti - 2, smallest_suffix()))
        _store_heads(o_ref.at[:, ti * tile:(ti + 1) * tile], [acc_ref[ti, hh] for hh in range(nh)])


def _sb_attention(qt, k, vt):
    b, d, s = qt.shape
    nh = SB_HEADS
    nt = SB_TILES
    gw = nh * HEAD_DIM
    n_blk = s // ATT_TILE
    assert n_blk % nt == 0
    later = jnp.asarray(np.triu(np.ones((ATT_TILE, ATT_TILE), np.float32), 1), jnp.bfloat16)
    return pl.pallas_call(
        functools.partial(_sb_kernel, nh=nh, nt=nt),
        grid=(b, d // gw, n_blk // nt),
        in_specs=[
            pl.BlockSpec((ATT_TILE, ATT_TILE), lambda i, j, t: (0, 0)),
            pl.BlockSpec((1, gw, nt * ATT_TILE), lambda i, j, t: (i, j, t)),
            pl.BlockSpec((1, s, gw), lambda i, j, t: (i, 0, j)),
            pl.BlockSpec((1, n_blk, gw, ATT_TILE), lambda i, j, t: (i, 0, j, 0)),
        ],
        out_specs=pl.BlockSpec((1, nt * ATT_TILE, gw), lambda i, j, t: (i, t, j)),
        out_shape=jax.ShapeDtypeStruct((b, s, d), jnp.bfloat16),
        scratch_shapes=[
            pltpu.VMEM((nt, nh, 1, ATT_TILE), jnp.float32),
            pltpu.VMEM((nt, nh, HEAD_DIM, ATT_TILE), jnp.float32),
        ],
        compiler_params=pltpu.CompilerParams(
            dimension_semantics=("arbitrary", "arbitrary", "arbitrary"),
            vmem_limit_bytes=VMEM_LIMIT),
        name="stick_breaking_attention",
    )(later, qt, k, vt)


def kernel(x, norm_g, w_in, w_out, final_g):
    b, s, d = x.shape
    depth = norm_g.shape[0]
    assert depth >= 1 and d == N_HEADS * HEAD_DIM and s % PROJ_ROWS == 0 and PROJ_ROWS % ATT_TILE == 0
    assert s % WIDE_ROWS == 0 and WIDE_ROWS % ATT_TILE == 0
    slopes = jnp.asarray(2.0 ** (-8.0 * np.arange(1, N_HEADS + 1) / N_HEADS), jnp.float32)

    w_out_bf = w_out.astype(jnp.bfloat16)

    def proj_weights(layer):
        return w_in, layer, layer + 1 < depth

    h = x
    outs = _layer_boundary(h, norm_g[0], proj=proj_weights(0), rows=WIDE_ROWS)
    for i in range(depth):
        last = i + 1 == depth
        k, qt, vt = outs[0], outs[-2], outs[-1]
        gate = (norm_g[i], w_in) if last else outs[1]
        o_parts = _moba_attention(qt, k, vt, slopes) if i % 2 == 0 else [_sb_attention(qt, k, vt)]
        attn = (o_parts, gate, w_out_bf, i)
        if last:
            rows = WIDE_ROWS if len(o_parts) == 1 else PROJ_ROWS
            (h,) = _layer_boundary(h, final_g, attn=attn, rows=rows)
        else:
            h, *outs = _layer_boundary(h, norm_g[i + 1], attn=attn, proj=proj_weights(i + 1))
    return h
```

```python
import functools
import math

import numpy as np
import jax
import jax.numpy as jnp
from jax import lax
from jax.experimental import pallas as pl
from jax.experimental.pallas import tpu as pltpu

N_HEADS = 16
HEAD_DIM = 64
MOBA_BLOCK = 256
MOBA_TOPK = 3
NORM_EPS = 1e-6

LANES = 128
BF16_ROWS = 16
ATT_TILE = MOBA_BLOCK
PROJ_ROWS = 512
WIDE_ROWS = 1024
MOBA_HEADS = 8
SB_HEADS = 4
SB_TILES = 8
SOFTMAX_LAG = 2
SB_STAGE_LAGS = (1, 2, 3, 4)
LOG2E = math.log2(math.e)
MASKED = -1e30
EXP2_ARG_MAX = 126.0
SB_NEAR_KEYS = 192
SB_SUFFIX_CUTOFF = 152.0
VMEM_LIMIT = 56 * 1024 * 1024


def _dot(a, b):
    return jnp.dot(a, b, preferred_element_type=jnp.float32)


def _dot_nt(a, b):
    return lax.dot_general(a, b, (((1,), (1,)), ((), ())), preferred_element_type=jnp.float32)


def _bf16_round(x):
    return x.astype(jnp.bfloat16).astype(jnp.float32)


def _rms_norm(h, g_ref):
    return h * lax.rsqrt(jnp.mean(h * h, axis=-1, keepdims=True) + NORM_EPS) * g_ref[...]


def _boundary_kernel(*refs, n_o, half_steps, gate_in, has_proj, emit_z, d, scale):
    refs = list(refs)
    if has_proj:
        wvt_ref, wqt_ref = refs.pop(), refs.pop()
    if n_o:
        o_refs = [refs.pop(0) for _ in range(n_o)]
        if gate_in:
            z_ref = refs.pop(0)
        else:
            gate_gain_ref, gate_w_ref = refs.pop(0), refs.pop(0)
        h_ref, wout_ref = refs.pop(0), refs.pop(0)
    else:
        h_ref = refs.pop(0)
    g_ref = refs.pop(0)
    if has_proj:
        wk_ref = refs.pop(0)
        wz_ref = refs.pop(0) if emit_z else None
        wq_ref, wv_ref = refs.pop(0), refs.pop(0)

        @pl.when((pl.program_id(0) == 0) & (pl.program_id(1) == 0))
        def _():
            wqt_ref[...] = wq_ref[0].T.astype(wqt_ref.dtype)
            wvt_ref[...] = wv_ref[0].T.astype(wvt_ref.dtype)
    if n_o:
        hout_ref = refs.pop(0)

    h = h_ref[0]
    if n_o:
        if n_o == 1:
            o = o_refs[0][0]
        else:
            o = jnp.where(pl.program_id(1) < half_steps, o_refs[0][0], o_refs[1][0])
        if gate_in:
            z = z_ref[0].astype(jnp.float32)
        else:
            z = _dot(_rms_norm(h, gate_gain_ref).astype(jnp.bfloat16), gate_w_ref[0].astype(jnp.bfloat16))
        gated = o.astype(jnp.float32) * (z / (1.0 + jnp.exp(-z)))
        h = h + _dot(gated.astype(jnp.bfloat16), wout_ref[0])
    y = _rms_norm(h, g_ref)
    if n_o:
        hout_ref[0] = h if has_proj else y
    if has_proj:
        k_ref = refs.pop(0)
        znext_ref = refs.pop(0) if emit_z else None
        qt_ref, vt_ref = refs
        xn = y.astype(jnp.bfloat16)
        k_ref[0] = _dot(xn, wk_ref[0].astype(jnp.bfloat16)).astype(k_ref.dtype)
        if emit_z:
            znext_ref[0] = _dot(xn, wz_ref[0].astype(jnp.bfloat16)).astype(znext_ref.dtype)
        qt_ref[0] = (_dot_nt(wqt_ref[...], xn) * scale).astype(qt_ref.dtype)
        v_t = _dot_nt(wvt_ref[...], xn)
        for c in range(vt_ref.shape[1]):
            vt_ref[0, c] = v_t[:, c * ATT_TILE:(c + 1) * ATT_TILE].astype(vt_ref.dtype)


def _layer_boundary(h, gain, attn=None, proj=None, rows=PROJ_ROWS):
    b, s, d = h.shape
    n_steps = s // rows
    half = n_steps // 2
    n_blk = s // ATT_TILE
    row_spec = pl.BlockSpec((1, rows, d), lambda i, t: (i, t, 0))

    def const_spec(shape):
        return pl.BlockSpec(shape, lambda i, t: (0,) * len(shape), pipeline_mode=pl.Buffered(1))

    def weight_spec(layer, col_block):
        return pl.BlockSpec((1, d, d), lambda i, t: (layer, 0, col_block), pipeline_mode=pl.Buffered(1))

    in_specs, args, out_specs, out_shape = [], [], [], []
    n_o = 0
    gate_in = True
    if attn is not None:
        o_parts, gate, w_out, out_layer = attn
        n_o = len(o_parts)
        if n_o == 1:
            in_specs.append(row_spec)
        else:
            assert rows == PROJ_ROWS
            in_specs += [
                pl.BlockSpec((1, rows, d), lambda i, t: (i, jnp.minimum(t, half - 1), 0)),
                pl.BlockSpec((1, rows, d), lambda i, t: (i, jnp.clip(n_steps - 1 - t, 0, half - 1), 0)),
            ]
        args += list(o_parts)
        gate_in = not isinstance(gate, tuple)
        if gate_in:
            in_specs.append(row_spec)
            args.append(gate)
        else:
            gate_gain, w_in_all = gate
            in_specs += [const_spec((1, d)), weight_spec(out_layer, 3)]
            args += [gate_gain.reshape(1, d), w_in_all]
        in_specs += [row_spec, weight_spec(out_layer, 0)]
        args += [h, w_out]
        out_specs.append(row_spec)
        out_shape.append(jax.ShapeDtypeStruct((b, s, d), jnp.float32))
    else:
        in_specs.append(row_spec)
        args.append(h)
    in_specs.append(const_spec((1, d)))
    args.append(gain.reshape(1, d))
    emit_z = False
    scratch_shapes = []
    if proj is not None:
        w_in, layer, emit_z = proj
        token_major = (pl.BlockSpec((1, rows, d), lambda i, t: (i, t, 0)),
                       jax.ShapeDtypeStruct((b, s, d), jnp.bfloat16))
        in_specs += [weight_spec(layer, 1)] + ([weight_spec(layer, 3)] if emit_z else [])
        in_specs += [weight_spec(layer, 0), weight_spec(layer, 2)]
        args += [w_in] * (4 if emit_z else 3)
        scratch_shapes = [pltpu.VMEM((d, d), jnp.bfloat16)] * 2
        outputs = [token_major] + ([token_major] if emit_z else []) + [
            (pl.BlockSpec((1, d, rows), lambda i, t: (i, 0, t)),
             jax.ShapeDtypeStruct((b, d, s), jnp.bfloat16)),
            (pl.BlockSpec((1, rows // ATT_TILE, d, ATT_TILE), lambda i, t: (i, t, 0, 0)),
             jax.ShapeDtypeStruct((b, n_blk, d, ATT_TILE), jnp.bfloat16)),
        ]
        out_specs += [spec for spec, _ in outputs]
        out_shape += [shape for _, shape in outputs]
    kern = functools.partial(_boundary_kernel, n_o=n_o, half_steps=half, gate_in=gate_in,
                             has_proj=proj is not None, emit_z=emit_z, d=d, scale=HEAD_DIM ** -0.5 * LOG2E)
    name = ("gate_outproj_" if attn is not None else "") + ("rmsnorm_qkvz_proj" if proj is not None else "final_rmsnorm")
    return pl.pallas_call(
        kern,
        grid=(b, n_steps),
        in_specs=in_specs,
        out_specs=out_specs,
        out_shape=out_shape,
        scratch_shapes=scratch_shapes,
        compiler_params=pltpu.CompilerParams(
            dimension_semantics=("arbitrary", "arbitrary"), vmem_limit_bytes=VMEM_LIMIT),
        name=name,
    )(*args)


def _head_masked(qt, hh):
    row = lax.broadcasted_iota(jnp.int32, qt.shape, 0)
    return jnp.where((row >= hh * HEAD_DIM) & (row < (hh + 1) * HEAD_DIM), qt, jnp.zeros_like(qt))


def _store_heads(o_ref, outs_t):
    stacked = jnp.concatenate(outs_t, axis=0)
    o_ref[0] = stacked.T.astype(o_ref.dtype)


def _moba_key_features(n_blk):
    pos = np.arange(n_blk * ATT_TILE)
    feat = np.zeros((n_blk * ATT_TILE, HEAD_DIM), np.float32)
    feat[pos, pos // ATT_TILE] = 1.0
    feat[:, n_blk:n_blk + 3] = (pos // ATT_TILE)[:, None]
    feat[:, n_blk + 3:n_blk + 6] = (pos % ATT_TILE)[:, None]
    return jnp.asarray(np.concatenate([feat, feat], axis=1), jnp.bfloat16)


def _moba_kernel(slopes_ref, kfeat_ref, qa_ref, qb_ref, k_ref, vt_ref, olo_ref, ohi_ref,
                 kmean_ref, kaug_ref, qaug_ref, m_ref, acc_ref):
    nh = MOBA_HEADS
    hg = pl.program_id(1)
    p = pl.program_id(2)
    n_blk = kmean_ref.shape[0]
    tile = ATT_TILE
    q_tile = (p, n_blk - 1 - p)
    q_refs = (qa_ref, qb_ref)
    key_pos = lax.broadcasted_iota(jnp.int32, (tile, tile), 0)
    qry_pos = lax.broadcasted_iota(jnp.int32, (tile, tile), 1)

    heads_per_group = LANES // HEAD_DIM

    def own_half(hh):
        first = hh % heads_per_group * HEAD_DIM
        return slice(first, first + HEAD_DIM)

    def head_cols(hh):
        return slice(hh * HEAD_DIM, (hh + 1) * HEAD_DIM)

    @pl.when(p == 0)
    def _():
        k_all = k_ref[0]
        kmean_ref[...] = jnp.mean(k_all.astype(jnp.float32).reshape(n_blk, tile, nh * HEAD_DIM), axis=1)
        for hh in range(nh):
            kaug_ref[hh, :, own_half(hh)] = k_all[:, head_cols(hh)]

    @pl.when((pl.program_id(0) == 0) & (hg == 0) & (p == 0))
    def _():
        for hh in range(nh):
            first = HEAD_DIM - own_half(hh).start
            kaug_ref[hh, :, first:first + HEAD_DIM] = kfeat_ref[:, first:first + HEAD_DIM]

    crow = lax.broadcasted_iota(jnp.int32, (HEAD_DIM - n_blk, tile), 0)
    no_choice = jnp.zeros((n_blk, tile), jnp.float32)
    for hh in range(nh):
        slope = slopes_ref[hg * nh + hh] * LOG2E
        base = jnp.where(crow < 3, slope * tile, jnp.where(crow < 6, slope, 0.0))
        part0 = _bf16_round(base)
        part1 = _bf16_round(base - part0)
        part2 = base - part0 - part1
        slope_rows = jnp.where((crow == 0) | (crow == 3), part0,
                               jnp.where((crow == 1) | (crow == 4), part1, part2))
        bias_rows = jnp.concatenate([no_choice, slope_rows], axis=0).astype(jnp.bfloat16)
        for sel in range(2):
            q_rows = q_refs[sel][0, head_cols(hh), :]
            first_half = own_half(hh).start == 0
            qaug_ref[sel, hh] = jnp.concatenate([q_rows, bias_rows] if first_half else [bias_rows, q_rows], axis=0)

    gate_parts = {}
    group_lane = lax.broadcasted_iota(jnp.int32, (n_blk, LANES), 1)
    for group in range(nh // heads_per_group):
        heads = range(group * heads_per_group, (group + 1) * heads_per_group)
        kmean = kmean_ref[:, group * LANES:(group + 1) * LANES]
        gate_lhs = []
        for hh in heads:
            in_own_half = (group_lane >= own_half(hh).start) & (group_lane < own_half(hh).stop)
            kmean_h = jnp.where(in_own_half, kmean, 0.0)
            kmean_hi = kmean_h.astype(jnp.bfloat16)
            gate_lhs += [kmean_hi, (kmean_h - kmean_hi.astype(jnp.float32)).astype(jnp.bfloat16)]
        gate_lhs = jnp.concatenate(gate_lhs, axis=0)
        for sel in range(2):
            g = _dot(gate_lhs, q_refs[sel][0, group * LANES:(group + 1) * LANES, :])
            for c, hh in enumerate(heads):
                gate_parts[sel, hh] = g[2 * c * n_blk:(2 * c + 1) * n_blk] + g[(2 * c + 1) * n_blk:(2 * c + 2) * n_blk]

    def store_block_choice(heads):
        pairs = [(sel, hh) for sel in range(2) for hh in heads]
        gates = jnp.concatenate([gate_parts[pair] for pair in pairs], axis=1)
        blk = lax.broadcasted_iota(jnp.int32, gates.shape, 0).astype(jnp.float32)
        lane = lax.broadcasted_iota(jnp.int32, gates.shape, 1)
        n_past = jnp.where(lane < len(heads) * tile, q_tile[0], q_tile[1]).astype(jnp.float32)
        gate = jnp.where(blk < n_past, gates, -jnp.inf)
        chosen = blk == n_past
        for _ in range(MOBA_TOPK):
            top = jnp.max(gate, axis=0, keepdims=True)
            at_top = (gate == top) & (top > -jnp.inf)
            first = jnp.min(jnp.where(at_top, blk, float(n_blk)), axis=0, keepdims=True)
            pick = blk == first
            chosen = jnp.logical_or(chosen, pick)
            gate = jnp.where(pick, -jnp.inf, gate)
        choice = jnp.where(chosen, 0.0, MASKED).astype(jnp.bfloat16)
        for c, (sel, hh) in enumerate(pairs):
            first_bias_row = HEAD_DIM - own_half(hh).start
            qaug_ref[sel, hh, first_bias_row:first_bias_row + n_blk] = choice[:, c * tile:(c + 1) * tile]

    def key_block(j, hh):
        return kaug_ref[hh, pl.ds(pl.multiple_of(j * tile, tile), tile), :]

    def slot_of(s):
        is_b = s >= p
        return is_b.astype(jnp.int32), jnp.where(is_b, s - p, s)

    causal = key_pos <= qry_pos
    ones_rows = jnp.ones((BF16_ROWS, tile), jnp.bfloat16)

    def weighted_values(j, hh, prob):
        v_aug = jnp.concatenate([vt_ref[0, j, hh * HEAD_DIM:(hh + 1) * HEAD_DIM, :], ones_rows], axis=0)
        return _dot(v_aug, prob.astype(jnp.bfloat16))

    def finish(sel, heads):
        o_ref = (olo_ref, ohi_ref)[sel]
        cols = slice(heads[0] * HEAD_DIM, (heads[-1] + 1) * HEAD_DIM)
        _store_heads(o_ref.at[:, :, cols],
                     [acc_ref[sel, hh, :HEAD_DIM] * (1.0 / acc_ref[sel, hh, HEAD_DIM:HEAD_DIM + 1]) for hh in heads])

    group_items = [(True, sel) for sel in range(2)] + [(False, s) for s in range(n_blk - 1)]
    first_tile_done = (False, n_blk // 2 - 2) if n_blk >= 4 else (True, 1)
    items = [(group, own, idx) for group in range(nh // heads_per_group) for own, idx in group_items]
    in_flight = {}
    for step in range(len(items) + SOFTMAX_LAG):
        if step < len(items):
            group, own, idx = items[step]
            heads = list(range(group * heads_per_group, (group + 1) * heads_per_group))
            if (own, idx) == group_items[2]:
                store_block_choice(heads)
            sel, j = (idx, q_tile[idx]) if own else slot_of(idx)
            logits = [_dot(key_block(j, hh), qaug_ref[sel, hh]) for hh in heads]
            if own:
                logits = [jnp.where(causal, lg, MASKED) for lg in logits]
            in_flight[step] = (heads, own, sel, j, logits)
        done = step - SOFTMAX_LAG
        if done >= 0:
            heads, own, sel, j, logits = in_flight.pop(done)
            for hh, lg in zip(heads, logits):
                top = jnp.max(lg, axis=0, keepdims=True)
                if own:
                    m_ref[sel, hh] = top
                    acc_ref[sel, hh] = weighted_values(j, hh, jnp.exp2(lg - top))
                else:
                    m_old = m_ref[sel, hh]
                    m_new = jnp.maximum(m_old, top)
                    m_ref[sel, hh] = m_new
                    acc_ref[sel, hh] = (jnp.exp2(m_old - m_new) * acc_ref[sel, hh]
                                        + weighted_values(j, hh, jnp.exp2(lg - m_new)))
            if items[done][1:] == first_tile_done:
                finish(0, heads)
            if items[done][1:] == group_items[-1]:
                finish(1, heads)


def _moba_attention(qt, k, vt, slopes):
    b, d, s = qt.shape
    nh = MOBA_HEADS
    gw = nh * HEAD_DIM
    n_blk = s // ATT_TILE
    assert gw % LANES == 0 and n_blk % 2 == 0 and PROJ_ROWS == 2 * ATT_TILE and HEAD_DIM >= n_blk + 6
    half = jax.ShapeDtypeStruct((b, s // 2, d), jnp.bfloat16)
    return pl.pallas_call(
        _moba_kernel,
        grid=(b, d // gw, n_blk // 2),
        in_specs=[
            pl.BlockSpec(memory_space=pltpu.SMEM),
            pl.BlockSpec((s, LANES), lambda i, j, t: (0, 0)),
            pl.BlockSpec((1, gw, ATT_TILE), lambda i, j, t: (i, j, t)),
            pl.BlockSpec((1, gw, ATT_TILE), lambda i, j, t: (i, j, n_blk - 1 - t)),
            pl.BlockSpec((1, s, gw), lambda i, j, t: (i, 0, j)),
            pl.BlockSpec((1, n_blk, gw, ATT_TILE), lambda i, j, t: (i, 0, j, 0)),
        ],
        out_specs=[
            pl.BlockSpec((1, ATT_TILE, gw), lambda i, j, t: (i, t, j)),
            pl.BlockSpec((1, ATT_TILE, gw), lambda i, j, t: (i, lax.bitwise_xor(t, 1), j)),
        ],
        out_shape=[half, half],
        scratch_shapes=[
            pltpu.VMEM((n_blk, gw), jnp.float32),
            pltpu.VMEM((nh, s, LANES), jnp.bfloat16),
            pltpu.VMEM((2, nh, LANES, ATT_TILE), jnp.bfloat16),
            pltpu.VMEM((2, nh, 1, ATT_TILE), jnp.float32),
            pltpu.VMEM((2, nh, HEAD_DIM + BF16_ROWS, ATT_TILE), jnp.float32),
        ],
        compiler_params=pltpu.CompilerParams(
            dimension_semantics=("arbitrary", "arbitrary", "arbitrary"),
            vmem_limit_bytes=VMEM_LIMIT),
        name="moba_attention",
    )(slopes, _moba_key_features(n_blk), qt, qt, k, vt)


def _softplus2(z2):
    return jnp.maximum(z2, jnp.log(1.0 + jnp.exp2(jnp.minimum(z2, EXP2_ARG_MAX))) * LOG2E)


def _sb_kernel(later_ref, later_near_ref, qt_ref, k_ref, vt_ref, o_ref, suffix_ref, acc_ref, *, nh, nt):
    first_tile = pl.program_id(2) * nt
    tile = ATT_TILE
    near = SB_NEAR_KEYS
    far = tile - near
    heads_per_group = LANES // HEAD_DIM

    def lane_group(hh):
        first = hh // heads_per_group * LANES
        return slice(first, first + LANES)

    qm = {(ti, hh): _head_masked(qt_ref[0, lane_group(hh), ti * tile:(ti + 1) * tile], hh % heads_per_group)
          for ti in range(nt) for hh in range(nh)}
    key_pos = lax.broadcasted_iota(jnp.int32, (tile, tile), 0)
    qry_pos = lax.broadcasted_iota(jnp.int32, (tile, tile), 1)
    strict = key_pos < qry_pos

    def key_block(j, hh):
        return k_ref[0, pl.ds(pl.multiple_of(j * tile, tile), tile), lane_group(hh)]

    def near_keys(j, hh):
        return k_ref[0, pl.ds(pl.multiple_of(j * tile + far, BF16_ROWS), near), lane_group(hh)]

    def later_sums(sp2):
        later = later_ref if sp2.shape[0] == tile else later_near_ref
        return _dot(later[...], sp2.astype(jnp.bfloat16))

    def values(j, hh):
        return vt_ref[0, j, hh * HEAD_DIM:(hh + 1) * HEAD_DIM, :]

    streams = [(ti, hh) for ti in range(nt) for hh in range(nh)]
    units = ([(ti, hh, first_tile + ti, True) for ti, hh in streams]
             + [(ti, hh, jnp.maximum(first_tile + ti - 1, 0), False) for ti, hh in streams])
    half = tile // 2

    def live_parts(x):
        return x[:half], x[half:, half:]

    def full_tile(parts):
        first, last = parts
        return jnp.concatenate([first, jnp.concatenate([jnp.zeros_like(last), last], axis=1)], axis=0)

    strict_parts = live_parts(strict)
    z2s, sp2s, inners, weights, totals, outs = {}, {}, {}, {}, {}, {}
    lag_softplus, lag_sums, lag_weights, lag_values = SB_STAGE_LAGS
    for step in range(len(units) + lag_values):
        if step < len(units):
            ti, hh, j, own = units[step]
            z2 = _dot(key_block(j, hh) if own else near_keys(j, hh), qm[ti, hh])
            if own:
                z2s[step] = [jnp.where(m, part, MASKED) for m, part in zip(strict_parts, live_parts(z2))]
            else:
                z2s[step] = [z2]
        u = step - lag_softplus
        if 0 <= u < len(units):
            sp2s[u] = [_softplus2(part) for part in z2s[u]]
        u = step - lag_sums
        if 0 <= u < len(units):
            sp2_bf = [part.astype(jnp.bfloat16) for part in sp2s[u]]
            inners[u] = later_sums(full_tile(sp2_bf) if units[u][3] else sp2_bf[0])
        u = step - lag_weights
        if 0 <= u < len(units):
            ti, _, _, own = units[u]
            totals[u] = inners[u][0:1] + sp2s[u][0][0:1]
            inner_parts = live_parts(inners[u]) if own else [inners[u]]
            expo = [z2 - sp2 - inner for z2, sp2, inner in zip(z2s[u], sp2s[u], inner_parts)]
            if own:
                weights[u] = full_tile([jnp.exp2(part).astype(jnp.bfloat16) for part in expo])
            else:
                own_total = totals[u - len(streams)]
                suffix = jnp.where(first_tile + ti > 0, own_total, -MASKED)
                weights[u] = jnp.exp2(expo[0] - suffix).astype(jnp.bfloat16)
        u = step - lag_values
        if 0 <= u < len(units):
            _, hh, j, own = units[u]
            outs[u] = _dot(values(j, hh) if own else values(j, hh)[:, far:], weights[u])
    for c, (ti, hh) in enumerate(streams):
        acc_ref[ti, hh] = outs[c] + outs[len(streams) + c]
        suffix_ref[ti, hh] = totals[c] + totals[len(streams) + c]

    for ti in range(nt):
        def smallest_suffix(ti=ti):
            return functools.reduce(jnp.minimum, [jnp.min(suffix_ref[ti, hh]) for hh in range(nh)])

        def more_to_do(carry):
            j, smallest = carry
            return jnp.logical_and(j >= 0, smallest <= SB_SUFFIX_CUTOFF)

        first_block = first_tile + ti - 1

        def farther_block(carry, ti=ti, smallest_suffix=smallest_suffix, first_block=first_block):
            j, _ = carry
            counted = jnp.logical_and(j == first_block, key_pos >= far)
            for hh in range(nh):
                z2 = jnp.where(counted, MASKED, _dot(key_block(j, hh), qm[ti, hh]))
                sp2 = _softplus2(z2)
                inner = later_sums(sp2)
                w = jnp.exp2(z2 - sp2 - inner - suffix_ref[ti, hh])
                acc_ref[ti, hh] += _dot(values(j, hh), w.astype(jnp.bfloat16))
                suffix_ref[ti, hh] += inner[0:1] + sp2[0:1]
            return j - 1, smallest_suffix()

        lax.while_loop(more_to_do, farther_block, (first_block, smallest_suffix()))
        _store_heads(o_ref.at[:, ti * tile:(ti + 1) * tile], [acc_ref[ti, hh] for hh in range(nh)])


def _sb_attention(qt, k, vt):
    b, d, s = qt.shape
    nh = SB_HEADS
    nt = SB_TILES
    gw = nh * HEAD_DIM
    n_blk = s // ATT_TILE
    assert n_blk % nt == 0
    later = jnp.asarray(np.triu(np.ones((ATT_TILE, ATT_TILE), np.float32), 1), jnp.bfloat16)
    later_near = jnp.asarray(np.triu(np.ones((SB_NEAR_KEYS, SB_NEAR_KEYS), np.float32), 1), jnp.bfloat16)
    return pl.pallas_call(
        functools.partial(_sb_kernel, nh=nh, nt=nt),
        grid=(b, d // gw, n_blk // nt),
        in_specs=[
            pl.BlockSpec((ATT_TILE, ATT_TILE), lambda i, j, t: (0, 0)),
            pl.BlockSpec((SB_NEAR_KEYS, SB_NEAR_KEYS), lambda i, j, t: (0, 0)),
            pl.BlockSpec((1, gw, nt * ATT_TILE), lambda i, j, t: (i, j, t)),
            pl.BlockSpec((1, s, gw), lambda i, j, t: (i, 0, j)),
            pl.BlockSpec((1, n_blk, gw, ATT_TILE), lambda i, j, t: (i, 0, j, 0)),
        ],
        out_specs=pl.BlockSpec((1, nt * ATT_TILE, gw), lambda i, j, t: (i, t, j)),
        out_shape=jax.ShapeDtypeStruct((b, s, d), jnp.bfloat16),
        scratch_shapes=[
            pltpu.VMEM((nt, nh, 1, ATT_TILE), jnp.float32),
            pltpu.VMEM((nt, nh, HEAD_DIM, ATT_TILE), jnp.float32),
        ],
        compiler_params=pltpu.CompilerParams(
            dimension_semantics=("arbitrary", "arbitrary", "arbitrary"),
            vmem_limit_bytes=VMEM_LIMIT),
        name="stick_breaking_attention",
    )(later, later_near, qt, k, vt)


def kernel(x, norm_g, w_in, w_out, final_g):
    b, s, d = x.shape
    depth = norm_g.shape[0]
    assert depth >= 1 and d == N_HEADS * HEAD_DIM and s % PROJ_ROWS == 0 and PROJ_ROWS % ATT_TILE == 0
    assert s % WIDE_ROWS == 0 and WIDE_ROWS % ATT_TILE == 0
    slopes = jnp.asarray(2.0 ** (-8.0 * np.arange(1, N_HEADS + 1) / N_HEADS), jnp.float32)

    w_out_bf = w_out.astype(jnp.bfloat16)

    def proj_weights(layer):
        return w_in, layer, layer + 1 < depth

    h = x
    outs = _layer_boundary(h, norm_g[0], proj=proj_weights(0), rows=WIDE_ROWS)
    for i in range(depth):
        last = i + 1 == depth
        k, qt, vt = outs[0], outs[-2], outs[-1]
        gate = (norm_g[i], w_in) if last else outs[1]
        o_parts = _moba_attention(qt, k, vt, slopes) if i % 2 == 0 else [_sb_attention(qt, k, vt)]
        attn = (o_parts, gate, w_out_bf, i)
        if last:
            rows = WIDE_ROWS if len(o_parts) == 1 else PROJ_ROWS
            (h,) = _layer_boundary(h, final_g, attn=attn, rows=rows)
        else:
            h, *outs = _layer_boundary(h, norm_g[i + 1], attn=attn, proj=proj_weights(i + 1))
    return h
```

```python
import functools
import math

import numpy as np
import jax
import jax.numpy as jnp
from jax import lax
from jax.experimental import pallas as pl
from jax.experimental.pallas import tpu as pltpu

N_HEADS = 16
HEAD_DIM = 64
MOBA_BLOCK = 256
MOBA_TOPK = 3
NORM_EPS = 1e-6

LANES = 128
BF16_ROWS = 16
ATT_TILE = MOBA_BLOCK
PROJ_ROWS = 512
WIDE_ROWS = 1024
MOBA_HEADS = 8
SB_HEADS = 4
SB_TILES = 8
SOFTMAX_LAG = 2
SB_STAGE_LAGS = (1, 2, 3, 4)
LOG2E = math.log2(math.e)
MASKED = -1e30
EXP2_ARG_MAX = 126.0
SB_NEAR_KEYS = 176
SB_SUFFIX_CUTOFF = 152.0
VMEM_LIMIT = 56 * 1024 * 1024


def _dot(a, b):
    return jnp.dot(a, b, preferred_element_type=jnp.float32)


def _dot_nt(a, b):
    return lax.dot_general(a, b, (((1,), (1,)), ((), ())), preferred_element_type=jnp.float32)


def _bf16_round(x):
    return x.astype(jnp.bfloat16).astype(jnp.float32)


def _rms_norm(h, g_ref):
    return h * lax.rsqrt(jnp.mean(h * h, axis=-1, keepdims=True) + NORM_EPS) * g_ref[...]


def _boundary_kernel(*refs, n_o, half_steps, gate_in, has_proj, emit_z, d, scale):
    refs = list(refs)
    if has_proj:
        wvt_ref, wqt_ref = refs.pop(), refs.pop()
    if n_o:
        o_refs = [refs.pop(0) for _ in range(n_o)]
        if gate_in:
            z_ref = refs.pop(0)
        else:
            gate_gain_ref, gate_w_ref = refs.pop(0), refs.pop(0)
        h_ref, wout_ref = refs.pop(0), refs.pop(0)
    else:
        h_ref = refs.pop(0)
    g_ref = refs.pop(0)
    if has_proj:
        wk_ref = refs.pop(0)
        wz_ref = refs.pop(0) if emit_z else None
        wq_ref, wv_ref = refs.pop(0), refs.pop(0)

        @pl.when((pl.program_id(0) == 0) & (pl.program_id(1) == 0))
        def _():
            wqt_ref[...] = wq_ref[0].T.astype(wqt_ref.dtype)
            wvt_ref[...] = wv_ref[0].T.astype(wvt_ref.dtype)
    if n_o:
        hout_ref = refs.pop(0)

    h = h_ref[0]
    if n_o:
        if n_o == 1:
            o = o_refs[0][0]
        else:
            o = jnp.where(pl.program_id(1) < half_steps, o_refs[0][0], o_refs[1][0])
        if gate_in:
            z = z_ref[0].astype(jnp.float32)
        else:
            z = _dot(_rms_norm(h, gate_gain_ref).astype(jnp.bfloat16), gate_w_ref[0].astype(jnp.bfloat16))
        gated = o.astype(jnp.float32) * (z / (1.0 + jnp.exp(-z)))
        h = h + _dot(gated.astype(jnp.bfloat16), wout_ref[0])
    y = _rms_norm(h, g_ref)
    if n_o:
        hout_ref[0] = h if has_proj else y
    if has_proj:
        k_ref = refs.pop(0)
        znext_ref = refs.pop(0) if emit_z else None
        qt_ref, vt_ref = refs
        xn = y.astype(jnp.bfloat16)
        k_ref[0] = _dot(xn, wk_ref[0].astype(jnp.bfloat16)).astype(k_ref.dtype)
        if emit_z:
            znext_ref[0] = _dot(xn, wz_ref[0].astype(jnp.bfloat16)).astype(znext_ref.dtype)
        qt_ref[0] = (_dot_nt(wqt_ref[...], xn) * scale).astype(qt_ref.dtype)
        v_t = _dot_nt(wvt_ref[...], xn)
        for c in range(vt_ref.shape[1]):
            vt_ref[0, c] = v_t[:, c * ATT_TILE:(c + 1) * ATT_TILE].astype(vt_ref.dtype)


def _layer_boundary(h, gain, attn=None, proj=None, rows=PROJ_ROWS):
    b, s, d = h.shape
    n_steps = s // rows
    half = n_steps // 2
    n_blk = s // ATT_TILE
    row_spec = pl.BlockSpec((1, rows, d), lambda i, t: (i, t, 0))

    def const_spec(shape):
        return pl.BlockSpec(shape, lambda i, t: (0,) * len(shape), pipeline_mode=pl.Buffered(1))

    def weight_spec(layer, col_block):
        return pl.BlockSpec((1, d, d), lambda i, t: (layer, 0, col_block), pipeline_mode=pl.Buffered(1))

    in_specs, args, out_specs, out_shape = [], [], [], []
    n_o = 0
    gate_in = True
    if attn is not None:
        o_parts, gate, w_out, out_layer = attn
        n_o = len(o_parts)
        if n_o == 1:
            in_specs.append(row_spec)
        else:
            assert rows == PROJ_ROWS
            in_specs += [
                pl.BlockSpec((1, rows, d), lambda i, t: (i, jnp.minimum(t, half - 1), 0)),
                pl.BlockSpec((1, rows, d), lambda i, t: (i, jnp.clip(n_steps - 1 - t, 0, half - 1), 0)),
            ]
        args += list(o_parts)
        gate_in = not isinstance(gate, tuple)
        if gate_in:
            in_specs.append(row_spec)
            args.append(gate)
        else:
            gate_gain, w_in_all = gate
            in_specs += [const_spec((1, d)), weight_spec(out_layer, 3)]
            args += [gate_gain.reshape(1, d), w_in_all]
        in_specs += [row_spec, weight_spec(out_layer, 0)]
        args += [h, w_out]
        out_specs.append(row_spec)
        out_shape.append(jax.ShapeDtypeStruct((b, s, d), jnp.float32))
    else:
        in_specs.append(row_spec)
        args.append(h)
    in_specs.append(const_spec((1, d)))
    args.append(gain.reshape(1, d))
    emit_z = False
    scratch_shapes = []
    if proj is not None:
        w_in, layer, emit_z = proj
        token_major = (pl.BlockSpec((1, rows, d), lambda i, t: (i, t, 0)),
                       jax.ShapeDtypeStruct((b, s, d), jnp.bfloat16))
        in_specs += [weight_spec(layer, 1)] + ([weight_spec(layer, 3)] if emit_z else [])
        in_specs += [weight_spec(layer, 0), weight_spec(layer, 2)]
        args += [w_in] * (4 if emit_z else 3)
        scratch_shapes = [pltpu.VMEM((d, d), jnp.bfloat16)] * 2
        outputs = [token_major] + ([token_major] if emit_z else []) + [
            (pl.BlockSpec((1, d, rows), lambda i, t: (i, 0, t)),
             jax.ShapeDtypeStruct((b, d, s), jnp.bfloat16)),
            (pl.BlockSpec((1, rows // ATT_TILE, d, ATT_TILE), lambda i, t: (i, t, 0, 0)),
             jax.ShapeDtypeStruct((b, n_blk, d, ATT_TILE), jnp.bfloat16)),
        ]
        out_specs += [spec for spec, _ in outputs]
        out_shape += [shape for _, shape in outputs]
    kern = functools.partial(_boundary_kernel, n_o=n_o, half_steps=half, gate_in=gate_in,
                             has_proj=proj is not None, emit_z=emit_z, d=d, scale=HEAD_DIM ** -0.5 * LOG2E)
    name = ("gate_outproj_" if attn is not None else "") + ("rmsnorm_qkvz_proj" if proj is not None else "final_rmsnorm")
    return pl.pallas_call(
        kern,
        grid=(b, n_steps),
        in_specs=in_specs,
        out_specs=out_specs,
        out_shape=out_shape,
        scratch_shapes=scratch_shapes,
        compiler_params=pltpu.CompilerParams(
            dimension_semantics=("arbitrary", "arbitrary"), vmem_limit_bytes=VMEM_LIMIT),
        name=name,
    )(*args)


def _head_masked(qt, hh):
    row = lax.broadcasted_iota(jnp.int32, qt.shape, 0)
    return jnp.where((row >= hh * HEAD_DIM) & (row < (hh + 1) * HEAD_DIM), qt, jnp.zeros_like(qt))


def _store_heads(o_ref, outs_t):
    stacked = jnp.concatenate(outs_t, axis=0)
    o_ref[0] = stacked.T.astype(o_ref.dtype)


def _moba_key_features(n_blk):
    pos = np.arange(n_blk * ATT_TILE)
    feat = np.zeros((n_blk * ATT_TILE, HEAD_DIM), np.float32)
    feat[pos, pos // ATT_TILE] = 1.0
    feat[:, n_blk:n_blk + 3] = (pos // ATT_TILE)[:, None]
    feat[:, n_blk + 3:n_blk + 6] = (pos % ATT_TILE)[:, None]
    return jnp.asarray(np.concatenate([feat, feat], axis=1), jnp.bfloat16)


def _moba_kernel(slopes_ref, kfeat_ref, qa_ref, qb_ref, k_ref, vt_ref, olo_ref, ohi_ref,
                 kmean_ref, kaug_ref, qaug_ref, m_ref, acc_ref):
    nh = MOBA_HEADS
    hg = pl.program_id(1)
    p = pl.program_id(2)
    n_blk = kmean_ref.shape[0]
    tile = ATT_TILE
    q_tile = (p, n_blk - 1 - p)
    q_refs = (qa_ref, qb_ref)
    key_pos = lax.broadcasted_iota(jnp.int32, (tile, tile), 0)
    qry_pos = lax.broadcasted_iota(jnp.int32, (tile, tile), 1)

    heads_per_group = LANES // HEAD_DIM

    def own_half(hh):
        first = hh % heads_per_group * HEAD_DIM
        return slice(first, first + HEAD_DIM)

    def head_cols(hh):
        return slice(hh * HEAD_DIM, (hh + 1) * HEAD_DIM)

    @pl.when(p == 0)
    def _():
        k_all = k_ref[0]
        kmean_ref[...] = jnp.mean(k_all.astype(jnp.float32).reshape(n_blk, tile, nh * HEAD_DIM), axis=1)
        for hh in range(nh):
            kaug_ref[hh, :, own_half(hh)] = k_all[:, head_cols(hh)]

    @pl.when((pl.program_id(0) == 0) & (hg == 0) & (p == 0))
    def _():
        for hh in range(nh):
            first = HEAD_DIM - own_half(hh).start
            kaug_ref[hh, :, first:first + HEAD_DIM] = kfeat_ref[:, first:first + HEAD_DIM]

    crow = lax.broadcasted_iota(jnp.int32, (HEAD_DIM - n_blk, tile), 0)
    no_choice = jnp.zeros((n_blk, tile), jnp.float32)
    for hh in range(nh):
        slope = slopes_ref[hg * nh + hh] * LOG2E
        base = jnp.where(crow < 3, slope * tile, jnp.where(crow < 6, slope, 0.0))
        part0 = _bf16_round(base)
        part1 = _bf16_round(base - part0)
        part2 = base - part0 - part1
        slope_rows = jnp.where((crow == 0) | (crow == 3), part0,
                               jnp.where((crow == 1) | (crow == 4), part1, part2))
        bias_rows = jnp.concatenate([no_choice, slope_rows], axis=0).astype(jnp.bfloat16)
        for sel in range(2):
            q_rows = q_refs[sel][0, head_cols(hh), :]
            first_half = own_half(hh).start == 0
            qaug_ref[sel, hh] = jnp.concatenate([q_rows, bias_rows] if first_half else [bias_rows, q_rows], axis=0)

    gate_parts = {}
    group_lane = lax.broadcasted_iota(jnp.int32, (n_blk, LANES), 1)
    for group in range(nh // heads_per_group):
        heads = range(group * heads_per_group, (group + 1) * heads_per_group)
        kmean = kmean_ref[:, group * LANES:(group + 1) * LANES]
        gate_lhs = []
        for hh in heads:
            in_own_half = (group_lane >= own_half(hh).start) & (group_lane < own_half(hh).stop)
            kmean_h = jnp.where(in_own_half, kmean, 0.0)
            kmean_hi = kmean_h.astype(jnp.bfloat16)
            gate_lhs += [kmean_hi, (kmean_h - kmean_hi.astype(jnp.float32)).astype(jnp.bfloat16)]
        gate_lhs = jnp.concatenate(gate_lhs, axis=0)
        for sel in range(2):
            g = _dot(gate_lhs, q_refs[sel][0, group * LANES:(group + 1) * LANES, :])
            for c, hh in enumerate(heads):
                gate_parts[sel, hh] = g[2 * c * n_blk:(2 * c + 1) * n_blk] + g[(2 * c + 1) * n_blk:(2 * c + 2) * n_blk]

    def store_block_choice(heads):
        pairs = [(sel, hh) for sel in range(2) for hh in heads]
        gates = jnp.concatenate([gate_parts[pair] for pair in pairs], axis=1)
        blk = lax.broadcasted_iota(jnp.int32, gates.shape, 0).astype(jnp.float32)
        lane = lax.broadcasted_iota(jnp.int32, gates.shape, 1)
        n_past = jnp.where(lane < len(heads) * tile, q_tile[0], q_tile[1]).astype(jnp.float32)
        gate = jnp.where(blk < n_past, gates, -jnp.inf)
        chosen = blk == n_past
        for _ in range(MOBA_TOPK):
            top = jnp.max(gate, axis=0, keepdims=True)
            at_top = (gate == top) & (top > -jnp.inf)
            first = jnp.min(jnp.where(at_top, blk, float(n_blk)), axis=0, keepdims=True)
            pick = blk == first
            chosen = jnp.logical_or(chosen, pick)
            gate = jnp.where(pick, -jnp.inf, gate)
        choice = jnp.where(chosen, 0.0, MASKED).astype(jnp.bfloat16)
        for c, (sel, hh) in enumerate(pairs):
            first_bias_row = HEAD_DIM - own_half(hh).start
            qaug_ref[sel, hh, first_bias_row:first_bias_row + n_blk] = choice[:, c * tile:(c + 1) * tile]

    def key_block(j, hh):
        return kaug_ref[hh, pl.ds(pl.multiple_of(j * tile, tile), tile), :]

    def slot_of(s):
        is_b = s >= p
        return is_b.astype(jnp.int32), jnp.where(is_b, s - p, s)

    causal = key_pos <= qry_pos
    ones_rows = jnp.ones((BF16_ROWS, tile), jnp.bfloat16)

    def weighted_values(j, hh, prob):
        v_aug = jnp.concatenate([vt_ref[0, j, hh * HEAD_DIM:(hh + 1) * HEAD_DIM, :], ones_rows], axis=0)
        return _dot(v_aug, prob.astype(jnp.bfloat16))

    def finish(sel, heads):
        o_ref = (olo_ref, ohi_ref)[sel]
        cols = slice(heads[0] * HEAD_DIM, (heads[-1] + 1) * HEAD_DIM)
        _store_heads(o_ref.at[:, :, cols],
                     [acc_ref[sel, hh, :HEAD_DIM] * (1.0 / acc_ref[sel, hh, HEAD_DIM:HEAD_DIM + 1]) for hh in heads])

    group_items = [(True, sel) for sel in range(2)] + [(False, s) for s in range(n_blk - 1)]
    first_tile_done = (False, n_blk // 2 - 2) if n_blk >= 4 else (True, 1)
    items = [(group, own, idx) for group in range(nh // heads_per_group) for own, idx in group_items]
    in_flight = {}
    for step in range(len(items) + SOFTMAX_LAG):
        if step < len(items):
            group, own, idx = items[step]
            heads = list(range(group * heads_per_group, (group + 1) * heads_per_group))
            if (own, idx) == group_items[2]:
                store_block_choice(heads)
            sel, j = (idx, q_tile[idx]) if own else slot_of(idx)
            logits = [_dot(key_block(j, hh), qaug_ref[sel, hh]) for hh in heads]
            if own:
                logits = [jnp.where(causal, lg, MASKED) for lg in logits]
            in_flight[step] = (heads, own, sel, j, logits)
        done = step - SOFTMAX_LAG
        if done >= 0:
            heads, own, sel, j, logits = in_flight.pop(done)
            for hh, lg in zip(heads, logits):
                top = jnp.max(lg, axis=0, keepdims=True)
                if own:
                    m_ref[sel, hh] = top
                    acc_ref[sel, hh] = weighted_values(j, hh, jnp.exp2(lg - top))
                else:
                    m_old = m_ref[sel, hh]
                    m_new = jnp.maximum(m_old, top)
                    m_ref[sel, hh] = m_new
                    acc_ref[sel, hh] = (jnp.exp2(m_old - m_new) * acc_ref[sel, hh]
                                        + weighted_values(j, hh, jnp.exp2(lg - m_new)))
            if items[done][1:] == first_tile_done:
                finish(0, heads)
            if items[done][1:] == group_items[-1]:
                finish(1, heads)


def _moba_attention(qt, k, vt, slopes):
    b, d, s = qt.shape
    nh = MOBA_HEADS
    gw = nh * HEAD_DIM
    n_blk = s // ATT_TILE
    assert gw % LANES == 0 and n_blk % 2 == 0 and PROJ_ROWS == 2 * ATT_TILE and HEAD_DIM >= n_blk + 6
    half = jax.ShapeDtypeStruct((b, s // 2, d), jnp.bfloat16)
    return pl.pallas_call(
        _moba_kernel,
        grid=(b, d // gw, n_blk // 2),
        in_specs=[
            pl.BlockSpec(memory_space=pltpu.SMEM),
            pl.BlockSpec((s, LANES), lambda i, j, t: (0, 0)),
            pl.BlockSpec((1, gw, ATT_TILE), lambda i, j, t: (i, j, t)),
            pl.BlockSpec((1, gw, ATT_TILE), lambda i, j, t: (i, j, n_blk - 1 - t)),
            pl.BlockSpec((1, s, gw), lambda i, j, t: (i, 0, j)),
            pl.BlockSpec((1, n_blk, gw, ATT_TILE), lambda i, j, t: (i, 0, j, 0)),
        ],
        out_specs=[
            pl.BlockSpec((1, ATT_TILE, gw), lambda i, j, t: (i, t, j)),
            pl.BlockSpec((1, ATT_TILE, gw), lambda i, j, t: (i, lax.bitwise_xor(t, 1), j)),
        ],
        out_shape=[half, half],
        scratch_shapes=[
            pltpu.VMEM((n_blk, gw), jnp.float32),
            pltpu.VMEM((nh, s, LANES), jnp.bfloat16),
            pltpu.VMEM((2, nh, LANES, ATT_TILE), jnp.bfloat16),
            pltpu.VMEM((2, nh, 1, ATT_TILE), jnp.float32),
            pltpu.VMEM((2, nh, HEAD_DIM + BF16_ROWS, ATT_TILE), jnp.float32),
        ],
        compiler_params=pltpu.CompilerParams(
            dimension_semantics=("arbitrary", "arbitrary", "arbitrary"),
            vmem_limit_bytes=VMEM_LIMIT),
        name="moba_attention",
    )(slopes, _moba_key_features(n_blk), qt, qt, k, vt)


def _softplus2(z2):
    return jnp.maximum(z2, jnp.log(1.0 + jnp.exp2(jnp.minimum(z2, EXP2_ARG_MAX))) * LOG2E)


def _sb_kernel(later_ref, later_near_ref, qt_ref, k_ref, vt_ref, o_ref, suffix_ref, acc_ref, *, nh, nt):
    first_tile = pl.program_id(2) * nt
    tile = ATT_TILE
    near = SB_NEAR_KEYS
    far = tile - near
    heads_per_group = LANES // HEAD_DIM

    def lane_group(hh):
        first = hh // heads_per_group * LANES
        return slice(first, first + LANES)

    qm = {(ti, hh): _head_masked(qt_ref[0, lane_group(hh), ti * tile:(ti + 1) * tile], hh % heads_per_group)
          for ti in range(nt) for hh in range(nh)}
    key_pos = lax.broadcasted_iota(jnp.int32, (tile, tile), 0)
    qry_pos = lax.broadcasted_iota(jnp.int32, (tile, tile), 1)
    strict = key_pos < qry_pos

    def key_block(j, hh):
        return k_ref[0, pl.ds(pl.multiple_of(j * tile, tile), tile), lane_group(hh)]

    def near_keys(j, hh):
        return k_ref[0, pl.ds(pl.multiple_of(j * tile + far, BF16_ROWS), near), lane_group(hh)]

    def later_sums(sp2):
        later = later_ref if sp2.shape[0] == tile else later_near_ref
        return _dot(later[...], sp2.astype(jnp.bfloat16))

    def values(j, hh):
        return vt_ref[0, j, hh * HEAD_DIM:(hh + 1) * HEAD_DIM, :]

    streams = [(ti, hh) for ti in range(nt) for hh in range(nh)]
    units = ([(ti, hh, first_tile + ti, True) for ti, hh in streams]
             + [(ti, hh, jnp.maximum(first_tile + ti - 1, 0), False) for ti, hh in streams])
    half = tile // 2

    def live_parts(x):
        return x[:half], x[half:, half:]

    def full_tile(parts):
        first, last = parts
        return jnp.concatenate([first, jnp.concatenate([jnp.zeros_like(last), last], axis=1)], axis=0)

    strict_parts = live_parts(strict)
    z2s, sp2s, inners, weights, totals, outs = {}, {}, {}, {}, {}, {}
    lag_softplus, lag_sums, lag_weights, lag_values = SB_STAGE_LAGS
    for step in range(len(units) + lag_values):
        if step < len(units):
            ti, hh, j, own = units[step]
            z2 = _dot(key_block(j, hh) if own else near_keys(j, hh), qm[ti, hh])
            if own:
                z2s[step] = [jnp.where(m, part, MASKED) for m, part in zip(strict_parts, live_parts(z2))]
            else:
                z2s[step] = [z2]
        u = step - lag_softplus
        if 0 <= u < len(units):
            sp2s[u] = [_softplus2(part) for part in z2s[u]]
        u = step - lag_sums
        if 0 <= u < len(units):
            sp2_bf = [part.astype(jnp.bfloat16) for part in sp2s[u]]
            inners[u] = later_sums(full_tile(sp2_bf) if units[u][3] else sp2_bf[0])
        u = step - lag_weights
        if 0 <= u < len(units):
            ti, _, _, own = units[u]
            totals[u] = inners[u][0:1] + sp2s[u][0][0:1]
            inner_parts = live_parts(inners[u]) if own else [inners[u]]
            expo = [z2 - sp2 - inner for z2, sp2, inner in zip(z2s[u], sp2s[u], inner_parts)]
            if own:
                weights[u] = full_tile([jnp.exp2(part).astype(jnp.bfloat16) for part in expo])
            else:
                own_total = totals[u - len(streams)]
                suffix = jnp.where(first_tile + ti > 0, own_total, -MASKED)
                weights[u] = jnp.exp2(expo[0] - suffix).astype(jnp.bfloat16)
        u = step - lag_values
        if 0 <= u < len(units):
            _, hh, j, own = units[u]
            outs[u] = _dot(values(j, hh) if own else values(j, hh)[:, far:], weights[u])
    for c, (ti, hh) in enumerate(streams):
        acc_ref[ti, hh] = outs[c] + outs[len(streams) + c]
        suffix_ref[ti, hh] = totals[c] + totals[len(streams) + c]

    for ti in range(nt):
        def smallest_suffix(ti=ti):
            return functools.reduce(jnp.minimum, [jnp.min(suffix_ref[ti, hh]) for hh in range(nh)])

        def more_to_do(carry):
            j, smallest = carry
            return jnp.logical_and(j >= 0, smallest <= SB_SUFFIX_CUTOFF)

        first_block = first_tile + ti - 1

        def farther_block(carry, ti=ti, smallest_suffix=smallest_suffix, first_block=first_block):
            j, _ = carry
            counted = jnp.logical_and(j == first_block, key_pos >= far)
            for hh in range(nh):
                z2 = jnp.where(counted, MASKED, _dot(key_block(j, hh), qm[ti, hh]))
                sp2 = _softplus2(z2)
                inner = later_sums(sp2)
                w = jnp.exp2(z2 - sp2 - inner - suffix_ref[ti, hh])
                acc_ref[ti, hh] += _dot(values(j, hh), w.astype(jnp.bfloat16))
                suffix_ref[ti, hh] += inner[0:1] + sp2[0:1]
            return j - 1, smallest_suffix()

        lax.while_loop(more_to_do, farther_block, (first_block, smallest_suffix()))
        _store_heads(o_ref.at[:, ti * tile:(ti + 1) * tile], [acc_ref[ti, hh] for hh in range(nh)])


def _sb_attention(qt, k, vt):
    b, d, s = qt.shape
    nh = SB_HEADS
    nt = SB_TILES
    gw = nh * HEAD_DIM
    n_blk = s // ATT_TILE
    assert n_blk % nt == 0
    later = jnp.asarray(np.triu(np.ones((ATT_TILE, ATT_TILE), np.float32), 1), jnp.bfloat16)
    later_near = jnp.asarray(np.triu(np.ones((SB_NEAR_KEYS, SB_NEAR_KEYS), np.float32), 1), jnp.bfloat16)
    return pl.pallas_call(
        functools.partial(_sb_kernel, nh=nh, nt=nt),
        grid=(b, d // gw, n_blk // nt),
        in_specs=[
            pl.BlockSpec((ATT_TILE, ATT_TILE), lambda i, j, t: (0, 0)),
            pl.BlockSpec((SB_NEAR_KEYS, SB_NEAR_KEYS), lambda i, j, t: (0, 0)),
            pl.BlockSpec((1, gw, nt * ATT_TILE), lambda i, j, t: (i, j, t)),
            pl.BlockSpec((1, s, gw), lambda i, j, t: (i, 0, j)),
            pl.BlockSpec((1, n_blk, gw, ATT_TILE), lambda i, j, t: (i, 0, j, 0)),
        ],
        out_specs=pl.BlockSpec((1, nt * ATT_TILE, gw), lambda i, j, t: (i, t, j)),
        out_shape=jax.ShapeDtypeStruct((b, s, d), jnp.bfloat16),
        scratch_shapes=[
            pltpu.VMEM((nt, nh, 1, ATT_TILE), jnp.float32),
            pltpu.VMEM((nt, nh, HEAD_DIM, ATT_TILE), jnp.float32),
        ],
        compiler_params=pltpu.CompilerParams(
            dimension_semantics=("arbitrary", "arbitrary", "arbitrary"),
            vmem_limit_bytes=VMEM_LIMIT),
        name="stick_breaking_attention",
    )(later, later_near, qt, k, vt)


def kernel(x, norm_g, w_in, w_out, final_g):
    b, s, d = x.shape
    depth = norm_g.shape[0]
    assert depth >= 1 and d == N_HEADS * HEAD_DIM and s % PROJ_ROWS == 0 and PROJ_ROWS % ATT_TILE == 0
    assert s % WIDE_ROWS == 0 and WIDE_ROWS % ATT_TILE == 0
    slopes = jnp.asarray(2.0 ** (-8.0 * np.arange(1, N_HEADS + 1) / N_HEADS), jnp.float32)

    w_out_bf = w_out.astype(jnp.bfloat16)

    def proj_weights(layer):
        return w_in, layer, layer + 1 < depth

    h = x
    outs = _layer_boundary(h, norm_g[0], proj=proj_weights(0), rows=WIDE_ROWS)
    for i in range(depth):
        last = i + 1 == depth
        k, qt, vt = outs[0], outs[-2], outs[-1]
        gate = (norm_g[i], w_in) if last else outs[1]
        o_parts = _moba_attention(qt, k, vt, slopes) if i % 2 == 0 else [_sb_attention(qt, k, vt)]
        attn = (o_parts, gate, w_out_bf, i)
        if last:
            rows = WIDE_ROWS if len(o_parts) == 1 else PROJ_ROWS
            (h,) = _layer_boundary(h, final_g, attn=attn, rows=rows)
        else:
            h, *outs = _layer_boundary(h, norm_g[i + 1], attn=attn, proj=proj_weights(i + 1))
    return h
```

```python
import functools
import math

import numpy as np
import jax
import jax.numpy as jnp
from jax import lax
from jax.experimental import pallas as pl
from jax.experimental.pallas import tpu as pltpu

N_HEADS = 16
HEAD_DIM = 64
MOBA_BLOCK = 256
MOBA_TOPK = 3
NORM_EPS = 1e-6

LANES = 128
BF16_ROWS = 16
ATT_TILE = MOBA_BLOCK
PROJ_ROWS = 512
WIDE_ROWS = 1024
MOBA_HEADS = 8
SB_HEADS = 4
SB_TILES = 8
SOFTMAX_LAG = 2
SB_STAGE_LAGS = (2, 3, 4, 5)
LOG2E = math.log2(math.e)
MASKED = -1e30
EXP2_ARG_MAX = 126.0
SB_NEAR_KEYS = 192
SB_SUFFIX_CUTOFF = 152.0
VMEM_LIMIT = 56 * 1024 * 1024


def _dot(a, b):
    return jnp.dot(a, b, preferred_element_type=jnp.float32)


def _dot_nt(a, b):
    return lax.dot_general(a, b, (((1,), (1,)), ((), ())), preferred_element_type=jnp.float32)


def _bf16_round(x):
    return x.astype(jnp.bfloat16).astype(jnp.float32)


def _rms_norm(h, g_ref):
    return h * lax.rsqrt(jnp.mean(h * h, axis=-1, keepdims=True) + NORM_EPS) * g_ref[...]


def _boundary_kernel(*refs, n_o, half_steps, gate_in, has_proj, emit_z, d, scale):
    refs = list(refs)
    if has_proj:
        wvt_ref, wqt_ref = refs.pop(), refs.pop()
    if n_o:
        o_refs = [refs.pop(0) for _ in range(n_o)]
        if gate_in:
            z_ref = refs.pop(0)
        else:
            gate_gain_ref, gate_w_ref = refs.pop(0), refs.pop(0)
        h_ref, wout_ref = refs.pop(0), refs.pop(0)
    else:
        h_ref = refs.pop(0)
    g_ref = refs.pop(0)
    if has_proj:
        wk_ref = refs.pop(0)
        wz_ref = refs.pop(0) if emit_z else None
        wq_ref, wv_ref = refs.pop(0), refs.pop(0)

        @pl.when((pl.program_id(0) == 0) & (pl.program_id(1) == 0))
        def _():
            wqt_ref[...] = wq_ref[0].T.astype(wqt_ref.dtype)
            wvt_ref[...] = wv_ref[0].T.astype(wvt_ref.dtype)
    if n_o:
        hout_ref = refs.pop(0)

    h = h_ref[0]
    if n_o:
        if n_o == 1:
            o = o_refs[0][0]
        else:
            o = jnp.where(pl.program_id(1) < half_steps, o_refs[0][0], o_refs[1][0])
        if gate_in:
            z = z_ref[0].astype(jnp.float32)
        else:
            z = _dot(_rms_norm(h, gate_gain_ref).astype(jnp.bfloat16), gate_w_ref[0].astype(jnp.bfloat16))
        gated = o.astype(jnp.float32) * (z / (1.0 + jnp.exp(-z)))
        h = h + _dot(gated.astype(jnp.bfloat16), wout_ref[0])
    y = _rms_norm(h, g_ref)
    if n_o:
        hout_ref[0] = h if has_proj else y
    if has_proj:
        k_ref = refs.pop(0)
        znext_ref = refs.pop(0) if emit_z else None
        qt_ref, vt_ref = refs
        xn = y.astype(jnp.bfloat16)
        k_ref[0] = _dot(xn, wk_ref[0].astype(jnp.bfloat16)).astype(k_ref.dtype)
        if emit_z:
            znext_ref[0] = _dot(xn, wz_ref[0].astype(jnp.bfloat16)).astype(znext_ref.dtype)
        qt_ref[0] = (_dot_nt(wqt_ref[...], xn) * scale).astype(qt_ref.dtype)
        v_t = _dot_nt(wvt_ref[...], xn)
        for c in range(vt_ref.shape[1]):
            vt_ref[0, c] = v_t[:, c * ATT_TILE:(c + 1) * ATT_TILE].astype(vt_ref.dtype)


def _layer_boundary(h, gain, attn=None, proj=None, rows=PROJ_ROWS):
    b, s, d = h.shape
    n_steps = s // rows
    half = n_steps // 2
    n_blk = s // ATT_TILE
    row_spec = pl.BlockSpec((1, rows, d), lambda i, t: (i, t, 0))

    def const_spec(shape):
        return pl.BlockSpec(shape, lambda i, t: (0,) * len(shape), pipeline_mode=pl.Buffered(1))

    def weight_spec(layer, col_block):
        return pl.BlockSpec((1, d, d), lambda i, t: (layer, 0, col_block), pipeline_mode=pl.Buffered(1))

    in_specs, args, out_specs, out_shape = [], [], [], []
    n_o = 0
    gate_in = True
    if attn is not None:
        o_parts, gate, w_out, out_layer = attn
        n_o = len(o_parts)
        if n_o == 1:
            in_specs.append(row_spec)
        else:
            assert rows == PROJ_ROWS
            in_specs += [
                pl.BlockSpec((1, rows, d), lambda i, t: (i, jnp.minimum(t, half - 1), 0)),
                pl.BlockSpec((1, rows, d), lambda i, t: (i, jnp.clip(n_steps - 1 - t, 0, half - 1), 0)),
            ]
        args += list(o_parts)
        gate_in = not isinstance(gate, tuple)
        if gate_in:
            in_specs.append(row_spec)
            args.append(gate)
        else:
            gate_gain, w_in_all = gate
            in_specs += [const_spec((1, d)), weight_spec(out_layer, 3)]
            args += [gate_gain.reshape(1, d), w_in_all]
        in_specs += [row_spec, weight_spec(out_layer, 0)]
        args += [h, w_out]
        out_specs.append(row_spec)
        out_shape.append(jax.ShapeDtypeStruct((b, s, d), jnp.float32))
    else:
        in_specs.append(row_spec)
        args.append(h)
    in_specs.append(const_spec((1, d)))
    args.append(gain.reshape(1, d))
    emit_z = False
    scratch_shapes = []
    if proj is not None:
        w_in, layer, emit_z = proj
        token_major = (pl.BlockSpec((1, rows, d), lambda i, t: (i, t, 0)),
                       jax.ShapeDtypeStruct((b, s, d), jnp.bfloat16))
        in_specs += [weight_spec(layer, 1)] + ([weight_spec(layer, 3)] if emit_z else [])
        in_specs += [weight_spec(layer, 0), weight_spec(layer, 2)]
        args += [w_in] * (4 if emit_z else 3)
        scratch_shapes = [pltpu.VMEM((d, d), jnp.bfloat16)] * 2
        outputs = [token_major] + ([token_major] if emit_z else []) + [
            (pl.BlockSpec((1, d, rows), lambda i, t: (i, 0, t)),
             jax.ShapeDtypeStruct((b, d, s), jnp.bfloat16)),
            (pl.BlockSpec((1, rows // ATT_TILE, d, ATT_TILE), lambda i, t: (i, t, 0, 0)),
             jax.ShapeDtypeStruct((b, n_blk, d, ATT_TILE), jnp.bfloat16)),
        ]
        out_specs += [spec for spec, _ in outputs]
        out_shape += [shape for _, shape in outputs]
    kern = functools.partial(_boundary_kernel, n_o=n_o, half_steps=half, gate_in=gate_in,
                             has_proj=proj is not None, emit_z=emit_z, d=d, scale=HEAD_DIM ** -0.5 * LOG2E)
    name = ("gate_outproj_" if attn is not None else "") + ("rmsnorm_qkvz_proj" if proj is not None else "final_rmsnorm")
    return pl.pallas_call(
        kern,
        grid=(b, n_steps),
        in_specs=in_specs,
        out_specs=out_specs,
        out_shape=out_shape,
        scratch_shapes=scratch_shapes,
        compiler_params=pltpu.CompilerParams(
            dimension_semantics=("arbitrary", "arbitrary"), vmem_limit_bytes=VMEM_LIMIT),
        name=name,
    )(*args)


def _head_masked(qt, hh):
    row = lax.broadcasted_iota(jnp.int32, qt.shape, 0)
    return jnp.where((row >= hh * HEAD_DIM) & (row < (hh + 1) * HEAD_DIM), qt, jnp.zeros_like(qt))


def _store_heads(o_ref, outs_t):
    stacked = jnp.concatenate(outs_t, axis=0)
    o_ref[0] = stacked.T.astype(o_ref.dtype)


def _moba_key_features(n_blk):
    pos = np.arange(n_blk * ATT_TILE)
    feat = np.zeros((n_blk * ATT_TILE, HEAD_DIM), np.float32)
    feat[pos, pos // ATT_TILE] = 1.0
    feat[:, n_blk:n_blk + 3] = (pos // ATT_TILE)[:, None]
    feat[:, n_blk + 3:n_blk + 6] = (pos % ATT_TILE)[:, None]
    return jnp.asarray(np.concatenate([feat, feat], axis=1), jnp.bfloat16)


def _moba_kernel(slopes_ref, kfeat_ref, qa_ref, qb_ref, k_ref, vt_ref, olo_ref, ohi_ref,
                 kmean_ref, kaug_ref, qaug_ref, m_ref, acc_ref):
    nh = MOBA_HEADS
    hg = pl.program_id(1)
    p = pl.program_id(2)
    n_blk = kmean_ref.shape[0]
    tile = ATT_TILE
    q_tile = (p, n_blk - 1 - p)
    q_refs = (qa_ref, qb_ref)
    key_pos = lax.broadcasted_iota(jnp.int32, (tile, tile), 0)
    qry_pos = lax.broadcasted_iota(jnp.int32, (tile, tile), 1)

    heads_per_group = LANES // HEAD_DIM

    def own_half(hh):
        first = hh % heads_per_group * HEAD_DIM
        return slice(first, first + HEAD_DIM)

    def head_cols(hh):
        return slice(hh * HEAD_DIM, (hh + 1) * HEAD_DIM)

    @pl.when(p == 0)
    def _():
        k_all = k_ref[0]
        kmean_ref[...] = jnp.mean(k_all.astype(jnp.float32).reshape(n_blk, tile, nh * HEAD_DIM), axis=1)
        for hh in range(nh):
            kaug_ref[hh, :, own_half(hh)] = k_all[:, head_cols(hh)]

    @pl.when((pl.program_id(0) == 0) & (hg == 0) & (p == 0))
    def _():
        for hh in range(nh):
            first = HEAD_DIM - own_half(hh).start
            kaug_ref[hh, :, first:first + HEAD_DIM] = kfeat_ref[:, first:first + HEAD_DIM]

    crow = lax.broadcasted_iota(jnp.int32, (HEAD_DIM - n_blk, tile), 0)
    no_choice = jnp.zeros((n_blk, tile), jnp.float32)
    for hh in range(nh):
        slope = slopes_ref[hg * nh + hh] * LOG2E
        base = jnp.where(crow < 3, slope * tile, jnp.where(crow < 6, slope, 0.0))
        part0 = _bf16_round(base)
        part1 = _bf16_round(base - part0)
        part2 = base - part0 - part1
        slope_rows = jnp.where((crow == 0) | (crow == 3), part0,
                               jnp.where((crow == 1) | (crow == 4), part1, part2))
        bias_rows = jnp.concatenate([no_choice, slope_rows], axis=0).astype(jnp.bfloat16)
        for sel in range(2):
            q_rows = q_refs[sel][0, head_cols(hh), :]
            first_half = own_half(hh).start == 0
            qaug_ref[sel, hh] = jnp.concatenate([q_rows, bias_rows] if first_half else [bias_rows, q_rows], axis=0)

    gate_parts = {}
    group_lane = lax.broadcasted_iota(jnp.int32, (n_blk, LANES), 1)
    for group in range(nh // heads_per_group):
        heads = range(group * heads_per_group, (group + 1) * heads_per_group)
        kmean = kmean_ref[:, group * LANES:(group + 1) * LANES]
        gate_lhs = []
        for hh in heads:
            in_own_half = (group_lane >= own_half(hh).start) & (group_lane < own_half(hh).stop)
            kmean_h = jnp.where(in_own_half, kmean, 0.0)
            kmean_hi = kmean_h.astype(jnp.bfloat16)
            gate_lhs += [kmean_hi, (kmean_h - kmean_hi.astype(jnp.float32)).astype(jnp.bfloat16)]
        gate_lhs = jnp.concatenate(gate_lhs, axis=0)
        for sel in range(2):
            g = _dot(gate_lhs, q_refs[sel][0, group * LANES:(group + 1) * LANES, :])
            for c, hh in enumerate(heads):
                gate_parts[sel, hh] = g[2 * c * n_blk:(2 * c + 1) * n_blk] + g[(2 * c + 1) * n_blk:(2 * c + 2) * n_blk]

    def store_block_choice(heads):
        pairs = [(sel, hh) for sel in range(2) for hh in heads]
        gates = jnp.concatenate([gate_parts[pair] for pair in pairs], axis=1)
        blk = lax.broadcasted_iota(jnp.int32, gates.shape, 0).astype(jnp.float32)
        lane = lax.broadcasted_iota(jnp.int32, gates.shape, 1)
        n_past = jnp.where(lane < len(heads) * tile, q_tile[0], q_tile[1]).astype(jnp.float32)
        gate = jnp.where(blk < n_past, gates, -jnp.inf)
        chosen = blk == n_past
        for _ in range(MOBA_TOPK):
            top = jnp.max(gate, axis=0, keepdims=True)
            at_top = (gate == top) & (top > -jnp.inf)
            first = jnp.min(jnp.where(at_top, blk, float(n_blk)), axis=0, keepdims=True)
            pick = blk == first
            chosen = jnp.logical_or(chosen, pick)
            gate = jnp.where(pick, -jnp.inf, gate)
        choice = jnp.where(chosen, 0.0, MASKED).astype(jnp.bfloat16)
        for c, (sel, hh) in enumerate(pairs):
            first_bias_row = HEAD_DIM - own_half(hh).start
            qaug_ref[sel, hh, first_bias_row:first_bias_row + n_blk] = choice[:, c * tile:(c + 1) * tile]

    def key_block(j, hh):
        return kaug_ref[hh, pl.ds(pl.multiple_of(j * tile, tile), tile), :]

    def slot_of(s):
        is_b = s >= p
        return is_b.astype(jnp.int32), jnp.where(is_b, s - p, s)

    causal = key_pos <= qry_pos
    ones_rows = jnp.ones((BF16_ROWS, tile), jnp.bfloat16)

    def weighted_values(j, hh, prob):
        v_aug = jnp.concatenate([vt_ref[0, j, hh * HEAD_DIM:(hh + 1) * HEAD_DIM, :], ones_rows], axis=0)
        return _dot(v_aug, prob.astype(jnp.bfloat16))

    def finish(sel, heads):
        o_ref = (olo_ref, ohi_ref)[sel]
        cols = slice(heads[0] * HEAD_DIM, (heads[-1] + 1) * HEAD_DIM)
        _store_heads(o_ref.at[:, :, cols],
                     [acc_ref[sel, hh, :HEAD_DIM] * (1.0 / acc_ref[sel, hh, HEAD_DIM:HEAD_DIM + 1]) for hh in heads])

    group_items = [(True, sel) for sel in range(2)] + [(False, s) for s in range(n_blk - 1)]
    first_tile_done = (False, n_blk // 2 - 2) if n_blk >= 4 else (True, 1)
    items = [(group, own, idx) for group in range(nh // heads_per_group) for own, idx in group_items]
    in_flight = {}
    for step in range(len(items) + SOFTMAX_LAG):
        if step < len(items):
            group, own, idx = items[step]
            heads = list(range(group * heads_per_group, (group + 1) * heads_per_group))
            if (own, idx) == group_items[2]:
                store_block_choice(heads)
            sel, j = (idx, q_tile[idx]) if own else slot_of(idx)
            logits = [_dot(key_block(j, hh), qaug_ref[sel, hh]) for hh in heads]
            if own:
                logits = [jnp.where(causal, lg, MASKED) for lg in logits]
            in_flight[step] = (heads, own, sel, j, logits)
        done = step - SOFTMAX_LAG
        if done >= 0:
            heads, own, sel, j, logits = in_flight.pop(done)
            for hh, lg in zip(heads, logits):
                top = jnp.max(lg, axis=0, keepdims=True)
                if own:
                    m_ref[sel, hh] = top
                    acc_ref[sel, hh] = weighted_values(j, hh, jnp.exp2(lg - top))
                else:
                    m_old = m_ref[sel, hh]
                    m_new = jnp.maximum(m_old, top)
                    m_ref[sel, hh] = m_new
                    acc_ref[sel, hh] = (jnp.exp2(m_old - m_new) * acc_ref[sel, hh]
                                        + weighted_values(j, hh, jnp.exp2(lg - m_new)))
            if items[done][1:] == first_tile_done:
                finish(0, heads)
            if items[done][1:] == group_items[-1]:
                finish(1, heads)


def _moba_attention(qt, k, vt, slopes):
    b, d, s = qt.shape
    nh = MOBA_HEADS
    gw = nh * HEAD_DIM
    n_blk = s // ATT_TILE
    assert gw % LANES == 0 and n_blk % 2 == 0 and PROJ_ROWS == 2 * ATT_TILE and HEAD_DIM >= n_blk + 6
    half = jax.ShapeDtypeStruct((b, s // 2, d), jnp.bfloat16)
    return pl.pallas_call(
        _moba_kernel,
        grid=(b, d // gw, n_blk // 2),
        in_specs=[
            pl.BlockSpec(memory_space=pltpu.SMEM),
            pl.BlockSpec((s, LANES), lambda i, j, t: (0, 0)),
            pl.BlockSpec((1, gw, ATT_TILE), lambda i, j, t: (i, j, t)),
            pl.BlockSpec((1, gw, ATT_TILE), lambda i, j, t: (i, j, n_blk - 1 - t)),
            pl.BlockSpec((1, s, gw), lambda i, j, t: (i, 0, j)),
            pl.BlockSpec((1, n_blk, gw, ATT_TILE), lambda i, j, t: (i, 0, j, 0)),
        ],
        out_specs=[
            pl.BlockSpec((1, ATT_TILE, gw), lambda i, j, t: (i, t, j)),
            pl.BlockSpec((1, ATT_TILE, gw), lambda i, j, t: (i, lax.bitwise_xor(t, 1), j)),
        ],
        out_shape=[half, half],
        scratch_shapes=[
            pltpu.VMEM((n_blk, gw), jnp.float32),
            pltpu.VMEM((nh, s, LANES), jnp.bfloat16),
            pltpu.VMEM((2, nh, LANES, ATT_TILE), jnp.bfloat16),
            pltpu.VMEM((2, nh, 1, ATT_TILE), jnp.float32),
            pltpu.VMEM((2, nh, HEAD_DIM + BF16_ROWS, ATT_TILE), jnp.float32),
        ],
        compiler_params=pltpu.CompilerParams(
            dimension_semantics=("arbitrary", "arbitrary", "arbitrary"),
            vmem_limit_bytes=VMEM_LIMIT),
        name="moba_attention",
    )(slopes, _moba_key_features(n_blk), qt, qt, k, vt)


def _softplus2(z2):
    return jnp.maximum(z2, jnp.log(1.0 + jnp.exp2(jnp.minimum(z2, EXP2_ARG_MAX))) * LOG2E)


def _sb_kernel(later_ref, later_near_ref, qt_ref, k_ref, vt_ref, o_ref, suffix_ref, acc_ref, *, nh, nt):
    first_tile = pl.program_id(2) * nt
    tile = ATT_TILE
    near = SB_NEAR_KEYS
    far = tile - near
    heads_per_group = LANES // HEAD_DIM

    def lane_group(hh):
        first = hh // heads_per_group * LANES
        return slice(first, first + LANES)

    qm = {(ti, hh): _head_masked(qt_ref[0, lane_group(hh), ti * tile:(ti + 1) * tile], hh % heads_per_group)
          for ti in range(nt) for hh in range(nh)}
    key_pos = lax.broadcasted_iota(jnp.int32, (tile, tile), 0)
    qry_pos = lax.broadcasted_iota(jnp.int32, (tile, tile), 1)
    strict = key_pos < qry_pos

    def key_block(j, hh):
        return k_ref[0, pl.ds(pl.multiple_of(j * tile, tile), tile), lane_group(hh)]

    def near_keys(j, hh):
        return k_ref[0, pl.ds(pl.multiple_of(j * tile + far, BF16_ROWS), near), lane_group(hh)]

    def later_sums(sp2):
        later = later_ref if sp2.shape[0] == tile else later_near_ref
        return _dot(later[...], sp2.astype(jnp.bfloat16))

    def values(j, hh):
        return vt_ref[0, j, hh * HEAD_DIM:(hh + 1) * HEAD_DIM, :]

    streams = [(ti, hh) for ti in range(nt) for hh in range(nh)]
    units = ([(ti, hh, first_tile + ti, True) for ti, hh in streams]
             + [(ti, hh, jnp.maximum(first_tile + ti - 1, 0), False) for ti, hh in streams])
    half = tile // 2

    def live_parts(x):
        return x[:half], x[half:, half:]

    def full_tile(parts):
        first, last = parts
        return jnp.concatenate([first, jnp.concatenate([jnp.zeros_like(last), last], axis=1)], axis=0)

    strict_parts = live_parts(strict)
    z2s, sp2s, inners, weights, totals, outs = {}, {}, {}, {}, {}, {}
    lag_softplus, lag_sums, lag_weights, lag_values = SB_STAGE_LAGS
    for step in range(len(units) + lag_values):
        if step < len(units):
            ti, hh, j, own = units[step]
            z2 = _dot(key_block(j, hh) if own else near_keys(j, hh), qm[ti, hh])
            if own:
                z2s[step] = [jnp.where(m, part, MASKED) for m, part in zip(strict_parts, live_parts(z2))]
            else:
                z2s[step] = [z2]
        u = step - lag_softplus
        if 0 <= u < len(units):
            sp2s[u] = [_softplus2(part) for part in z2s[u]]
        u = step - lag_sums
        if 0 <= u < len(units):
            sp2_bf = [part.astype(jnp.bfloat16) for part in sp2s[u]]
            inners[u] = later_sums(full_tile(sp2_bf) if units[u][3] else sp2_bf[0])
        u = step - lag_weights
        if 0 <= u < len(units):
            ti, _, _, own = units[u]
            totals[u] = inners[u][0:1] + sp2s[u][0][0:1]
            inner_parts = live_parts(inners[u]) if own else [inners[u]]
            expo = [z2 - sp2 - inner for z2, sp2, inner in zip(z2s[u], sp2s[u], inner_parts)]
            if own:
                weights[u] = full_tile([jnp.exp2(part).astype(jnp.bfloat16) for part in expo])
            else:
                own_total = totals[u - len(streams)]
                suffix = jnp.where(first_tile + ti > 0, own_total, -MASKED)
                weights[u] = jnp.exp2(expo[0] - suffix).astype(jnp.bfloat16)
        u = step - lag_values
        if 0 <= u < len(units):
            _, hh, j, own = units[u]
            outs[u] = _dot(values(j, hh) if own else values(j, hh)[:, far:], weights[u])
    for c, (ti, hh) in enumerate(streams):
        acc_ref[ti, hh] = outs[c] + outs[len(streams) + c]
        suffix_ref[ti, hh] = totals[c] + totals[len(streams) + c]

    for ti in range(nt):
        def smallest_suffix(ti=ti):
            return functools.reduce(jnp.minimum, [jnp.min(suffix_ref[ti, hh]) for hh in range(nh)])

        def more_to_do(carry):
            j, smallest = carry
            return jnp.logical_and(j >= 0, smallest <= SB_SUFFIX_CUTOFF)

        first_block = first_tile + ti - 1

        def farther_block(carry, ti=ti, smallest_suffix=smallest_suffix, first_block=first_block):
            j, _ = carry
            counted = jnp.logical_and(j == first_block, key_pos >= far)
            for hh in range(nh):
                z2 = jnp.where(counted, MASKED, _dot(key_block(j, hh), qm[ti, hh]))
                sp2 = _softplus2(z2)
                inner = later_sums(sp2)
                w = jnp.exp2(z2 - sp2 - inner - suffix_ref[ti, hh])
                acc_ref[ti, hh] += _dot(values(j, hh), w.astype(jnp.bfloat16))
                suffix_ref[ti, hh] += inner[0:1] + sp2[0:1]
            return j - 1, smallest_suffix()

        lax.while_loop(more_to_do, farther_block, (first_block, smallest_suffix()))
        _store_heads(o_ref.at[:, ti * tile:(ti + 1) * tile], [acc_ref[ti, hh] for hh in range(nh)])


def _sb_attention(qt, k, vt):
    b, d, s = qt.shape
    nh = SB_HEADS
    nt = SB_TILES
    gw = nh * HEAD_DIM
    n_blk = s // ATT_TILE
    assert n_blk % nt == 0
    later = jnp.asarray(np.triu(np.ones((ATT_TILE, ATT_TILE), np.float32), 1), jnp.bfloat16)
    later_near = jnp.asarray(np.triu(np.ones((SB_NEAR_KEYS, SB_NEAR_KEYS), np.float32), 1), jnp.bfloat16)
    return pl.pallas_call(
        functools.partial(_sb_kernel, nh=nh, nt=nt),
        grid=(b, d // gw, n_blk // nt),
        in_specs=[
            pl.BlockSpec((ATT_TILE, ATT_TILE), lambda i, j, t: (0, 0)),
            pl.BlockSpec((SB_NEAR_KEYS, SB_NEAR_KEYS), lambda i, j, t: (0, 0)),
            pl.BlockSpec((1, gw, nt * ATT_TILE), lambda i, j, t: (i, j, t)),
            pl.BlockSpec((1, s, gw), lambda i, j, t: (i, 0, j)),
            pl.BlockSpec((1, n_blk, gw, ATT_TILE), lambda i, j, t: (i, 0, j, 0)),
        ],
        out_specs=pl.BlockSpec((1, nt * ATT_TILE, gw), lambda i, j, t: (i, t, j)),
        out_shape=jax.ShapeDtypeStruct((b, s, d), jnp.bfloat16),
        scratch_shapes=[
            pltpu.VMEM((nt, nh, 1, ATT_TILE), jnp.float32),
            pltpu.VMEM((nt, nh, HEAD_DIM, ATT_TILE), jnp.float32),
        ],
        compiler_params=pltpu.CompilerParams(
            dimension_semantics=("arbitrary", "arbitrary", "arbitrary"),
            vmem_limit_bytes=VMEM_LIMIT),
        name="stick_breaking_attention",
    )(later, later_near, qt, k, vt)


def kernel(x, norm_g, w_in, w_out, final_g):
    b, s, d = x.shape
    depth = norm_g.shape[0]
    assert depth >= 1 and d == N_HEADS * HEAD_DIM and s % PROJ_ROWS == 0 and PROJ_ROWS % ATT_TILE == 0
    assert s % WIDE_ROWS == 0 and WIDE_ROWS % ATT_TILE == 0
    slopes = jnp.asarray(2.0 ** (-8.0 * np.arange(1, N_HEADS + 1) / N_HEADS), jnp.float32)

    w_out_bf = w_out.astype(jnp.bfloat16)

    def proj_weights(layer):
        return w_in, layer, layer + 1 < depth

    h = x
    outs = _layer_boundary(h, norm_g[0], proj=proj_weights(0), rows=WIDE_ROWS)
    for i in range(depth):
        last = i + 1 == depth
        k, qt, vt = outs[0], outs[-2], outs[-1]
        gate = (norm_g[i], w_in) if last else outs[1]
        o_parts = _moba_attention(qt, k, vt, slopes) if i % 2 == 0 else [_sb_attention(qt, k, vt)]
        attn = (o_parts, gate, w_out_bf, i)
        if last:
            rows = WIDE_ROWS if len(o_parts) == 1 else PROJ_ROWS
            (h,) = _layer_boundary(h, final_g, attn=attn, rows=rows)
        else:
            h, *outs = _layer_boundary(h, norm_g[i + 1], attn=attn, proj=proj_weights(i + 1))
    return h
```

```python
import functools
import math

import numpy as np
import jax
import jax.numpy as jnp
from jax import lax
from jax.experimental import pallas as pl
from jax.experimental.pallas import tpu as pltpu

N_HEADS = 16
HEAD_DIM = 64
MOBA_BLOCK = 256
MOBA_TOPK = 3
NORM_EPS = 1e-6

LANES = 128
BF16_ROWS = 16
ATT_TILE = MOBA_BLOCK
PROJ_ROWS = 512
WIDE_ROWS = 1024
MOBA_HEADS = 8
SB_HEADS = 4
SB_TILES = 8
SOFTMAX_LAG = 2
SB_STAGE_LAGS = (2, 3, 4, 5)
LOG2E = math.log2(math.e)
MASKED = -1e30
EXP2_ARG_MAX = 126.0
SB_NEAR_KEYS = 192
SB_SUFFIX_CUTOFF = 152.0
VMEM_LIMIT = 56 * 1024 * 1024


def _dot(a, b):
    return jnp.dot(a, b, preferred_element_type=jnp.float32)


def _dot_nt(a, b):
    return lax.dot_general(a, b, (((1,), (1,)), ((), ())), preferred_element_type=jnp.float32)


def _bf16_round(x):
    return x.astype(jnp.bfloat16).astype(jnp.float32)


def _rms_norm(h, g_ref):
    return h * lax.rsqrt(jnp.mean(h * h, axis=-1, keepdims=True) + NORM_EPS) * g_ref[...]


def _boundary_kernel(*refs, n_o, half_steps, gate_in, has_proj, emit_z, d, scale):
    refs = list(refs)
    if has_proj:
        wvt_ref, wqt_ref = refs.pop(), refs.pop()
    if n_o:
        o_refs = [refs.pop(0) for _ in range(n_o)]
        if gate_in:
            z_ref = refs.pop(0)
        else:
            gate_gain_ref, gate_w_ref = refs.pop(0), refs.pop(0)
        h_ref, wout_ref = refs.pop(0), refs.pop(0)
    else:
        h_ref = refs.pop(0)
    g_ref = refs.pop(0)
    if has_proj:
        wk_ref = refs.pop(0)
        wz_ref = refs.pop(0) if emit_z else None
        wq_ref, wv_ref = refs.pop(0), refs.pop(0)

        @pl.when((pl.program_id(0) == 0) & (pl.program_id(1) == 0))
        def _():
            wqt_ref[...] = wq_ref[0].T.astype(wqt_ref.dtype)
            wvt_ref[...] = wv_ref[0].T.astype(wvt_ref.dtype)
    if n_o:
        hout_ref = refs.pop(0)

    h = h_ref[0]
    if n_o:
        if n_o == 1:
            o = o_refs[0][0]
        else:
            o = jnp.where(pl.program_id(1) < half_steps, o_refs[0][0], o_refs[1][0])
        if gate_in:
            z = z_ref[0].astype(jnp.float32)
        else:
            z = _dot(_rms_norm(h, gate_gain_ref).astype(jnp.bfloat16), gate_w_ref[0].astype(jnp.bfloat16))
        gated = o.astype(jnp.float32) * (z / (1.0 + jnp.exp(-z)))
        h = h + _dot(gated.astype(jnp.bfloat16), wout_ref[0])
    y = _rms_norm(h, g_ref)
    if n_o:
        hout_ref[0] = h if has_proj else y
    if has_proj:
        k_ref = refs.pop(0)
        znext_ref = refs.pop(0) if emit_z else None
        qt_ref, vt_ref = refs
        xn = y.astype(jnp.bfloat16)
        k_ref[0] = _dot(xn, wk_ref[0].astype(jnp.bfloat16)).astype(k_ref.dtype)
        if emit_z:
            znext_ref[0] = _dot(xn, wz_ref[0].astype(jnp.bfloat16)).astype(znext_ref.dtype)
        qt_ref[0] = (_dot_nt(wqt_ref[...], xn) * scale).astype(qt_ref.dtype)
        v_t = _dot_nt(wvt_ref[...], xn)
        for c in range(vt_ref.shape[1]):
            vt_ref[0, c] = v_t[:, c * ATT_TILE:(c + 1) * ATT_TILE].astype(vt_ref.dtype)


def _layer_boundary(h, gain, attn=None, proj=None, rows=PROJ_ROWS):
    b, s, d = h.shape
    n_steps = s // rows
    half = n_steps // 2
    n_blk = s // ATT_TILE
    row_spec = pl.BlockSpec((1, rows, d), lambda i, t: (i, t, 0))

    def const_spec(shape):
        return pl.BlockSpec(shape, lambda i, t: (0,) * len(shape), pipeline_mode=pl.Buffered(1))

    def weight_spec(layer, col_block):
        return pl.BlockSpec((1, d, d), lambda i, t: (layer, 0, col_block), pipeline_mode=pl.Buffered(1))

    in_specs, args, out_specs, out_shape = [], [], [], []
    n_o = 0
    gate_in = True
    if attn is not None:
        o_parts, gate, w_out, out_layer = attn
        n_o = len(o_parts)
        if n_o == 1:
            in_specs.append(row_spec)
        else:
            assert rows == PROJ_ROWS
            in_specs += [
                pl.BlockSpec((1, rows, d), lambda i, t: (i, jnp.minimum(t, half - 1), 0)),
                pl.BlockSpec((1, rows, d), lambda i, t: (i, jnp.clip(n_steps - 1 - t, 0, half - 1), 0)),
            ]
        args += list(o_parts)
        gate_in = not isinstance(gate, tuple)
        if gate_in:
            in_specs.append(row_spec)
            args.append(gate)
        else:
            gate_gain, w_in_all = gate
            in_specs += [const_spec((1, d)), weight_spec(out_layer, 3)]
            args += [gate_gain.reshape(1, d), w_in_all]
        in_specs += [row_spec, weight_spec(out_layer, 0)]
        args += [h, w_out]
        out_specs.append(row_spec)
        out_shape.append(jax.ShapeDtypeStruct((b, s, d), jnp.float32))
    else:
        in_specs.append(row_spec)
        args.append(h)
    in_specs.append(const_spec((1, d)))
    args.append(gain.reshape(1, d))
    emit_z = False
    scratch_shapes = []
    if proj is not None:
        w_in, layer, emit_z = proj
        token_major = (pl.BlockSpec((1, rows, d), lambda i, t: (i, t, 0)),
                       jax.ShapeDtypeStruct((b, s, d), jnp.bfloat16))
        in_specs += [weight_spec(layer, 1)] + ([weight_spec(layer, 3)] if emit_z else [])
        in_specs += [weight_spec(layer, 0), weight_spec(layer, 2)]
        args += [w_in] * (4 if emit_z else 3)
        scratch_shapes = [pltpu.VMEM((d, d), jnp.bfloat16)] * 2
        outputs = [token_major] + ([token_major] if emit_z else []) + [
            (pl.BlockSpec((1, d, rows), lambda i, t: (i, 0, t)),
             jax.ShapeDtypeStruct((b, d, s), jnp.bfloat16)),
            (pl.BlockSpec((1, rows // ATT_TILE, d, ATT_TILE), lambda i, t: (i, t, 0, 0)),
             jax.ShapeDtypeStruct((b, n_blk, d, ATT_TILE), jnp.bfloat16)),
        ]
        out_specs += [spec for spec, _ in outputs]
        out_shape += [shape for _, shape in outputs]
    kern = functools.partial(_boundary_kernel, n_o=n_o, half_steps=half, gate_in=gate_in,
                             has_proj=proj is not None, emit_z=emit_z, d=d, scale=HEAD_DIM ** -0.5 * LOG2E)
    name = ("gate_outproj_" if attn is not None else "") + ("rmsnorm_qkvz_proj" if proj is not None else "final_rmsnorm")
    return pl.pallas_call(
        kern,
        grid=(b, n_steps),
        in_specs=in_specs,
        out_specs=out_specs,
        out_shape=out_shape,
        scratch_shapes=scratch_shapes,
        compiler_params=pltpu.CompilerParams(
            dimension_semantics=("arbitrary", "arbitrary"), vmem_limit_bytes=VMEM_LIMIT),
        name=name,
    )(*args)


def _head_masked(qt, hh):
    row = lax.broadcasted_iota(jnp.int32, qt.shape, 0)
    return jnp.where((row >= hh * HEAD_DIM) & (row < (hh + 1) * HEAD_DIM), qt, jnp.zeros_like(qt))


def _store_heads(o_ref, outs_t):
    stacked = jnp.concatenate(outs_t, axis=0)
    o_ref[0] = stacked.T.astype(o_ref.dtype)


def _moba_key_features(n_blk):
    pos = np.arange(n_blk * ATT_TILE)
    feat = np.zeros((n_blk * ATT_TILE, HEAD_DIM), np.float32)
    feat[pos, pos // ATT_TILE] = 1.0
    feat[:, n_blk:n_blk + 3] = (pos // ATT_TILE)[:, None]
    feat[:, n_blk + 3:n_blk + 6] = (pos % ATT_TILE)[:, None]
    return jnp.asarray(np.concatenate([feat, feat], axis=1), jnp.bfloat16)


def _moba_kernel(slopes_ref, kfeat_ref, qa_ref, qb_ref, k_ref, vt_ref, olo_ref, ohi_ref,
                 kmean_ref, kaug_ref, qaug_ref, m_ref, acc_ref):
    nh = MOBA_HEADS
    hg = pl.program_id(1)
    p = pl.program_id(2)
    n_blk = kmean_ref.shape[0]
    tile = ATT_TILE
    q_tile = (p, n_blk - 1 - p)
    q_refs = (qa_ref, qb_ref)
    key_pos = lax.broadcasted_iota(jnp.int32, (tile, tile), 0)
    qry_pos = lax.broadcasted_iota(jnp.int32, (tile, tile), 1)

    heads_per_group = LANES // HEAD_DIM

    def own_half(hh):
        first = hh % heads_per_group * HEAD_DIM
        return slice(first, first + HEAD_DIM)

    def head_cols(hh):
        return slice(hh * HEAD_DIM, (hh + 1) * HEAD_DIM)

    @pl.when(p == 0)
    def _():
        k_all = k_ref[0]
        kmean_ref[...] = jnp.mean(k_all.astype(jnp.float32).reshape(n_blk, tile, nh * HEAD_DIM), axis=1)
        for hh in range(nh):
            kaug_ref[hh, :, own_half(hh)] = k_all[:, head_cols(hh)]

    @pl.when((pl.program_id(0) == 0) & (hg == 0) & (p == 0))
    def _():
        for hh in range(nh):
            first = HEAD_DIM - own_half(hh).start
            kaug_ref[hh, :, first:first + HEAD_DIM] = kfeat_ref[:, first:first + HEAD_DIM]

    crow = lax.broadcasted_iota(jnp.int32, (HEAD_DIM - n_blk, tile), 0)
    no_choice = jnp.zeros((n_blk, tile), jnp.float32)
    for hh in range(nh):
        slope = slopes_ref[hg * nh + hh] * LOG2E
        base = jnp.where(crow < 3, slope * tile, jnp.where(crow < 6, slope, 0.0))
        part0 = _bf16_round(base)
        part1 = _bf16_round(base - part0)
        part2 = base - part0 - part1
        slope_rows = jnp.where((crow == 0) | (crow == 3), part0,
                               jnp.where((crow == 1) | (crow == 4), part1, part2))
        bias_rows = jnp.concatenate([no_choice, slope_rows], axis=0).astype(jnp.bfloat16)
        for sel in range(2):
            q_rows = q_refs[sel][0, head_cols(hh), :]
            first_half = own_half(hh).start == 0
            qaug_ref[sel, hh] = jnp.concatenate([q_rows, bias_rows] if first_half else [bias_rows, q_rows], axis=0)

    gate_parts = {}
    group_lane = lax.broadcasted_iota(jnp.int32, (n_blk, LANES), 1)
    for group in range(nh // heads_per_group):
        heads = range(group * heads_per_group, (group + 1) * heads_per_group)
        kmean = kmean_ref[:, group * LANES:(group + 1) * LANES]
        gate_lhs = []
        for hh in heads:
            in_own_half = (group_lane >= own_half(hh).start) & (group_lane < own_half(hh).stop)
            kmean_h = jnp.where(in_own_half, kmean, 0.0)
            kmean_hi = kmean_h.astype(jnp.bfloat16)
            gate_lhs += [kmean_hi, (kmean_h - kmean_hi.astype(jnp.float32)).astype(jnp.bfloat16)]
        gate_lhs = jnp.concatenate(gate_lhs, axis=0)
        for sel in range(2):
            g = _dot(gate_lhs, q_refs[sel][0, group * LANES:(group + 1) * LANES, :])
            for c, hh in enumerate(heads):
                gate_parts[sel, hh] = g[2 * c * n_blk:(2 * c + 1) * n_blk] + g[(2 * c + 1) * n_blk:(2 * c + 2) * n_blk]

    def store_block_choice(heads):
        pairs = [(sel, hh) for sel in range(2) for hh in heads]
        gates = jnp.concatenate([gate_parts[pair] for pair in pairs], axis=1)
        blk = lax.broadcasted_iota(jnp.int32, gates.shape, 0).astype(jnp.float32)
        lane = lax.broadcasted_iota(jnp.int32, gates.shape, 1)
        n_past = jnp.where(lane < len(heads) * tile, q_tile[0], q_tile[1]).astype(jnp.float32)
        gate = jnp.where(blk < n_past, gates, -jnp.inf)
        chosen = blk == n_past
        for _ in range(MOBA_TOPK):
            top = jnp.max(gate, axis=0, keepdims=True)
            at_top = (gate == top) & (top > -jnp.inf)
            first = jnp.min(jnp.where(at_top, blk, float(n_blk)), axis=0, keepdims=True)
            pick = blk == first
            chosen = jnp.logical_or(chosen, pick)
            gate = jnp.where(pick, -jnp.inf, gate)
        choice = jnp.where(chosen, 0.0, MASKED).astype(jnp.bfloat16)
        for c, (sel, hh) in enumerate(pairs):
            first_bias_row = HEAD_DIM - own_half(hh).start
            qaug_ref[sel, hh, first_bias_row:first_bias_row + n_blk] = choice[:, c * tile:(c + 1) * tile]

    def key_block(j, hh):
        return kaug_ref[hh, pl.ds(pl.multiple_of(j * tile, tile), tile), :]

    def slot_of(s):
        is_b = s >= p
        return is_b.astype(jnp.int32), jnp.where(is_b, s - p, s)

    causal = key_pos <= qry_pos
    ones_rows = jnp.ones((BF16_ROWS, tile), jnp.bfloat16)

    def weighted_values(j, hh, prob):
        v_aug = jnp.concatenate([vt_ref[0, j, hh * HEAD_DIM:(hh + 1) * HEAD_DIM, :], ones_rows], axis=0)
        return _dot(v_aug, prob.astype(jnp.bfloat16))

    def finish(sel, heads):
        o_ref = (olo_ref, ohi_ref)[sel]
        cols = slice(heads[0] * HEAD_DIM, (heads[-1] + 1) * HEAD_DIM)
        _store_heads(o_ref.at[:, :, cols],
                     [acc_ref[sel, hh, :HEAD_DIM] * (1.0 / acc_ref[sel, hh, HEAD_DIM:HEAD_DIM + 1]) for hh in heads])

    group_items = [(True, sel) for sel in range(2)] + [(False, s) for s in range(n_blk - 1)]
    first_tile_done = (False, n_blk // 2 - 2) if n_blk >= 4 else (True, 1)
    items = [(group, own, idx) for group in range(nh // heads_per_group) for own, idx in group_items]
    in_flight = {}
    for step in range(len(items) + SOFTMAX_LAG):
        if step < len(items):
            group, own, idx = items[step]
            heads = list(range(group * heads_per_group, (group + 1) * heads_per_group))
            if (own, idx) == group_items[2]:
                store_block_choice(heads)
            sel, j = (idx, q_tile[idx]) if own else slot_of(idx)
            logits = [_dot(key_block(j, hh), qaug_ref[sel, hh]) for hh in heads]
            if own:
                logits = [jnp.where(causal, lg, MASKED) for lg in logits]
            in_flight[step] = (heads, own, sel, j, logits)
        done = step - SOFTMAX_LAG
        if done >= 0:
            heads, own, sel, j, logits = in_flight.pop(done)
            for hh, lg in zip(heads, logits):
                m_old = None if own else m_ref[sel, hh]
                m_new, prob = [], []
                for c in range(tile // LANES):
                    cols = slice(c * LANES, (c + 1) * LANES)
                    top = jnp.max(lg[:, cols], axis=0, keepdims=True)
                    m_c = top if own else jnp.maximum(m_old[:, cols], top)
                    m_new.append(m_c)
                    prob.append(jnp.exp2(lg[:, cols] - m_c).astype(jnp.bfloat16))
                m_new = jnp.concatenate(m_new, axis=1)
                m_ref[sel, hh] = m_new
                pv = weighted_values(j, hh, jnp.concatenate(prob, axis=1))
                acc_ref[sel, hh] = pv if own else jnp.exp2(m_old - m_new) * acc_ref[sel, hh] + pv
            if items[done][1:] == first_tile_done:
                finish(0, heads)
            if items[done][1:] == group_items[-1]:
                finish(1, heads)


def _moba_attention(qt, k, vt, slopes):
    b, d, s = qt.shape
    nh = MOBA_HEADS
    gw = nh * HEAD_DIM
    n_blk = s // ATT_TILE
    assert gw % LANES == 0 and n_blk % 2 == 0 and PROJ_ROWS == 2 * ATT_TILE and HEAD_DIM >= n_blk + 6
    half = jax.ShapeDtypeStruct((b, s // 2, d), jnp.bfloat16)
    return pl.pallas_call(
        _moba_kernel,
        grid=(b, d // gw, n_blk // 2),
        in_specs=[
            pl.BlockSpec(memory_space=pltpu.SMEM),
            pl.BlockSpec((s, LANES), lambda i, j, t: (0, 0)),
            pl.BlockSpec((1, gw, ATT_TILE), lambda i, j, t: (i, j, t)),
            pl.BlockSpec((1, gw, ATT_TILE), lambda i, j, t: (i, j, n_blk - 1 - t)),
            pl.BlockSpec((1, s, gw), lambda i, j, t: (i, 0, j)),
            pl.BlockSpec((1, n_blk, gw, ATT_TILE), lambda i, j, t: (i, 0, j, 0)),
        ],
        out_specs=[
            pl.BlockSpec((1, ATT_TILE, gw), lambda i, j, t: (i, t, j)),
            pl.BlockSpec((1, ATT_TILE, gw), lambda i, j, t: (i, lax.bitwise_xor(t, 1), j)),
        ],
        out_shape=[half, half],
        scratch_shapes=[
            pltpu.VMEM((n_blk, gw), jnp.float32),
            pltpu.VMEM((nh, s, LANES), jnp.bfloat16),
            pltpu.VMEM((2, nh, LANES, ATT_TILE), jnp.bfloat16),
            pltpu.VMEM((2, nh, 1, ATT_TILE), jnp.float32),
            pltpu.VMEM((2, nh, HEAD_DIM + BF16_ROWS, ATT_TILE), jnp.float32),
        ],
        compiler_params=pltpu.CompilerParams(
            dimension_semantics=("arbitrary", "arbitrary", "arbitrary"),
            vmem_limit_bytes=VMEM_LIMIT),
        name="moba_attention",
    )(slopes, _moba_key_features(n_blk), qt, qt, k, vt)


def _softplus2(z2):
    return jnp.maximum(z2, jnp.log(1.0 + jnp.exp2(jnp.minimum(z2, EXP2_ARG_MAX))) * LOG2E)


def _sb_kernel(later_ref, later_near_ref, qt_ref, k_ref, vt_ref, o_ref, suffix_ref, acc_ref, *, nh, nt):
    first_tile = pl.program_id(2) * nt
    tile = ATT_TILE
    near = SB_NEAR_KEYS
    far = tile - near
    heads_per_group = LANES // HEAD_DIM

    def lane_group(hh):
        first = hh // heads_per_group * LANES
        return slice(first, first + LANES)

    qm = {(ti, hh): _head_masked(qt_ref[0, lane_group(hh), ti * tile:(ti + 1) * tile], hh % heads_per_group)
          for ti in range(nt) for hh in range(nh)}
    key_pos = lax.broadcasted_iota(jnp.int32, (tile, tile), 0)
    qry_pos = lax.broadcasted_iota(jnp.int32, (tile, tile), 1)
    strict = key_pos < qry_pos

    def key_block(j, hh):
        return k_ref[0, pl.ds(pl.multiple_of(j * tile, tile), tile), lane_group(hh)]

    def near_keys(j, hh):
        return k_ref[0, pl.ds(pl.multiple_of(j * tile + far, BF16_ROWS), near), lane_group(hh)]

    def later_sums(sp2):
        later = later_ref if sp2.shape[0] == tile else later_near_ref
        return _dot(later[...], sp2.astype(jnp.bfloat16))

    def values(j, hh):
        return vt_ref[0, j, hh * HEAD_DIM:(hh + 1) * HEAD_DIM, :]

    streams = [(ti, hh) for ti in range(nt) for hh in range(nh)]
    units = ([(ti, hh, first_tile + ti, True) for ti, hh in streams]
             + [(ti, hh, jnp.maximum(first_tile + ti - 1, 0), False) for ti, hh in streams])
    half = tile // 2

    def live_parts(x):
        return x[:half], x[half:, half:]

    def full_tile(parts):
        first, last = parts
        return jnp.concatenate([first, jnp.concatenate([jnp.zeros_like(last), last], axis=1)], axis=0)

    strict_parts = live_parts(strict)
    z2s, sp2s, inners, weights, totals, outs = {}, {}, {}, {}, {}, {}
    lag_softplus, lag_sums, lag_weights, lag_values = SB_STAGE_LAGS
    for step in range(len(units) + lag_values):
        if step < len(units):
            ti, hh, j, own = units[step]
            z2 = _dot(key_block(j, hh) if own else near_keys(j, hh), qm[ti, hh])
            if own:
                z2s[step] = [jnp.where(m, part, MASKED) for m, part in zip(strict_parts, live_parts(z2))]
            else:
                z2s[step] = [z2]
        u = step - lag_softplus
        if 0 <= u < len(units):
            sp2s[u] = [_softplus2(part) for part in z2s[u]]
        u = step - lag_sums
        if 0 <= u < len(units):
            sp2_bf = [part.astype(jnp.bfloat16) for part in sp2s[u]]
            inners[u] = later_sums(full_tile(sp2_bf) if units[u][3] else sp2_bf[0])
        u = step - lag_weights
        if 0 <= u < len(units):
            ti, _, _, own = units[u]
            totals[u] = inners[u][0:1] + sp2s[u][0][0:1]
            inner_parts = live_parts(inners[u]) if own else [inners[u]]
            expo = [z2 - sp2 - inner for z2, sp2, inner in zip(z2s[u], sp2s[u], inner_parts)]
            if own:
                weights[u] = full_tile([jnp.exp2(part).astype(jnp.bfloat16) for part in expo])
            else:
                own_total = totals[u - len(streams)]
                suffix = jnp.where(first_tile + ti > 0, own_total, -MASKED)
                weights[u] = jnp.exp2(expo[0] - suffix).astype(jnp.bfloat16)
        u = step - lag_values
        if 0 <= u < len(units):
            _, hh, j, own = units[u]
            outs[u] = _dot(values(j, hh) if own else values(j, hh)[:, far:], weights[u])
    for c, (ti, hh) in enumerate(streams):
        acc_ref[ti, hh] = outs[c] + outs[len(streams) + c]
        suffix_ref[ti, hh] = totals[c] + totals[len(streams) + c]

    for ti in range(nt):
        def smallest_suffix(ti=ti):
            return functools.reduce(jnp.minimum, [jnp.min(suffix_ref[ti, hh]) for hh in range(nh)])

        def more_to_do(carry):
            j, smallest = carry
            return jnp.logical_and(j >= 0, smallest <= SB_SUFFIX_CUTOFF)

        first_block = first_tile + ti - 1

        def farther_block(carry, ti=ti, smallest_suffix=smallest_suffix, first_block=first_block):
            j, _ = carry
            counted = jnp.logical_and(j == first_block, key_pos >= far)
            for hh in range(nh):
                z2 = jnp.where(counted, MASKED, _dot(key_block(j, hh), qm[ti, hh]))
                sp2 = _softplus2(z2)
                inner = later_sums(sp2)
                w = jnp.exp2(z2 - sp2 - inner - suffix_ref[ti, hh])
                acc_ref[ti, hh] += _dot(values(j, hh), w.astype(jnp.bfloat16))
                suffix_ref[ti, hh] += inner[0:1] + sp2[0:1]
            return j - 1, smallest_suffix()

        lax.while_loop(more_to_do, farther_block, (first_block, smallest_suffix()))
        _store_heads(o_ref.at[:, ti * tile:(ti + 1) * tile], [acc_ref[ti, hh] for hh in range(nh)])


def _sb_attention(qt, k, vt):
    b, d, s = qt.shape
    nh = SB_HEADS
    nt = SB_TILES
    gw = nh * HEAD_DIM
    n_blk = s // ATT_TILE
    assert n_blk % nt == 0
    later = jnp.asarray(np.triu(np.ones((ATT_TILE, ATT_TILE), np.float32), 1), jnp.bfloat16)
    later_near = jnp.asarray(np.triu(np.ones((SB_NEAR_KEYS, SB_NEAR_KEYS), np.float32), 1), jnp.bfloat16)
    return pl.pallas_call(
        functools.partial(_sb_kernel, nh=nh, nt=nt),
        grid=(b, d // gw, n_blk // nt),
        in_specs=[
            pl.BlockSpec((ATT_TILE, ATT_TILE), lambda i, j, t: (0, 0)),
            pl.BlockSpec((SB_NEAR_KEYS, SB_NEAR_KEYS), lambda i, j, t: (0, 0)),
            pl.BlockSpec((1, gw, nt * ATT_TILE), lambda i, j, t: (i, j, t)),
            pl.BlockSpec((1, s, gw), lambda i, j, t: (i, 0, j)),
            pl.BlockSpec((1, n_blk, gw, ATT_TILE), lambda i, j, t: (i, 0, j, 0)),
        ],
        out_specs=pl.BlockSpec((1, nt * ATT_TILE, gw), lambda i, j, t: (i, t, j)),
        out_shape=jax.ShapeDtypeStruct((b, s, d), jnp.bfloat16),
        scratch_shapes=[
            pltpu.VMEM((nt, nh, 1, ATT_TILE), jnp.float32),
            pltpu.VMEM((nt, nh, HEAD_DIM, ATT_TILE), jnp.float32),
        ],
        compiler_params=pltpu.CompilerParams(
            dimension_semantics=("arbitrary", "arbitrary", "arbitrary"),
            vmem_limit_bytes=VMEM_LIMIT),
        name="stick_breaking_attention",
    )(later, later_near, qt, k, vt)


def kernel(x, norm_g, w_in, w_out, final_g):
    b, s, d = x.shape
    depth = norm_g.shape[0]
    assert depth >= 1 and d == N_HEADS * HEAD_DIM and s % PROJ_ROWS == 0 and PROJ_ROWS % ATT_TILE == 0
    assert s % WIDE_ROWS == 0 and WIDE_ROWS % ATT_TILE == 0
    slopes = jnp.asarray(2.0 ** (-8.0 * np.arange(1, N_HEADS + 1) / N_HEADS), jnp.float32)

    w_out_bf = w_out.astype(jnp.bfloat16)

    def proj_weights(layer):
        return w_in, layer, layer + 1 < depth

    h = x
    outs = _layer_boundary(h, norm_g[0], proj=proj_weights(0), rows=WIDE_ROWS)
    for i in range(depth):
        last = i + 1 == depth
        k, qt, vt = outs[0], outs[-2], outs[-1]
        gate = (norm_g[i], w_in) if last else outs[1]
        o_parts = _moba_attention(qt, k, vt, slopes) if i % 2 == 0 else [_sb_attention(qt, k, vt)]
        attn = (o_parts, gate, w_out_bf, i)
        if last:
            rows = WIDE_ROWS if len(o_parts) == 1 else PROJ_ROWS
            (h,) = _layer_boundary(h, final_g, attn=attn, rows=rows)
        else:
            h, *outs = _layer_boundary(h, norm_g[i + 1], attn=attn, proj=proj_weights(i + 1))
    return h
```
